```python
import math
import jax, jax.numpy as jnp
from jax import lax
import numpy as np

D_MODEL = 1024
BATCH = 8
SEQ = 2048
DEPTH = 2
DEC_BATCH = 128
DEC_SEQ = 4
PAST_LEN = 16384
PAGE_SIZE = 128

N_META = 16
H_A = 4
DK_A = 128
DV_A = 128
QK_A = H_A * DK_A
W_A = H_A * DV_A
H_B = 4
DK_B = 128
DV_B = 128
W_B = H_B * DV_B
CONV_B = 4
D_FF = 2816
CONV_F = 3
CHUNK = 64
ALPHA = (2 * DEPTH) ** 0.25
BETA_INIT = (8 * DEPTH) ** -0.25
LN_EPS = 1e-5
NORM_EPS = 1e-6

A_Q = 0
A_K = A_Q + QK_A
A_V = A_K + QK_A
A_O = A_V + W_A
A_I = A_O + W_A
B_QKV = A_I + 2 * H_A
B_Z = B_QKV + 3 * W_B
B_BETA = B_Z + W_B
B_A = B_BETA + H_B
G_MERGE = B_A + H_B
N_IN = G_MERGE + 2 * D_MODEL

kernel_name = 'hybrid_mlstm_gdn_convffn_step'


def layer_norm(x, g, b):
    xf = x.astype(jnp.float32)
    mu = xf.mean(-1, keepdims=True)
    var = jnp.square(xf - mu).mean(-1, keepdims=True)
    return ((xf - mu) * lax.rsqrt(var + LN_EPS) * g + b).astype(x.dtype)


def head_layer_norm(h):
    mu = h.mean(-1, keepdims=True)
    var = jnp.square(h - mu).mean(-1, keepdims=True)
    return (h - mu) * lax.rsqrt(var + NORM_EPS)


def rms_norm(h):
    return h * lax.rsqrt(jnp.square(h).mean(-1, keepdims=True) + NORM_EPS)


def l2norm(h):
    return h * lax.rsqrt(jnp.square(h).sum(-1, keepdims=True) + NORM_EPS)


def causal_dwconv(u, buf, w):
    width = w.shape[0]
    L = u.shape[1]
    ext = jnp.concatenate([buf.astype(u.dtype), u], axis=1)
    out = ext[:, 0:L] * w[0]
    for j in range(1, width):
        out = out + ext[:, j:j + L] * w[j]
    return out, ext[:, L:]


def _to_chunks(a, c):
    B, H, L = a.shape[:3]
    a = a.reshape((B, H, L // c, c) + a.shape[3:])
    return jnp.moveaxis(a, 2, 0)


def _from_chunks(a):
    a = jnp.moveaxis(a, 0, 2)
    return a.reshape(a.shape[:2] + (a.shape[2] * a.shape[3],) + a.shape[4:])


def run_segments(step, carry, xs, seg_lens):
    outs = []
    start = 0
    for L in seg_lens:
        c = math.gcd(L, CHUNK)
        seg = tuple(_to_chunks(a[:, :, start:start + L], c) for a in xs)
        carry, o = lax.scan(step, carry, seg)
        outs.append(_from_chunks(o))
        start += L
    return jnp.concatenate(outs, axis=2), carry


def _mlstm_step(carry, xs):
    C, n, m = carry
    q, k, v, ig, lf = xs
    c = q.shape[2]
    incl = jnp.tril(jnp.ones((c, c), dtype=bool))
    b = jnp.cumsum(lf, axis=-1)
    D = jnp.where(incl, b[..., :, None] - b[..., None, :] + ig[..., None, :], -jnp.inf)
    inter = b + m[..., None]
    m_t = jnp.maximum(inter, D.max(-1))
    w_intra = jnp.exp(D - m_t[..., None])
    w_inter = jnp.exp(inter - m_t)
    s = jnp.einsum('bhtd,bhsd->bhts', q, k) * w_intra
    num = jnp.einsum('bhts,bhsv->bhtv', s, v) + w_inter[..., None] * jnp.einsum('bhtd,bhdv->bhtv', q, C)
    den = s.sum(-1) + w_inter * jnp.einsum('bhtd,bhd->bht', q, n)
    h = num / jnp.maximum(jnp.abs(den), jnp.exp(-m_t))[..., None]
    m_new = m_t[..., -1]
    w_src = jnp.exp(b[..., -1:] - b + ig - m_new[..., None])
    w_old = jnp.exp(b[..., -1] + m - m_new)
    kw = k * w_src[..., None]
    C_new = w_old[..., None, None] * C + jnp.einsum('bhsd,bhsv->bhdv', kw, v)
    n_new = w_old[..., None] * n + kw.sum(2)
    return (C_new, n_new, m_new), h


def _gdn_step(S, xs):
    q, k, v, beta, g = xs
    c = q.shape[2]
    incl = jnp.tril(jnp.ones((c, c), dtype=bool))
    strict = jnp.tril(jnp.ones((c, c), dtype=bool), -1)
    G = jnp.cumsum(g, axis=-1)
    decay = jnp.exp(jnp.where(incl, G[..., :, None] - G[..., None, :], -jnp.inf))
    A = jnp.where(strict, beta[..., None] * jnp.einsum('bhtd,bhsd->bhts', k, k) * decay, 0.0)
    rhs = jnp.concatenate([beta[..., None] * v, (beta * jnp.exp(G))[..., None] * k], axis=-1)
    sol = lax.linalg.triangular_solve(jnp.eye(c, dtype=A.dtype) + A, rhs, left_side=True, lower=True,
                                      unit_diagonal=True)
    dv = v.shape[-1]
    U = sol[..., :dv] - jnp.einsum('bhtd,bhdv->bhtv', sol[..., dv:], S)
    o = (jnp.exp(G)[..., None] * jnp.einsum('bhtd,bhdv->bhtv', q, S)
         + jnp.einsum('bhts,bhsv->bhtv', jnp.einsum('bhtd,bhsd->bhts', q, k) * decay, U))
    G_end = G[..., -1]
    S_new = (jnp.exp(G_end)[..., None, None] * S
             + jnp.einsum('bhsd,bhsv->bhdv', k * jnp.exp(G_end[..., None] - G)[..., None], U))
    return S_new, o


def token_mixers(x, state, w_in, gate_bias, a_norm_w, conv_w, A_log, dt_bias, b_norm_w,
                 w_pa, w_pb, w_out, seg_lens):
    C0, n0, m0, S0, conv_buf = state
    B, L, _ = x.shape
    f32 = jnp.float32
    p = x @ w_in

    def heads(a, h):
        return a.reshape(B, L, h, -1).transpose(0, 2, 1, 3).astype(f32)

    qa = heads(p[..., A_Q:A_K], H_A) * DK_A ** -0.5
    ka = heads(p[..., A_K:A_V], H_A)
    va = heads(p[..., A_V:A_O], H_A)
    gif = (p[..., A_I:B_QKV] + gate_bias).astype(f32).transpose(0, 2, 1)
    ig = gif[:, :H_A]
    lf = jax.nn.log_sigmoid(gif[:, H_A:])
    hA, (C, n, m) = run_segments(_mlstm_step, (C0.astype(f32), n0.astype(f32), m0.astype(f32)),
                                 (qa, ka, va, ig, lf), seg_lens)
    hA = head_layer_norm(hA.transpose(0, 2, 1, 3)).reshape(B, L, W_A) * a_norm_w
    hA = (jax.nn.sigmoid(p[..., A_O:A_I].astype(f32)) * hA).astype(x.dtype)

    qkv, conv_new = causal_dwconv(p[..., B_QKV:B_Z], conv_buf, conv_w)
    qkv = jax.nn.silu(qkv)
    qb = l2norm(heads(qkv[..., :W_B], H_B)) * DK_B ** -0.5
    kb = l2norm(heads(qkv[..., W_B:2 * W_B], H_B))
    vb = heads(qkv[..., 2 * W_B:], H_B)
    beta = jax.nn.sigmoid(p[..., B_BETA:B_A].astype(f32)).transpose(0, 2, 1)
    g = (-jnp.exp(A_log.astype(f32))
         * jax.nn.softplus(p[..., B_A:G_MERGE].astype(f32) + dt_bias)).transpose(0, 2, 1)
    hB, S = run_segments(_gdn_step, S0.astype(f32), (qb, kb, vb, beta, g), seg_lens)
    hB = rms_norm(hB.transpose(0, 2, 1, 3)) * b_norm_w
    hB = (hB.reshape(B, L, W_B) * jax.nn.silu(p[..., B_Z:B_BETA].astype(f32))).astype(x.dtype)

    mg = jax.nn.sigmoid(p[..., G_MERGE:])
    y = mg[..., :D_MODEL] * (hA @ w_pa) + mg[..., D_MODEL:] * (hB @ w_pb)
    return y @ w_out, (C, n, m, S, conv_new)


def conv_ffn(x, w_up, conv_w, w_down, buf):
    u = x @ w_up
    u, new_buf = causal_dwconv(u, buf, conv_w)
    h = jax.nn.silu(u[..., :D_FF]) * u[..., D_FF:]
    return h @ w_down, new_buf


def decoder_layer(x, st, lp, seg_lens):
    (w_in, gate_bias, a_norm_w, conv_w, A_log, dt_bias, b_norm_w, w_pa, w_pb, w_out,
     ln1_g, ln1_b, w_up, f_conv_w, w_down, ln2_g, ln2_b) = lp
    C0, n0, m0, S0, gbuf, fbuf = st
    mix, (C, n, m, S, gbuf_new) = token_mixers(x, (C0, n0, m0, S0, gbuf), w_in, gate_bias, a_norm_w,
                                               conv_w, A_log, dt_bias, b_norm_w, w_pa, w_pb, w_out,
                                               seg_lens)
    x = layer_norm(ALPHA * x + mix, ln1_g, ln1_b)
    ff, fbuf_new = conv_ffn(x, w_up, f_conv_w, w_down, fbuf)
    x = layer_norm(ALPHA * x + ff, ln2_g, ln2_b)
    return x, (C, n, m, S, gbuf_new, fbuf_new)


def setup_inputs(seed: int = 0) -> dict:
    key = jax.random.key(seed)
    ks = jax.random.split(key, 32)
    f32 = jnp.float32

    def nrm(k, shape, s):
        return jax.random.normal(k, shape, f32) * s

    dt = jax.random.uniform(ks[14], (DEPTH, H_B), f32, 0.001, 0.1)
    return {
        'x_prompt': nrm(ks[0], (BATCH, SEQ, D_MODEL), 1.0),
        'x_sample': nrm(ks[1], (DEC_BATCH, DEC_SEQ, D_MODEL), 1.0),
        'state_mlstm_C': nrm(ks[2], (DEPTH, DEC_BATCH, H_A, DK_A, DV_A), 0.1),
        'state_mlstm_n': nrm(ks[3], (DEPTH, DEC_BATCH, H_A, DK_A), 0.1),
        'state_mlstm_m': jax.random.uniform(ks[4], (DEPTH, DEC_BATCH, H_A), f32, 0.0, 3.0),
        'state_gdn_S': nrm(ks[5], (DEPTH, DEC_BATCH, H_B, DK_B, DV_B), DK_B ** -0.5),
        'state_gdn_conv': nrm(ks[6], (DEPTH, DEC_BATCH, CONV_B - 1, 3 * W_B), 1.0),
        'state_ffn_conv': nrm(ks[7], (DEPTH, DEC_BATCH, CONV_F - 1, 2 * D_FF), 1.0),
        'meta_tokens': nrm(ks[8], (N_META, D_MODEL), 1.0),
        'ln_emb_g': 1.0 + nrm(ks[9], (D_MODEL,), 0.01),
        'ln_emb_b': nrm(ks[10], (D_MODEL,), 0.01),
        'w_in': nrm(ks[11], (DEPTH, D_MODEL, N_IN), D_MODEL ** -0.5),
        'mlstm_gate_bias': jnp.concatenate([nrm(ks[12], (DEPTH, H_A), 0.1),
                                            3.0 + nrm(ks[13], (DEPTH, H_A), 0.1)], axis=-1),
        'mlstm_norm_w': 1.0 + nrm(ks[15], (DEPTH, W_A), 0.01),
        'gdn_conv_w': nrm(ks[16], (DEPTH, CONV_B, 3 * W_B), CONV_B ** -0.5),
        'gdn_A_log': jnp.log(jax.random.uniform(ks[17], (DEPTH, H_B), f32, 1.0, 16.0)),
        'gdn_dt_bias': jnp.log(jnp.expm1(dt)),
        'gdn_norm_w': 1.0 + nrm(ks[18], (DEPTH, DV_B), 0.01),
        'w_branch_a': nrm(ks[19], (DEPTH, W_A, D_MODEL), W_A ** -0.5),
        'w_branch_b': nrm(ks[20], (DEPTH, W_B, D_MODEL), W_B ** -0.5),
        'w_out': nrm(ks[21], (DEPTH, D_MODEL, D_MODEL), BETA_INIT * D_MODEL ** -0.5),
        'ln1_g': 1.0 + nrm(ks[22], (DEPTH, D_MODEL), 0.01),
        'ln1_b': nrm(ks[23], (DEPTH, D_MODEL), 0.01),
        'w_up': nrm(ks[24], (DEPTH, D_MODEL, 2 * D_FF), D_MODEL ** -0.5),
        'ffn_conv_w': nrm(ks[25], (DEPTH, CONV_F, 2 * D_FF), CONV_F ** -0.5),
        'w_down': nrm(ks[26], (DEPTH, D_FF, D_MODEL), BETA_INIT * D_FF ** -0.5),
        'ln2_g': 1.0 + nrm(ks[27], (DEPTH, D_MODEL), 0.01),
        'ln2_b': nrm(ks[28], (DEPTH, D_MODEL), 0.01),
    }


def reference(x_prompt, x_sample, state_mlstm_C, state_mlstm_n, state_mlstm_m, state_gdn_S,
              state_gdn_conv, state_ffn_conv, meta_tokens, ln_emb_g, ln_emb_b, w_in, mlstm_gate_bias,
              mlstm_norm_w, gdn_conv_w, gdn_A_log, gdn_dt_bias, gdn_norm_w, w_branch_a, w_branch_b,
              w_out, ln1_g, ln1_b, w_up, ffn_conv_w, w_down, ln2_g, ln2_b):
    f32 = jnp.float32
    B = x_prompt.shape[0]
    dt = x_prompt.dtype
    meta = jnp.broadcast_to(meta_tokens.astype(dt), (B, N_META, D_MODEL))
    xp = layer_norm(jnp.concatenate([meta, x_prompt], axis=1), ln_emb_g, ln_emb_b)
    xs = layer_norm(x_sample, ln_emb_g, ln_emb_b)
    prompt_segs = (N_META, x_prompt.shape[1])
    sample_segs = (x_sample.shape[1],)
    p_states = []
    s_states = []
    for l in range(DEPTH):
        lp = (w_in[l], mlstm_gate_bias[l], mlstm_norm_w[l], gdn_conv_w[l], gdn_A_log[l], gdn_dt_bias[l],
              gdn_norm_w[l], w_branch_a[l], w_branch_b[l], w_out[l], ln1_g[l], ln1_b[l], w_up[l],
              ffn_conv_w[l], w_down[l], ln2_g[l], ln2_b[l])
        zero_st = (jnp.zeros((B, H_A, DK_A, DV_A), f32), jnp.zeros((B, H_A, DK_A), f32),
                   jnp.zeros((B, H_A), f32), jnp.zeros((B, H_B, DK_B, DV_B), f32),
                   jnp.zeros((B, CONV_B - 1, 3 * W_B), dt), jnp.zeros((B, CONV_F - 1, 2 * D_FF), dt))
        samp_st = (state_mlstm_C[l], state_mlstm_n[l], state_mlstm_m[l], state_gdn_S[l],
                   state_gdn_conv[l], state_ffn_conv[l])
        xp, st_p = decoder_layer(xp, zero_st, lp, prompt_segs)
        xs, st_s = decoder_layer(xs, samp_st, lp, sample_segs)
        p_states.append(st_p)
        s_states.append(st_s)
    dtypes = (state_mlstm_C.dtype, state_mlstm_n.dtype, state_mlstm_m.dtype, state_gdn_S.dtype,
              state_gdn_conv.dtype, state_ffn_conv.dtype)

    def stack(states, i):
        return jnp.stack([s[i] for s in states], axis=0).astype(dtypes[i])

    y_prompt = xp[:, N_META:]
    y_sample = xs
    return (y_prompt, y_sample,
            stack(p_states, 0), stack(p_states, 1), stack(p_states, 2), stack(p_states, 3),
            stack(p_states, 4), stack(p_states, 5),
            stack(s_states, 0), stack(s_states, 1), stack(s_states, 2), stack(s_states, 3),
            stack(s_states, 4), stack(s_states, 5))
```

```python
import functools

import jax
import jax.numpy as jnp
from jax import lax
from jax.experimental import pallas as pl
from jax.experimental.pallas import tpu as pltpu

f32 = jnp.float32
bf16 = jnp.bfloat16

D_MODEL = 1024
N_META = 16
NH = 4
DH = 128
WB = NH * DH
CONV_B = 4
D_FF = 2816
CONV_F = 3
DEPTH = 2
ALPHA = (2 * DEPTH) ** 0.25
LN_EPS = 1e-5
NORM_EPS = 1e-6
QSCALE = DH ** -0.5

A_Q = 0
A_I = 4 * WB
B_QKV = A_I + 2 * NH
B_Z = B_QKV + 3 * WB
B_BETA = B_Z + WB
G_MERGE = B_BETA + 2 * NH

C_QA, C_KA, C_VA, C_OA = 0, WB, 2 * WB, 3 * WB
C_G = 4 * WB
C_QKV = C_G + 128
C_Z = C_QKV + 3 * WB
N_MIX = C_Z + WB + 128

CH = 64
HDR = 8
NEG = -1e30
FT = 256
NF = D_FF // FT
VMEM_LIMIT = 56 * 1024 * 1024


def _cparams(sem):
    return pltpu.CompilerParams(dimension_semantics=sem, vmem_limit_bytes=VMEM_LIMIT)


def _dot(a, b):
    return jnp.dot(a, b, preferred_element_type=f32)


def _dot_nt(a, b):
    return lax.dot_general(a, b, (((1,), (1,)), ((), ())), preferred_element_type=f32)


def _dot_tn(a, b):
    return lax.dot_general(a, b, (((0,), (0,)), ((), ())), preferred_element_type=f32)


def _sel_dot(sel, x):
    hi = x.astype(bf16)
    r1 = x - hi.astype(f32)
    mid = r1.astype(bf16)
    lo = (r1 - mid.astype(f32)).astype(bf16)
    return (_dot(sel, hi) + _dot(sel, mid)) + _dot(sel, lo)


def _softplus(x):
    return jnp.maximum(x, 0.0) + jnp.log1p(jnp.exp(-jnp.abs(x)))


def _sigmoid(x):
    return 1.0 / (1.0 + jnp.exp(-x))


def _silu(x):
    return x * _sigmoid(x)


def _ln_rows(x, g, b):
    mu = jnp.mean(x, axis=-1, keepdims=True)
    xc = x - mu
    var = jnp.mean(xc * xc, axis=-1, keepdims=True)
    return xc * lax.rsqrt(var + LN_EPS) * g + b


def _ln_kernel(x_ref, g_ref, b_ref, o_ref):
    o_ref[...] = _ln_rows(x_ref[...], g_ref[...], b_ref[...])


def _layer_norm(x, g, b):
    m = x.shape[0]
    tm = min(m, 1024)
    return pl.pallas_call(
        _ln_kernel,
        grid=(m // tm,),
        in_specs=[pl.BlockSpec((tm, D_MODEL), lambda i: (i, 0)),
                  pl.BlockSpec((1, D_MODEL), lambda i: (0, 0)),
                  pl.BlockSpec((1, D_MODEL), lambda i: (0, 0))],
        out_specs=pl.BlockSpec((tm, D_MODEL), lambda i: (i, 0)),
        out_shape=jax.ShapeDtypeStruct((m, D_MODEL), f32),
        compiler_params=_cparams(("arbitrary",)),
        name="embed_ln",
    )(x, g, b)


def _inproj_kernel(x_ref, w_ref, o_ref):
    o_ref[...] = _dot(x_ref[...].astype(bf16), w_ref[...])


def _inproj(x, w_mix):
    m = x.shape[0]
    tm = min(m, 512)
    return pl.pallas_call(
        _inproj_kernel,
        grid=(m // tm,),
        in_specs=[pl.BlockSpec((tm, D_MODEL), lambda i: (i, 0)),
                  pl.BlockSpec((D_MODEL, N_MIX), lambda i: (0, 0))],
        out_specs=pl.BlockSpec((tm, N_MIX), lambda i: (i, 0)),
        out_shape=jax.ShapeDtypeStruct((m, N_MIX), f32),
        compiler_params=_cparams(("arbitrary",)),
        name="mixer_inproj",
    )(x, w_mix)


def _chunk_consts(blk, valid):
    lg = blk.bit_length() - 1
    r = lax.broadcasted_iota(jnp.int32, (CH, CH), 0)
    c = lax.broadcasted_iota(jnp.int32, (CH, CH), 1)
    same = (r >> lg) == (c >> lg)
    tri = jnp.logical_and(same, c <= r)
    stri = jnp.logical_and(same, c < r)
    sel = lambda cond: jnp.where(cond, 1.0, 0.0).astype(bf16)
    tri_m = sel(tri)
    blk_m = sel(same)
    last_m = sel(c == ((r >> lg) << lg) + (blk - 1))
    re = lax.broadcasted_iota(jnp.int32, (CH, 8), 0)
    ce = lax.broadcasted_iota(jnp.int32, (CH, 8), 1)
    expand_m = sel(ce == (re >> lg))
    rs = lax.broadcasted_iota(jnp.int32, (8, CH), 0)
    cs = lax.broadcasted_iota(jnp.int32, (8, CH), 1)
    rowsel_m = sel(cs == (rs << lg) + (blk - 1))
    rv = lax.broadcasted_iota(jnp.int32, (CH, 128), 0)
    valid_m = (rv & (blk - 1)) >= (blk - valid)
    return dict(tri=tri, stri=stri, tri_m=tri_m, blk_m=blk_m, last_m=last_m,
                expand_m=expand_m, rowsel_m=rowsel_m, valid_m=valid_m)


def _mix_chunk(blk, valid, k, pr, qkv, par, st, out):
    nseq = CH // blk
    gbias, alog, anw, bnw = par
    c_ref, n_ref, m_ref, s_ref = st
    ha_ref, hb_ref, rows = out
    tri, stri = k["tri"], k["stri"]

    lane = lax.broadcasted_iota(jnp.int32, (CH, 128), 1)
    graw = pr(C_G, 128) + gbias
    gt = jnp.where(lane < 4, graw,
                   jnp.where(lane < 8, -_softplus(-graw),
                             jnp.where(lane < 12, _sigmoid(graw),
                                       jnp.where(lane < 16, -jnp.exp(alog) * _softplus(graw), 0.0))))
    if valid < blk:
        gt = jnp.where(k["valid_m"], gt, jnp.where(lane < 4, NEG, 0.0))
    cs = _sel_dot(k["tri_m"], gt)
    tot = _sel_dot(k["blk_m"], gt)
    cum_lane = jnp.logical_or(jnp.logical_and(lane >= 4, lane < 8), lane >= 12)
    mix_t = jnp.transpose(jnp.where(cum_lane, cs, gt))

    b_all = pltpu.roll(cs, 124, 1)
    btot_all = pltpu.roll(tot, 124, 1)
    mprev_all = _sel_dot(k["expand_m"], m_ref[...])
    d_list = []
    dmax_all = jnp.zeros((CH, 128), f32)
    for h in range(NH):
        dm = jnp.where(tri, cs[:, 4 + h:5 + h] - mix_t[4 + h:5 + h, :] + mix_t[h:h + 1, :], NEG)
        d_list.append(dm)
        dmax_all = jnp.where(lane == h, jnp.max(dm, axis=1, keepdims=True), dmax_all)
    inter_all = b_all + mprev_all
    mt_all = jnp.maximum(inter_all, dmax_all)
    winter_all = jnp.exp(inter_all - mt_all)
    emt_all = jnp.exp(-mt_all)
    mnew_all = _sel_dot(k["last_m"], mt_all)
    wsrc_all = jnp.exp(btot_all - b_all + gt - mnew_all)
    wold_all = jnp.exp(btot_all + mprev_all - mnew_all)
    m_ref[...] = _sel_dot(k["rowsel_m"], mnew_all)

    for h in range(NH):
        q = pr(C_QA + h * DH, DH) * QSCALE
        kk = pr(C_KA + h * DH, DH)
        v = pr(C_VA + h * DH, DH)
        qb, kb, vb = q.astype(bf16), kk.astype(bf16), v.astype(bf16)
        mt = mt_all[:, h:h + 1]
        winter = winter_all[:, h:h + 1]
        s = _dot_nt(qb, kb) * jnp.exp(d_list[h] - mt)
        num = _dot(s.astype(bf16), vb)
        den = jnp.sum(s, axis=1, keepdims=True)
        qc, qn = [], []
        for i in range(nseq):
            sl = slice(i * blk, (i + 1) * blk)
            qc.append(_dot(qb[sl], c_ref[i, h].astype(bf16)))
            qn.append(jnp.sum(q[sl] * n_ref[i, h:h + 1, :], axis=1, keepdims=True))
        qc = qc[0] if nseq == 1 else jnp.concatenate(qc, axis=0)
        qn = qn[0] if nseq == 1 else jnp.concatenate(qn, axis=0)
        num = num + winter * qc
        den = den + winter * qn
        hraw = num / jnp.maximum(jnp.abs(den), emt_all[:, h:h + 1])
        kw = kk * wsrc_all[:, h:h + 1]
        kwb = kw.astype(bf16)
        for i in range(nseq):
            sl = slice(i * blk, (i + 1) * blk)
            last = (i + 1) * blk - 1
            wold = wold_all[last:last + 1, h:h + 1]
            c_ref[i, h] = wold * c_ref[i, h] + _dot_tn(kwb[sl], vb[sl])
            n_ref[i, h:h + 1, :] = wold * n_ref[i, h:h + 1, :] + jnp.sum(kw[sl], axis=0, keepdims=True)
        mu = jnp.mean(hraw, axis=1, keepdims=True)
        hc = hraw - mu
        var = jnp.mean(hc * hc, axis=1, keepdims=True)
        hn = hc * lax.rsqrt(var + NORM_EPS) * anw[:, h * DH:(h + 1) * DH]
        ha_ref[rows, h * DH:(h + 1) * DH] = _sigmoid(pr(C_OA + h * DH, DH)) * hn

    expg_all = jnp.exp(cs)
    expdiff_all = jnp.exp(tot - cs)
    expgtot_all = jnp.exp(tot)
    for h in range(NH):
        qs = qkv(h * DH, DH)
        ks = qkv(WB + h * DH, DH)
        v = qkv(2 * WB + h * DH, DH)
        q = qs * lax.rsqrt(jnp.sum(qs * qs, axis=1, keepdims=True) + NORM_EPS) * QSCALE
        kk = ks * lax.rsqrt(jnp.sum(ks * ks, axis=1, keepdims=True) + NORM_EPS)
        qb, kb = q.astype(bf16), kk.astype(bf16)
        beta = gt[:, 8 + h:9 + h]
        expg = expg_all[:, 12 + h:13 + h]
        decay = jnp.exp(jnp.where(tri, cs[:, 12 + h:13 + h] - mix_t[12 + h:13 + h, :], NEG))
        a = jnp.where(stri, beta * _dot_nt(kb, kb) * decay, 0.0)
        x = jnp.concatenate([beta * v, (beta * expg) * kk], axis=1)
        pw = -a
        nsteps = blk.bit_length() - 1
        for it in range(nsteps):
            pwb = pw.astype(bf16)
            x = x + _dot(pwb, x.astype(bf16))
            if it + 1 < nsteps:
                pw = _dot(pwb, pwb)
        sol_v, sol_k = x[:, :DH], x[:, DH:]
        solkb = sol_k.astype(bf16)
        sks, qss = [], []
        for i in range(nseq):
            sl = slice(i * blk, (i + 1) * blk)
            sb = s_ref[i, h].astype(bf16)
            sks.append(_dot(solkb[sl], sb))
            qss.append(_dot(qb[sl], sb))
        sks = sks[0] if nseq == 1 else jnp.concatenate(sks, axis=0)
        qss = qss[0] if nseq == 1 else jnp.concatenate(qss, axis=0)
        u = sol_v - sks
        ub = u.astype(bf16)
        o = expg * qss + _dot((_dot_nt(qb, kb) * decay).astype(bf16), ub)
        kdb = (kk * expdiff_all[:, 12 + h:13 + h]).astype(bf16)
        for i in range(nseq):
            sl = slice(i * blk, (i + 1) * blk)
            last = (i + 1) * blk - 1
            s_ref[i, h] = expgtot_all[last:last + 1, 12 + h:13 + h] * s_ref[i, h] + _dot_tn(kdb[sl], ub[sl])
        on = o * lax.rsqrt(jnp.mean(o * o, axis=1, keepdims=True) + NORM_EPS) * bnw
        hb_ref[rows, h * DH:(h + 1) * DH] = on * _silu(pr(C_Z + h * DH, DH))


def _conv_silu(ext_ref, cw_ref, dst_ref, t):
    for cb in range(3 * WB // 128):
        cols = slice(cb * 128, (cb + 1) * 128)
        acc = ext_ref[HDR - 3:HDR - 3 + t, cols] * cw_ref[0:1, cols]
        for j in range(1, CONV_B):
            acc = acc + ext_ref[HDR - 3 + j:HDR - 3 + j + t, cols] * cw_ref[j:j + 1, cols]
        dst_ref[:, cols] = _silu(acc)


def _mixer_long_kernel(t, p_ref, hist_ref, c0_ref, n0_ref, m0_ref, s0_ref,
                       gb_ref, al_ref, anw_ref, bnw_ref, cw_ref,
                       ha_ref, hb_ref, c_ref, n_ref, m_ref, s_ref, hist_out_ref,
                       ext_ref, qkv_ref):
    g = pl.program_id(1)

    @pl.when(g == 0)
    def _():
        ext_ref[0:HDR, :] = hist_ref[0]
        c_ref[...] = c0_ref[...]
        n_ref[...] = n0_ref[...]
        m_ref[...] = m0_ref[...]
        s_ref[...] = s0_ref[...]

    ext_ref[HDR:HDR + t, :] = p_ref[:, C_QKV:C_QKV + 3 * WB]
    _conv_silu(ext_ref, cw_ref, qkv_ref, t)
    hist = ext_ref[t:t + HDR, :]
    ext_ref[0:HDR, :] = hist
    hist_out_ref[0] = hist

    k = _chunk_consts(CH, CH)
    par = (gb_ref[...], al_ref[...], anw_ref[...], bnw_ref[...])

    def body(ci, carry):
        rows = pl.ds(pl.multiple_of(ci * CH, CH), CH)
        _mix_chunk(CH, CH, k,
                   lambda c0, n: p_ref[rows, c0:c0 + n],
                   lambda c0, n: qkv_ref[rows, c0:c0 + n],
                   par, (c_ref, n_ref, _M0(m_ref), s_ref), (ha_ref, hb_ref, rows))
        return carry

    lax.fori_loop(0, t // CH, body, 0)


class _M0:
    def __init__(self, ref):
        self.ref = ref

    def __getitem__(self, idx):
        return self.ref[0]

    def __setitem__(self, idx, val):
        self.ref[0] = val


def _mixer_long(p, hist, c0, n0, m0, s0, gb, al, anw, bnw, cw, nb, seq, t):
    nt = seq // t
    row_spec = lambda w: pl.BlockSpec((t, w), lambda b, g: (b * nt + g, 0))
    const = lambda shape: pl.BlockSpec(shape, lambda b, g: (0,) * len(shape))
    perb = lambda shape: pl.BlockSpec((1,) + shape, lambda b, g: (b,) + (0,) * len(shape))
    return pl.pallas_call(
        functools.partial(_mixer_long_kernel, t),
        grid=(nb, nt),
        in_specs=[row_spec(N_MIX), const((1, HDR, 3 * WB)),
                  const((1, NH, DH, DH)), const((1, NH, DH)), const((1, 8, 128)), const((1, NH, DH, DH)),
                  const((1, 128)), const((1, 128)), const((1, WB)), const((1, DH)), const((CONV_B, 3 * WB))],
        out_specs=[row_spec(WB), row_spec(WB),
                   perb((NH, DH, DH)), perb((NH, DH)), perb((8, 128)), perb((NH, DH, DH)),
                   perb((HDR, 3 * WB))],
        out_shape=[jax.ShapeDtypeStruct((nb * seq, WB), f32), jax.ShapeDtypeStruct((nb * seq, WB), f32),
                   jax.ShapeDtypeStruct((nb, NH, DH, DH), f32), jax.ShapeDtypeStruct((nb, NH, DH), f32),
                   jax.ShapeDtypeStruct((nb, 8, 128), f32), jax.ShapeDtypeStruct((nb, NH, DH, DH), f32),
                   jax.ShapeDtypeStruct((nb, HDR, 3 * WB), f32)],
        scratch_shapes=[pltpu.VMEM((t + HDR, 3 * WB), f32), pltpu.VMEM((t, 3 * WB), f32)],
        compiler_params=_cparams(("arbitrary", "arbitrary")),
        name="mixer_long",
    )(p, hist, c0, n0, m0, s0, gb, al, anw, bnw, cw)


def _mixer_block_kernel(blk, valid, p_ref, e_ref, c0_ref, n0_ref, m0_ref, s0_ref,
                        gb_ref, al_ref, anw_ref, bnw_ref, cw_ref,
                        ha_ref, hb_ref, c_ref, n_ref, m_ref, s_ref, ext_out_ref,
                        ext_ref, qkv_ref):
    c_ref[...] = c0_ref[...]
    n_ref[...] = n0_ref[...]
    m_ref[...] = m0_ref[...]
    s_ref[...] = s0_ref[...]
    k = _chunk_consts(blk, valid)
    rv = lax.broadcasted_iota(jnp.int32, (CH, 3 * WB), 0)
    ext = jnp.where((rv & (blk - 1)) >= (blk - valid), p_ref[:, C_QKV:C_QKV + 3 * WB], e_ref[...])
    ext_ref[0:HDR, :] = jnp.zeros((HDR, 3 * WB), f32)
    ext_ref[HDR:HDR + CH, :] = ext
    ext_out_ref[...] = ext
    _conv_silu(ext_ref, cw_ref, qkv_ref, CH)
    par = (gb_ref[...], al_ref[...], anw_ref[...], bnw_ref[...])
    rows = slice(0, CH)
    _mix_chunk(blk, valid, k,
               lambda c0, n: p_ref[:, c0:c0 + n],
               lambda c0, n: qkv_ref[:, c0:c0 + n],
               par, (c_ref, n_ref, _M0(m_ref), s_ref), (ha_ref, hb_ref, rows))


def _mixer_block(p, e, c0, n0, m0, s0, gb, al, anw, bnw, cw, blk, valid):
    ng = p.shape[0] // CH
    nseq = CH // blk
    row_spec = lambda w: pl.BlockSpec((CH, w), lambda g: (g, 0))
    const = lambda shape: pl.BlockSpec(shape, lambda g: (0,) * len(shape))
    perg = lambda shape: pl.BlockSpec(shape, lambda g: (g,) + (0,) * (len(shape) - 1))
    return pl.pallas_call(
        functools.partial(_mixer_block_kernel, blk, valid),
        grid=(ng,),
        in_specs=[row_spec(N_MIX), row_spec(3 * WB),
                  perg((nseq, NH, DH, DH)), perg((nseq, NH, DH)), perg((1, 8, 128)), perg((nseq, NH, DH, DH)),
                  const((1, 128)), const((1, 128)), const((1, WB)), const((1, DH)), const((CONV_B, 3 * WB))],
        out_specs=[row_spec(WB), row_spec(WB),
                   perg((nseq, NH, DH, DH)), perg((nseq, NH, DH)), perg((1, 8, 128)), perg((nseq, NH, DH, DH)),
                   row_spec(3 * WB)],
        out_shape=[jax.ShapeDtypeStruct((ng * CH, WB), f32), jax.ShapeDtypeStruct((ng * CH, WB), f32),
                   jax.ShapeDtypeStruct((ng * nseq, NH, DH, DH), f32), jax.ShapeDtypeStruct((ng * nseq, NH, DH), f32),
                   jax.ShapeDtypeStruct((ng, 8, 128), f32), jax.ShapeDtypeStruct((ng * nseq, NH, DH, DH), f32),
                   jax.ShapeDtypeStruct((ng * CH, 3 * WB), f32)],
        scratch_shapes=[pltpu.VMEM((CH + HDR, 3 * WB), f32), pltpu.VMEM((CH, 3 * WB), f32)],
        compiler_params=_cparams(("arbitrary",)),
        name="mixer_block",
    )(p, e, c0, n0, m0, s0, gb, al, anw, bnw, cw)


def _outproj_kernel(x_ref, ha_ref, hb_ref, wm_ref, wpa_ref, wpb_ref, wo_ref, g_ref, b_ref, o_ref):
    x = x_ref[...]
    mg = _sigmoid(_dot(x.astype(bf16), wm_ref[...]))
    ya = _dot(ha_ref[...].astype(bf16), wpa_ref[...])
    yb = _dot(hb_ref[...].astype(bf16), wpb_ref[...])
    y = mg[:, :D_MODEL] * ya + mg[:, D_MODEL:] * yb
    mix = _dot(y.astype(bf16), wo_ref[...])
    o_ref[...] = _ln_rows(ALPHA * x + mix, g_ref[...], b_ref[...])


def _outproj(x, ha, hb, w_merge, w_pa, w_pb, w_out, g, b):
    m = x.shape[0]
    tm = min(m, 512)
    row_spec = lambda w: pl.BlockSpec((tm, w), lambda i: (i, 0))
    const = lambda shape: pl.BlockSpec(shape, lambda i: (0, 0))
    return pl.pallas_call(
        _outproj_kernel,
        grid=(m // tm,),
        in_specs=[row_spec(D_MODEL), row_spec(WB), row_spec(WB),
                  const((D_MODEL, 2 * D_MODEL)), const((WB, D_MODEL)), const((WB, D_MODEL)),
                  const((D_MODEL, D_MODEL)), const((1, D_MODEL)), const((1, D_MODEL))],
        out_specs=row_spec(D_MODEL),
        out_shape=jax.ShapeDtypeStruct((m, D_MODEL), f32),
        compiler_params=_cparams(("arbitrary",)),
        name="merge_outproj_ln",
    )(x, ha, hb, w_merge, w_pa, w_pb, w_out, g, b)


def _ffn_kernel(long_mode, tm, blk, valid, *refs):
    if long_mode:
        (x_ref, wa_ref, wb_ref, cwa_ref, cwb_ref, wd_ref, g_ref, b_ref, ha_ref, hbuf_ref,
         o_ref, sa_ref, sb_ref, exta_ref, extb_ref, ca_ref, cb_ref) = refs
    else:
        (x_ref, wa_ref, wb_ref, cwa_ref, cwb_ref, wd_ref, g_ref, b_ref, ea_ref, eb_ref,
         o_ref, sa_ref, sb_ref, exta_ref, extb_ref) = refs
    mt = pl.program_id(1)
    f = pl.program_id(2)
    xb = x_ref[...].astype(bf16)
    ua = _dot(xb, wa_ref[...])
    ub = _dot(xb, wb_ref[...])
    if long_mode:
        @pl.when(mt == 0)
        def _():
            exta_ref[0:HDR, :] = ha_ref[0]
            extb_ref[0:HDR, :] = hbuf_ref[0]

        @pl.when(mt > 0)
        def _():
            exta_ref[0:HDR, :] = ca_ref[f]
            extb_ref[0:HDR, :] = cb_ref[f]
    else:
        rv = lax.broadcasted_iota(jnp.int32, (tm, FT), 0)
        vm = (rv & (blk - 1)) >= (blk - valid)
        ua = jnp.where(vm, ua, ea_ref[...])
        ub = jnp.where(vm, ub, eb_ref[...])
        exta_ref[0:HDR, :] = jnp.zeros((HDR, FT), f32)
        extb_ref[0:HDR, :] = jnp.zeros((HDR, FT), f32)
        sa_ref[...] = ua
        sb_ref[...] = ub
    exta_ref[HDR:HDR + tm, :] = ua
    extb_ref[HDR:HDR + tm, :] = ub
    if long_mode:
        ta = exta_ref[tm:tm + HDR, :]
        tb = extb_ref[tm:tm + HDR, :]
        ca_ref[f] = ta
        cb_ref[f] = tb
        sa_ref[0, 0] = ta
        sb_ref[0, 0] = tb

    def conv(ext_ref, cw_ref):
        acc = ext_ref[HDR - 2:HDR - 2 + tm, :] * cw_ref[0:1, :]
        for j in range(1, CONV_F):
            acc = acc + ext_ref[HDR - 2 + j:HDR - 2 + j + tm, :] * cw_ref[j:j + 1, :]
        return acc

    hh = _silu(conv(exta_ref, cwa_ref)) * conv(extb_ref, cwb_ref)
    part = _dot(hh.astype(bf16), wd_ref[...])

    @pl.when(f == 0)
    def _():
        o_ref[...] = part

    @pl.when(f > 0)
    def _():
        o_ref[...] = o_ref[...] + part

    @pl.when(f == NF - 1)
    def _():
        o_ref[...] = _ln_rows(ALPHA * x_ref[...] + o_ref[...], g_ref[...], b_ref[...])


def _ffn(x, w_up_a, w_up_b, cw_a, cw_b, w_down, g, b, hist_a=None, hist_b=None,
         e_a=None, e_b=None, nb=1, blk=CH, valid=CH):
    m = x.shape[0]
    long_mode = hist_a is not None
    seq = m // nb
    tm = min(seq, 1024)
    nmt = seq // tm
    rows = lambda w: pl.BlockSpec((tm, w), lambda bb, mt, f: (bb * nmt + mt, 0))
    in_specs = [rows(D_MODEL),
                pl.BlockSpec((D_MODEL, FT), lambda bb, mt, f: (0, f)),
                pl.BlockSpec((D_MODEL, FT), lambda bb, mt, f: (0, f)),
                pl.BlockSpec((CONV_F, FT), lambda bb, mt, f: (0, f)),
                pl.BlockSpec((CONV_F, FT), lambda bb, mt, f: (0, f)),
                pl.BlockSpec((FT, D_MODEL), lambda bb, mt, f: (f, 0)),
                pl.BlockSpec((1, D_MODEL), lambda bb, mt, f: (0, 0)),
                pl.BlockSpec((1, D_MODEL), lambda bb, mt, f: (0, 0))]
    scratch = [pltpu.VMEM((tm + HDR, FT), f32), pltpu.VMEM((tm + HDR, FT), f32)]
    if long_mode:
        in_specs += [pl.BlockSpec((1, HDR, FT), lambda bb, mt, f: (0, 0, f))] * 2
        st_spec = pl.BlockSpec((1, 1, HDR, FT), lambda bb, mt, f: (bb, mt, 0, f))
        st_shape = jax.ShapeDtypeStruct((nb, nmt, HDR, D_FF), f32)
        scratch += [pltpu.VMEM((NF, HDR, FT), f32), pltpu.VMEM((NF, HDR, FT), f32)]
        extra = (hist_a, hist_b)
    else:
        in_specs += [pl.BlockSpec((tm, FT), lambda bb, mt, f: (bb * nmt + mt, f))] * 2
        st_spec = pl.BlockSpec((tm, FT), lambda bb, mt, f: (bb * nmt + mt, f))
        st_shape = jax.ShapeDtypeStruct((m, D_FF), f32)
        extra = (e_a, e_b)
    return pl.pallas_call(
        functools.partial(_ffn_kernel, long_mode, tm, blk, valid),
        grid=(nb, nmt, NF),
        in_specs=in_specs,
        out_specs=[rows(D_MODEL), st_spec, st_spec],
        out_shape=[jax.ShapeDtypeStruct((m, D_MODEL), f32), st_shape, st_shape],
        scratch_shapes=scratch,
        compiler_params=_cparams(("arbitrary", "arbitrary", "arbitrary")),
        name="conv_ffn_ln",
    )(x, w_up_a, w_up_b, cw_a, cw_b, w_down, g, b, *extra)


def _layer_weights(l, w_in, mlstm_gate_bias, mlstm_norm_w, gdn_conv_w, gdn_A_log, gdn_dt_bias, gdn_norm_w,
                   w_branch_a, w_branch_b, w_out, ln1_g, ln1_b, w_up, ffn_conv_w, w_down, ln2_g, ln2_b):
    w = w_in[l]
    zc = lambda n: jnp.zeros((D_MODEL, n), w.dtype)
    w_mix = jnp.concatenate([w[:, A_Q:A_I], w[:, A_I:B_QKV], w[:, B_BETA:G_MERGE], zc(128 - 4 * NH),
                             w[:, B_QKV:B_Z], w[:, B_Z:B_BETA], zc(128)], axis=1).astype(bf16)
    z4 = jnp.zeros((NH,), f32)
    gb = jnp.concatenate([mlstm_gate_bias[l], z4, gdn_dt_bias[l], jnp.zeros((128 - 4 * NH,), f32)])[None]
    al = jnp.concatenate([z4, z4, z4, gdn_A_log[l], jnp.zeros((128 - 4 * NH,), f32)])[None]
    return dict(
        w_mix=w_mix, w_merge=w[:, G_MERGE:].astype(bf16), gb=gb, al=al,
        anw=mlstm_norm_w[l][None], bnw=gdn_norm_w[l][None], cw=gdn_conv_w[l],
        w_pa=w_branch_a[l].astype(bf16), w_pb=w_branch_b[l].astype(bf16), w_out=w_out[l].astype(bf16),
        ln1_g=ln1_g[l][None], ln1_b=ln1_b[l][None],
        w_up_a=w_up[l][:, :D_FF].astype(bf16), w_up_b=w_up[l][:, D_FF:].astype(bf16),
        cw_a=ffn_conv_w[l][:, :D_FF], cw_b=ffn_conv_w[l][:, D_FF:],
        w_down=w_down[l].astype(bf16), ln2_g=ln2_g[l][None], ln2_b=ln2_b[l][None])


def _block_layer(x, lw, st, blk, valid):
    c0, n0, m0, s0, gbuf, fbuf = st
    nseq_tot = c0.shape[0]
    nseq = CH // blk
    ng = nseq_tot // nseq
    front = blk - valid
    m0p = jnp.pad(m0.reshape(ng, nseq, NH), ((0, 0), (0, 8 - nseq), (0, 128 - NH)))
    e = jnp.pad(gbuf, ((0, 0), (front - (CONV_B - 1), valid), (0, 0))).reshape(nseq_tot * blk, 3 * WB)
    ef = jnp.pad(fbuf, ((0, 0), (front - (CONV_F - 1), valid), (0, 0))).reshape(nseq_tot * blk, 2 * D_FF)
    p = _inproj(x, lw["w_mix"])
    ha, hb, c, n, m, s, ext = _mixer_block(p, e, c0, n0, m0p, s0, lw["gb"], lw["al"], lw["anw"], lw["bnw"],
                                           lw["cw"], blk, valid)
    x1 = _outproj(x, ha, hb, lw["w_merge"], lw["w_pa"], lw["w_pb"], lw["w_out"], lw["ln1_g"], lw["ln1_b"])
    x2, ua, ub = _ffn(x1, lw["w_up_a"], lw["w_up_b"], lw["cw_a"], lw["cw_b"], lw["w_down"],
                      lw["ln2_g"], lw["ln2_b"], e_a=ef[:, :D_FF], e_b=ef[:, D_FF:], blk=blk, valid=valid)
    m_new = m[:, :nseq, :NH].reshape(nseq_tot, NH)
    gconv = ext.reshape(nseq_tot, blk, 3 * WB)[:, blk - (CONV_B - 1):]
    fconv = jnp.concatenate([ua, ub], axis=1).reshape(nseq_tot, blk, 2 * D_FF)[:, blk - (CONV_F - 1):]
    return x2, (c, n, m_new, s, gconv, fconv)


def _long_layer(x, lw, st, nb, seq):
    c0, n0, m0, s0, gbuf, fbuf = st
    m0p = jnp.pad(m0.reshape(1, 1, NH), ((0, 0), (0, 7), (0, 128 - NH)))
    hist = jnp.pad(gbuf, ((0, 0), (HDR - (CONV_B - 1), 0), (0, 0)))
    hf = jnp.pad(fbuf, ((0, 0), (HDR - (CONV_F - 1), 0), (0, 0)))
    p = _inproj(x, lw["w_mix"])
    ha, hb, c, n, m, s, hist_out = _mixer_long(p, hist, c0, n0, m0p, s0, lw["gb"], lw["al"], lw["anw"],
                                               lw["bnw"], lw["cw"], nb, seq, 512)
    x1 = _outproj(x, ha, hb, lw["w_merge"], lw["w_pa"], lw["w_pb"], lw["w_out"], lw["ln1_g"], lw["ln1_b"])
    x2, sa, sb = _ffn(x1, lw["w_up_a"], lw["w_up_b"], lw["cw_a"], lw["cw_b"], lw["w_down"],
                      lw["ln2_g"], lw["ln2_b"], hist_a=hf[:, :, :D_FF], hist_b=hf[:, :, D_FF:], nb=nb)
    gconv = hist_out[:, HDR - (CONV_B - 1):]
    fconv = jnp.concatenate([sa[:, -1], sb[:, -1]], axis=2)[:, HDR - (CONV_F - 1):]
    return x2, (c, n, m[:, 0, :NH], s, gconv, fconv)


def kernel(x_prompt, x_sample, state_mlstm_C, state_mlstm_n, state_mlstm_m, state_gdn_S, state_gdn_conv, state_ffn_conv, meta_tokens, ln_emb_g, ln_emb_b, w_in, mlstm_gate_bias, mlstm_norm_w, gdn_conv_w, gdn_A_log, gdn_dt_bias, gdn_norm_w, w_branch_a, w_branch_b, w_out, ln1_g, ln1_b, w_up, ffn_conv_w, w_down, ln2_g, ln2_b):
    nb, seq, _ = x_prompt.shape
    ns, ls, _ = x_sample.shape
    sblk = 8
    assert seq % 512 == 0 and ls + CONV_B - 1 <= sblk and ns % (CH // sblk) == 0 and N_META + CONV_B - 1 <= CH
    lws = [_layer_weights(l, w_in, mlstm_gate_bias, mlstm_norm_w, gdn_conv_w, gdn_A_log, gdn_dt_bias,
                          gdn_norm_w, w_branch_a, w_branch_b, w_out, ln1_g, ln1_b, w_up, ffn_conv_w,
                          w_down, ln2_g, ln2_b) for l in range(DEPTH)]
    eg, eb = ln_emb_g[None], ln_emb_b[None]

    xm = _layer_norm(jnp.pad(meta_tokens, ((CH - N_META, 0), (0, 0))), eg, eb)
    xs = _layer_norm(jnp.pad(x_sample, ((0, 0), (sblk - ls, 0), (0, 0))).reshape(ns * sblk, D_MODEL), eg, eb)
    xp = _layer_norm(x_prompt.reshape(nb * seq, D_MODEL), eg, eb)

    zero_st = (jnp.zeros((1, NH, DH, DH), f32), jnp.zeros((1, NH, DH), f32), jnp.zeros((1, NH), f32),
               jnp.zeros((1, NH, DH, DH), f32), jnp.zeros((1, CONV_B - 1, 3 * WB), f32),
               jnp.zeros((1, CONV_F - 1, 2 * D_FF), f32))
    p_states, s_states = [], []
    for l in range(DEPTH):
        xm, st_m = _block_layer(xm, lws[l], zero_st, CH, N_META)
        xp, st_p = _long_layer(xp, lws[l], st_m, nb, seq)
        samp_st = (state_mlstm_C[l], state_mlstm_n[l], state_mlstm_m[l], state_gdn_S[l],
                   state_gdn_conv[l], state_ffn_conv[l])
        xs, st_s = _block_layer(xs, lws[l], samp_st, sblk, ls)
        p_states.append(st_p)
        s_states.append(st_s)

    stack = lambda states, i: jnp.stack([s[i] for s in states], axis=0)
    y_prompt = xp.reshape(nb, seq, D_MODEL)
    y_sample = xs.reshape(ns, sblk, D_MODEL)[:, sblk - ls:]
    return (y_prompt, y_sample,
            stack(p_states, 0), stack(p_states, 1), stack(p_states, 2), stack(p_states, 3),
            stack(p_states, 4), stack(p_states, 5),
            stack(s_states, 0), stack(s_states, 1), stack(s_states, 2), stack(s_states, 3),
            stack(s_states, 4), stack(s_states, 5))
```

```python
import functools

import jax
import jax.numpy as jnp
from jax import lax
from jax.experimental import pallas as pl
from jax.experimental.pallas import tpu as pltpu

f32 = jnp.float32
bf16 = jnp.bfloat16

D_MODEL = 1024
N_META = 16
NH = 4
DH = 128
WB = NH * DH
CONV_B = 4
D_FF = 2816
CONV_F = 3
DEPTH = 2
ALPHA = (2 * DEPTH) ** 0.25
LN_EPS = 1e-5
NORM_EPS = 1e-6
QSCALE = DH ** -0.5

A_Q = 0
A_I = 4 * WB
B_QKV = A_I + 2 * NH
B_Z = B_QKV + 3 * WB
B_BETA = B_Z + WB
G_MERGE = B_BETA + 2 * NH

C_QA, C_KA, C_VA, C_OA = 0, WB, 2 * WB, 3 * WB
C_G = 4 * WB
C_QKV = C_G + 128
C_Z = C_QKV + 3 * WB
N_MIX = C_Z + WB + 128

CH = 64
A_GROUP = 8
HDR = 8
NEG = -1e30
FT = 256
NF = D_FF // FT
FFN_TM_BLOCK = 256
FFN_TM = 512
VMEM_LIMIT = 56 * 1024 * 1024


def _cparams(sem):
    return pltpu.CompilerParams(dimension_semantics=sem, vmem_limit_bytes=VMEM_LIMIT)


def _dot(a, b):
    return jnp.dot(a, b, preferred_element_type=f32)


def _dot_nt(a, b):
    return lax.dot_general(a, b, (((1,), (1,)), ((), ())), preferred_element_type=f32)


def _dot_tn(a, b):
    return lax.dot_general(a, b, (((0,), (0,)), ((), ())), preferred_element_type=f32)


def _sel_dot(sel, x):
    hi = x.astype(bf16)
    r1 = x - hi.astype(f32)
    mid = r1.astype(bf16)
    lo = (r1 - mid.astype(f32)).astype(bf16)
    return (_dot(sel, hi) + _dot(sel, mid)) + _dot(sel, lo)


def _softplus(x):
    return jnp.maximum(x, 0.0) + jnp.log1p(jnp.exp(-jnp.abs(x)))


def _sigmoid(x):
    return 1.0 / (1.0 + jnp.exp(-x))


def _silu(x):
    return x * _sigmoid(x)


def _ln_rows(x, g, b):
    mu = jnp.mean(x, axis=-1, keepdims=True)
    xc = x - mu
    var = jnp.mean(xc * xc, axis=-1, keepdims=True)
    return xc * lax.rsqrt(var + LN_EPS) * g + b


def _ln_kernel(x_ref, g_ref, b_ref, o_ref):
    o_ref[...] = _ln_rows(x_ref[...], g_ref[...], b_ref[...])


def _layer_norm(x, g, b):
    m = x.shape[0]
    tm = min(m, 1024)
    return pl.pallas_call(
        _ln_kernel,
        grid=(m // tm,),
        in_specs=[pl.BlockSpec((tm, D_MODEL), lambda i: (i, 0)),
                  pl.BlockSpec((1, D_MODEL), lambda i: (0, 0)),
                  pl.BlockSpec((1, D_MODEL), lambda i: (0, 0))],
        out_specs=pl.BlockSpec((tm, D_MODEL), lambda i: (i, 0)),
        out_shape=jax.ShapeDtypeStruct((m, D_MODEL), f32),
        compiler_params=_cparams(("arbitrary",)),
        name="embed_ln",
    )(x, g, b)


def _inproj_kernel(x_ref, w_ref, o_ref):
    o_ref[...] = _dot(x_ref[...].astype(bf16), w_ref[...])


def _inproj(x, w_mix):
    m = x.shape[0]
    tm = min(m, 512)
    return pl.pallas_call(
        _inproj_kernel,
        grid=(m // tm,),
        in_specs=[pl.BlockSpec((tm, D_MODEL), lambda i: (i, 0)),
                  pl.BlockSpec((D_MODEL, N_MIX), lambda i: (0, 0))],
        out_specs=pl.BlockSpec((tm, N_MIX), lambda i: (i, 0)),
        out_shape=jax.ShapeDtypeStruct((m, N_MIX), f32),
        compiler_params=_cparams(("arbitrary",)),
        name="mixer_inproj",
    )(x, w_mix)


def _chunk_consts(blk, valid):
    lg = blk.bit_length() - 1
    r = lax.broadcasted_iota(jnp.int32, (CH, CH), 0)
    c = lax.broadcasted_iota(jnp.int32, (CH, CH), 1)
    same = (r >> lg) == (c >> lg)
    tri = jnp.logical_and(same, c <= r)
    stri = jnp.logical_and(same, c < r)
    sel = lambda cond: jnp.where(cond, 1.0, 0.0).astype(bf16)
    tri_m = sel(tri)
    blk_m = sel(same)
    last_m = sel(c == ((r >> lg) << lg) + (blk - 1))
    re = lax.broadcasted_iota(jnp.int32, (CH, 8), 0)
    ce = lax.broadcasted_iota(jnp.int32, (CH, 8), 1)
    expand_m = sel(ce == (re >> lg))
    rs = lax.broadcasted_iota(jnp.int32, (8, CH), 0)
    cs = lax.broadcasted_iota(jnp.int32, (8, CH), 1)
    rowsel_m = sel(cs == (rs << lg) + (blk - 1))
    rv = lax.broadcasted_iota(jnp.int32, (CH, 128), 0)
    valid_m = (rv & (blk - 1)) >= (blk - valid)
    return dict(tri=tri, stri=stri, tri_m=tri_m, blk_m=blk_m, last_m=last_m,
                expand_m=expand_m, rowsel_m=rowsel_m, valid_m=valid_m)


def _block_max(x, blk):
    if blk == CH:
        return jnp.broadcast_to(jnp.max(x, axis=0, keepdims=True), x.shape)
    x3 = x.reshape(CH // blk, blk, 128)
    return jnp.broadcast_to(jnp.max(x3, axis=1, keepdims=True), x3.shape).reshape(CH, 128)


def _mix_scratch(nc, nseq):
    return [pltpu.VMEM((nc, 6, CH, 128), f32),
            pltpu.VMEM((nc, NH, CH, DH), f32),
            pltpu.VMEM((nc, nseq * NH, DH, DH), f32),
            pltpu.VMEM((nc, nseq, 8, DH), f32),
            pltpu.VMEM((nc, nseq * NH, DH, DH), bf16),
            pltpu.VMEM((nc, nseq * NH, DH, DH), f32),
            pltpu.VMEM((nc, NH, CH, DH), bf16),
            pltpu.VMEM((nc, NH, CH, DH), f32)]


def _phase_a(blk, valid, k, chunks, par, sc):
    nseq = CH // blk
    gbias, alog = par[0], par[1]
    t_ref, numl_ref, kv_ref, nv_ref, m1_ref, m2_ref, qeff_ref, o2_ref = sc
    tri, stri = k["tri"], k["stri"]
    seqs = [slice(i * blk, (i + 1) * blk) for i in range(nseq)]
    nch = len(chunks)
    units = [(c, h) for c in range(nch) for h in range(NH)]
    lane = lax.broadcasted_iota(jnp.int32, (CH, 128), 1)
    cum_lane = jnp.logical_or(jnp.logical_and(lane >= 4, lane < 8), lane >= 12)

    gt = []
    for ci, pr, qkv in chunks:
        graw = pr(C_G, 128) + gbias
        g = jnp.where(lane < 4, graw,
                      jnp.where(lane < 8, -_softplus(-graw),
                                jnp.where(lane < 12, _sigmoid(graw),
                                          jnp.where(lane < 16, -jnp.exp(alog) * _softplus(graw), 0.0))))
        if valid < blk:
            g = jnp.where(k["valid_m"], g, jnp.where(lane < 4, NEG, 0.0))
        gt.append(g)
    cs = [_sel_dot(k["tri_m"], g) for g in gt]
    if nseq == 1:
        tot = [jnp.broadcast_to(x[CH - 1:CH, :], (CH, 128)) for x in cs]
    else:
        tot = [_sel_dot(k["blk_m"], g) for g in gt]
    mix_t = [jnp.transpose(jnp.where(cum_lane, cs[c], gt[c])) for c in range(nch)]
    b_all = [pltpu.roll(x, 124, 1) for x in cs]
    btot_all = [pltpu.roll(x, 124, 1) for x in tot]
    wlog = [btot_all[c] - b_all[c] + gt[c] for c in range(nch)]
    mloc_all = [_block_max(x, blk) for x in wlog]
    wsrc_all = [jnp.exp(wlog[c] - mloc_all[c]) for c in range(nch)]
    expg_all = [jnp.exp(x) for x in cs]
    expdiff_all = [jnp.exp(tot[c] - cs[c]) for c in range(nch)]

    qa, ka, va, qs, ks, vg = {}, {}, {}, {}, {}, {}
    for c, h in units:
        _, pr, qkv = chunks[c]
        qa[c, h] = (pr(C_QA + h * DH, DH) * QSCALE).astype(bf16)
        ka[c, h] = pr(C_KA + h * DH, DH)
        va[c, h] = pr(C_VA + h * DH, DH).astype(bf16)
        qs[c, h] = qkv(h * DH, DH)
        ks[c, h] = qkv(WB + h * DH, DH)
        vg[c, h] = qkv(2 * WB + h * DH, DH)
    qss = {u: jnp.sum(qs[u] * qs[u], axis=1, keepdims=True) for u in units}
    kss = {u: jnp.sum(ks[u] * ks[u], axis=1, keepdims=True) for u in units}
    qg = {u: qs[u] * lax.rsqrt(qss[u] + NORM_EPS) * QSCALE for u in units}
    kgf = {u: ks[u] * lax.rsqrt(kss[u] + NORM_EPS) for u in units}
    qgb = {u: qg[u].astype(bf16) for u in units}
    kg = {u: kgf[u].astype(bf16) for u in units}

    sq = {u: _dot_nt(qa[u], ka[u].astype(bf16)) for u in units}
    kk = {u: _dot_nt(kg[u], kg[u]) for u in units}
    qk = {u: _dot_nt(qgb[u], kg[u]) for u in units}

    dm = {(c, h): jnp.where(tri, cs[c][:, 4 + h:5 + h] - mix_t[c][4 + h:5 + h, :] + mix_t[c][h:h + 1, :], NEG)
          for c, h in units}
    dmax = {u: jnp.max(dm[u], axis=1, keepdims=True) for u in units}
    s = {u: sq[u] * jnp.exp(dm[u] - dmax[u]) for u in units}
    denl = {u: jnp.sum(s[u], axis=1, keepdims=True) for u in units}
    sloc = {u: s[u].astype(bf16) for u in units}
    kwl = {(c, h): ka[c, h] * wsrc_all[c][:, h:h + 1] for c, h in units}
    for c in range(nch):
        ci = chunks[c][0]
        dmax_all = jnp.zeros((CH, 128), f32)
        denl_all = jnp.zeros((CH, 128), f32)
        for h in range(NH):
            dmax_all = jnp.where(lane == h, dmax[c, h], dmax_all)
            denl_all = jnp.where(lane == h, denl[c, h], denl_all)
        t_ref[ci, 0] = gt[c]
        t_ref[ci, 1] = b_all[c]
        t_ref[ci, 2] = btot_all[c]
        t_ref[ci, 3] = dmax_all
        t_ref[ci, 4] = denl_all
        t_ref[ci, 5] = mloc_all[c]

    xs, pw, qkd, kd = {}, {}, {}, {}
    for c, h in units:
        beta = gt[c][:, 8 + h:9 + h]
        decay = jnp.exp(jnp.where(tri, cs[c][:, 12 + h:13 + h] - mix_t[c][12 + h:13 + h, :], NEG))
        pw[c, h] = -jnp.where(stri, beta * kk[c, h] * decay, 0.0)
        xs[c, h] = jnp.concatenate([beta * vg[c, h], (beta * expg_all[c][:, 12 + h:13 + h]) * kgf[c, h]], axis=1)
        qkd[c, h] = (qk[c, h] * decay).astype(bf16)
        kd[c, h] = (kgf[c, h] * expdiff_all[c][:, 12 + h:13 + h]).astype(bf16)

    for c, h in units:
        ci = chunks[c][0]
        numl_ref[ci, h] = _dot(sloc[c, h], va[c, h])
        kwb = kwl[c, h].astype(bf16)
        for i, sl in enumerate(seqs):
            kv_ref[ci, i * NH + h] = _dot_tn(kwb[sl], va[c, h][sl])
            nv_ref[ci, i, h:h + 1, :] = jnp.sum(kwl[c, h][sl], axis=0, keepdims=True)

    nsteps = blk.bit_length() - 1
    for it in range(nsteps):
        pwb = {u: pw[u].astype(bf16) for u in units}
        xs = {u: xs[u] + _dot(pwb[u], xs[u].astype(bf16)) for u in units}
        if it + 1 < nsteps:
            pw = {u: _dot(pwb[u], pwb[u]) for u in units}

    xb = {u: xs[u].astype(bf16) for u in units}
    qx = {u: _dot(qkd[u], xb[u]) for u in units}
    for c, h in units:
        ci = chunks[c][0]
        o2_ref[ci, h] = qx[c, h][:, :DH]
        qeff_ref[ci, h] = (expg_all[c][:, 12 + h:13 + h] * qg[c, h] - qx[c, h][:, DH:]).astype(bf16)
        for i, sl in enumerate(seqs):
            mm = _dot_tn(kd[c, h][sl], xb[c, h][sl])
            m2_ref[ci, i * NH + h] = mm[:, :DH]
            m1_ref[ci, i * NH + h] = mm[:, DH:].astype(bf16)


def _phase_b(blk, k, pr, par, st, sc, out, ci):
    nseq = CH // blk
    anw, bnw = par[2], par[3]
    c_ref, n_ref, m_ref, s_ref = st
    t_ref, numl_ref, kv_ref, nv_ref, m1_ref, m2_ref, qeff_ref, o2_ref = sc
    ha_ref, hb_ref, rows = out
    seqs = [slice(i * blk, (i + 1) * blk) for i in range(nseq)]
    heads = range(NH)
    gt, b_all, btot_all = t_ref[ci, 0], t_ref[ci, 1], t_ref[ci, 2]
    dmax_all, denl_all, mloc_all = t_ref[ci, 3], t_ref[ci, 4], t_ref[ci, 5]

    qa = [pr(C_QA + h * DH, DH) * QSCALE for h in heads]
    qc, ms, oq = [], [], []
    for h in heads:
        qb = qa[h].astype(bf16)
        qc.append([_dot(qb[sl], c_ref[i, h].astype(bf16)) for i, sl in enumerate(seqs)])
    for h in heads:
        qe = qeff_ref[ci, h]
        ms_h, oq_h = [], []
        for i, sl in enumerate(seqs):
            sb = s_ref[i, h].astype(bf16)
            ms_h.append(_dot(m1_ref[ci, i * NH + h], sb))
            oq_h.append(_dot(qe[sl], sb))
        ms.append(ms_h)
        oq.append(oq_h)

    if nseq == 1:
        mprev_all = jnp.broadcast_to(m_ref[0:1, :], (CH, 128))
    else:
        mprev_all = _sel_dot(k["expand_m"], m_ref[...])
    inter = b_all + mprev_all
    mt = jnp.maximum(inter, dmax_all)
    scale_all = jnp.exp(dmax_all - mt)
    winter_all = jnp.exp(inter - mt)
    emt_all = jnp.exp(-mt)
    carry = btot_all + mprev_all
    mnew = jnp.maximum(carry, mloc_all)
    sc2_all = jnp.exp(mloc_all - mnew)
    wold_all = jnp.exp(carry - mnew)
    egt_all = jnp.exp(btot_all)
    if nseq == 1:
        m_ref[0:1, :] = mnew[CH - 1:CH, :]
    else:
        m_ref[...] = _sel_dot(k["rowsel_m"], mnew)

    cat = lambda parts: parts[0] if nseq == 1 else jnp.concatenate(parts, axis=0)
    qn = [cat([jnp.sum(qa[h][sl] * n_ref[i, h:h + 1, :], axis=1, keepdims=True) for i, sl in enumerate(seqs)])
          for h in heads]
    hraw = []
    for h in heads:
        scale, winter = scale_all[:, h:h + 1], winter_all[:, h:h + 1]
        num = scale * numl_ref[ci, h] + winter * cat(qc[h])
        den = scale * denl_all[:, h:h + 1] + winter * qn[h]
        hraw.append(num / jnp.maximum(jnp.abs(den), emt_all[:, h:h + 1]))
    o = [cat(oq[h]) + o2_ref[ci, h] for h in heads]

    for h in heads:
        for i in range(nseq):
            last = (i + 1) * blk - 1
            wold = wold_all[last:last + 1, h:h + 1]
            sc2 = sc2_all[last:last + 1, h:h + 1]
            c_ref[i, h] = wold * c_ref[i, h] + sc2 * kv_ref[ci, i * NH + h]
            n_ref[i, h:h + 1, :] = wold * n_ref[i, h:h + 1, :] + sc2 * nv_ref[ci, i, h:h + 1, :]
            s_ref[i, h] = (egt_all[last:last + 1, 8 + h:9 + h] * s_ref[i, h] - ms[h][i]) + m2_ref[ci, i * NH + h]

    mu = [jnp.mean(x, axis=1, keepdims=True) for x in hraw]
    osq = [jnp.mean(x * x, axis=1, keepdims=True) for x in o]
    hc = [hraw[h] - mu[h] for h in heads]
    var = [jnp.mean(x * x, axis=1, keepdims=True) for x in hc]
    for h in heads:
        hn = hc[h] * lax.rsqrt(var[h] + NORM_EPS) * anw[:, h * DH:(h + 1) * DH]
        ha_ref[rows, h * DH:(h + 1) * DH] = _sigmoid(pr(C_OA + h * DH, DH)) * hn
        on = o[h] * lax.rsqrt(osq[h] + NORM_EPS) * bnw
        hb_ref[rows, h * DH:(h + 1) * DH] = on * _silu(pr(C_Z + h * DH, DH))


def _conv_silu(ext_ref, cw_ref, dst_ref, t):
    for cb in range(3 * WB // 128):
        cols = slice(cb * 128, (cb + 1) * 128)
        acc = ext_ref[HDR - 3:HDR - 3 + t, cols] * cw_ref[0:1, cols]
        for j in range(1, CONV_B):
            acc = acc + ext_ref[HDR - 3 + j:HDR - 3 + j + t, cols] * cw_ref[j:j + 1, cols]
        dst_ref[:, cols] = _silu(acc)


def _mixer_long_kernel(t, p_ref, hist_ref, c0_ref, n0_ref, m0_ref, s0_ref,
                       gb_ref, al_ref, anw_ref, bnw_ref, cw_ref,
                       ha_ref, hb_ref, c_ref, n_ref, m_ref, s_ref, hist_out_ref,
                       ext_ref, qkv_ref, *sc):
    g = pl.program_id(1)

    @pl.when(g == 0)
    def _():
        ext_ref[0:HDR, :] = hist_ref[0]
        c_ref[...] = c0_ref[...]
        n_ref[...] = n0_ref[...]
        m_ref[...] = m0_ref[...]
        s_ref[...] = s0_ref[...]

    ext_ref[HDR:HDR + t, :] = p_ref[:, C_QKV:C_QKV + 3 * WB]
    _conv_silu(ext_ref, cw_ref, qkv_ref, t)
    hist = ext_ref[t:t + HDR, :]
    ext_ref[0:HDR, :] = hist
    hist_out_ref[0] = hist

    k = _chunk_consts(CH, CH)
    par = (gb_ref[...], al_ref[...], anw_ref[...], bnw_ref[...])

    st = (c_ref, n_ref, _M0(m_ref), s_ref)

    def chunk_views(ci):
        rows = pl.ds(pl.multiple_of(ci * CH, CH), CH)
        return ci, (lambda c0, n: p_ref[rows, c0:c0 + n]), (lambda c0, n: qkv_ref[rows, c0:c0 + n])

    def body_a(j, carry):
        _phase_a(CH, CH, k, [chunk_views(A_GROUP * j + c) for c in range(A_GROUP)], par, sc)
        return carry

    def body_b(ci, carry):
        rows = pl.ds(pl.multiple_of(ci * CH, CH), CH)
        _phase_b(CH, k, lambda c0, n: p_ref[rows, c0:c0 + n], par, st, sc, (ha_ref, hb_ref, rows), ci)
        return carry

    lax.fori_loop(0, t // (CH * A_GROUP), body_a, 0)
    lax.fori_loop(0, t // CH, body_b, 0)


class _M0:
    def __init__(self, ref):
        self.ref = ref

    def __getitem__(self, idx):
        return self.ref[0] if idx is Ellipsis else self.ref[(0,) + idx]

    def __setitem__(self, idx, val):
        if idx is Ellipsis:
            self.ref[0] = val
        else:
            self.ref[(0,) + idx] = val


def _mixer_long(p, hist, c0, n0, m0, s0, gb, al, anw, bnw, cw, nb, seq, t):
    nt = seq // t
    row_spec = lambda w: pl.BlockSpec((t, w), lambda b, g: (b * nt + g, 0))
    const = lambda shape: pl.BlockSpec(shape, lambda b, g: (0,) * len(shape))
    perb = lambda shape: pl.BlockSpec((1,) + shape, lambda b, g: (b,) + (0,) * len(shape))
    return pl.pallas_call(
        functools.partial(_mixer_long_kernel, t),
        grid=(nb, nt),
        in_specs=[row_spec(N_MIX), const((1, HDR, 3 * WB)),
                  const((1, NH, DH, DH)), const((1, NH, DH)), const((1, 8, 128)), const((1, NH, DH, DH)),
                  const((1, 128)), const((1, 128)), const((1, WB)), const((1, DH)), const((CONV_B, 3 * WB))],
        out_specs=[row_spec(WB), row_spec(WB),
                   perb((NH, DH, DH)), perb((NH, DH)), perb((8, 128)), perb((NH, DH, DH)),
                   perb((HDR, 3 * WB))],
        out_shape=[jax.ShapeDtypeStruct((nb * seq, WB), f32), jax.ShapeDtypeStruct((nb * seq, WB), f32),
                   jax.ShapeDtypeStruct((nb, NH, DH, DH), f32), jax.ShapeDtypeStruct((nb, NH, DH), f32),
                   jax.ShapeDtypeStruct((nb, 8, 128), f32), jax.ShapeDtypeStruct((nb, NH, DH, DH), f32),
                   jax.ShapeDtypeStruct((nb, HDR, 3 * WB), f32)],
        scratch_shapes=[pltpu.VMEM((t + HDR, 3 * WB), f32), pltpu.VMEM((t, 3 * WB), f32)] + _mix_scratch(t // CH, 1),
        compiler_params=_cparams(("arbitrary", "arbitrary")),
        name="mixer_long",
    )(p, hist, c0, n0, m0, s0, gb, al, anw, bnw, cw)


def _mixer_block_kernel(blk, valid, p_ref, e_ref, c0_ref, n0_ref, m0_ref, s0_ref,
                        gb_ref, al_ref, anw_ref, bnw_ref, cw_ref,
                        ha_ref, hb_ref, c_ref, n_ref, m_ref, s_ref, ext_out_ref,
                        ext_ref, qkv_ref, *sc):
    c_ref[...] = c0_ref[...]
    n_ref[...] = n0_ref[...]
    m_ref[...] = m0_ref[...]
    s_ref[...] = s0_ref[...]
    k = _chunk_consts(blk, valid)
    rv = lax.broadcasted_iota(jnp.int32, (CH, 3 * WB), 0)
    ext = jnp.where((rv & (blk - 1)) >= (blk - valid), p_ref[:, C_QKV:C_QKV + 3 * WB], e_ref[...])
    ext_ref[0:HDR, :] = jnp.zeros((HDR, 3 * WB), f32)
    ext_ref[HDR:HDR + CH, :] = ext
    ext_out_ref[...] = ext
    _conv_silu(ext_ref, cw_ref, qkv_ref, CH)
    par = (gb_ref[...], al_ref[...], anw_ref[...], bnw_ref[...])
    pr = lambda c0, n: p_ref[:, c0:c0 + n]
    _phase_a(blk, valid, k, [(0, pr, lambda c0, n: qkv_ref[:, c0:c0 + n])], par, sc)
    _phase_b(blk, k, pr, par, (c_ref, n_ref, _M0(m_ref), s_ref), sc, (ha_ref, hb_ref, slice(0, CH)), 0)


def _mixer_block(p, e, c0, n0, m0, s0, gb, al, anw, bnw, cw, blk, valid):
    ng = p.shape[0] // CH
    nseq = CH // blk
    row_spec = lambda w: pl.BlockSpec((CH, w), lambda g: (g, 0))
    const = lambda shape: pl.BlockSpec(shape, lambda g: (0,) * len(shape))
    perg = lambda shape: pl.BlockSpec(shape, lambda g: (g,) + (0,) * (len(shape) - 1))
    return pl.pallas_call(
        functools.partial(_mixer_block_kernel, blk, valid),
        grid=(ng,),
        in_specs=[row_spec(N_MIX), row_spec(3 * WB),
                  perg((nseq, NH, DH, DH)), perg((nseq, NH, DH)), perg((1, 8, 128)), perg((nseq, NH, DH, DH)),
                  const((1, 128)), const((1, 128)), const((1, WB)), const((1, DH)), const((CONV_B, 3 * WB))],
        out_specs=[row_spec(WB), row_spec(WB),
                   perg((nseq, NH, DH, DH)), perg((nseq, NH, DH)), perg((1, 8, 128)), perg((nseq, NH, DH, DH)),
                   row_spec(3 * WB)],
        out_shape=[jax.ShapeDtypeStruct((ng * CH, WB), f32), jax.ShapeDtypeStruct((ng * CH, WB), f32),
                   jax.ShapeDtypeStruct((ng * nseq, NH, DH, DH), f32), jax.ShapeDtypeStruct((ng * nseq, NH, DH), f32),
                   jax.ShapeDtypeStruct((ng, 8, 128), f32), jax.ShapeDtypeStruct((ng * nseq, NH, DH, DH), f32),
                   jax.ShapeDtypeStruct((ng * CH, 3 * WB), f32)],
        scratch_shapes=[pltpu.VMEM((CH + HDR, 3 * WB), f32), pltpu.VMEM((CH, 3 * WB), f32)] + _mix_scratch(1, nseq),
        compiler_params=_cparams(("arbitrary",)),
        name="mixer_block",
    )(p, e, c0, n0, m0, s0, gb, al, anw, bnw, cw)


def _outproj_kernel(x_ref, ha_ref, hb_ref, wm_ref, wpa_ref, wpb_ref, wo_ref, g_ref, b_ref, o_ref):
    x = x_ref[...]
    mg = _sigmoid(_dot(x.astype(bf16), wm_ref[...]))
    ya = _dot(ha_ref[...].astype(bf16), wpa_ref[...])
    yb = _dot(hb_ref[...].astype(bf16), wpb_ref[...])
    y = mg[:, :D_MODEL] * ya + mg[:, D_MODEL:] * yb
    mix = _dot(y.astype(bf16), wo_ref[...])
    o_ref[...] = _ln_rows(ALPHA * x + mix, g_ref[...], b_ref[...])


def _outproj(x, ha, hb, w_merge, w_pa, w_pb, w_out, g, b):
    m = x.shape[0]
    tm = min(m, 512)
    row_spec = lambda w: pl.BlockSpec((tm, w), lambda i: (i, 0))
    const = lambda shape: pl.BlockSpec(shape, lambda i: (0, 0))
    return pl.pallas_call(
        _outproj_kernel,
        grid=(m // tm,),
        in_specs=[row_spec(D_MODEL), row_spec(WB), row_spec(WB),
                  const((D_MODEL, 2 * D_MODEL)), const((WB, D_MODEL)), const((WB, D_MODEL)),
                  const((D_MODEL, D_MODEL)), const((1, D_MODEL)), const((1, D_MODEL))],
        out_specs=row_spec(D_MODEL),
        out_shape=jax.ShapeDtypeStruct((m, D_MODEL), f32),
        compiler_params=_cparams(("arbitrary",)),
        name="merge_outproj_ln",
    )(x, ha, hb, w_merge, w_pa, w_pb, w_out, g, b)


def _conv3(u, h8, cw):
    p1 = pltpu.roll(u, 1, 0)
    p2 = pltpu.roll(u, 2, 0)
    if h8 is not None:
        row = lax.broadcasted_iota(jnp.int32, u.shape, 0)
        p1 = jnp.where(row == 0, h8[7:8], p1)
        p2 = jnp.where(row == 0, h8[6:7], jnp.where(row == 1, h8[7:8], p2))
    return (p2 * cw[0:1] + p1 * cw[1:2]) + u * cw[2:3]


def _ffn2_kernel(long_mode, tm, nmt, blk, valid, *refs):
    if long_mode:
        (x_ref, wa_ref, wb_ref, cwa_ref, cwb_ref, wd_ref, g_ref, b_ref, ha_ref, hb_ref,
         o_ref, sa_ref, sb_ref, hh_ref, ca_ref, cb_ref) = refs

        @pl.when(pl.program_id(0) % nmt == 0)
        def _():
            ca_ref[...] = ha_ref[0]
            cb_ref[...] = hb_ref[0]
    else:
        (x_ref, wa_ref, wb_ref, cwa_ref, cwb_ref, wd_ref, g_ref, b_ref, ea_ref, eb_ref,
         o_ref, sa_ref, sb_ref, hh_ref) = refs
        row = lax.broadcasted_iota(jnp.int32, (tm, FT), 0)
        vm = (row & (blk - 1)) >= (blk - valid)
    xb = x_ref[...].astype(bf16)
    for cb in range(NF):
        cols = slice(cb * FT, (cb + 1) * FT)
        ua = _dot(xb, wa_ref[:, cols])
        ub = _dot(xb, wb_ref[:, cols])
        if long_mode:
            h8a, h8b = ca_ref[:, cols], cb_ref[:, cols]
            ta, tb = ua[tm - HDR:tm], ub[tm - HDR:tm]
            ca_ref[:, cols] = ta
            cb_ref[:, cols] = tb
            sa_ref[0, :, cols] = ta
            sb_ref[0, :, cols] = tb
        else:
            h8a = h8b = None
            ua = jnp.where(vm, ua, ea_ref[:, cols])
            ub = jnp.where(vm, ub, eb_ref[:, cols])
            sa_ref[:, cols] = ua
            sb_ref[:, cols] = ub
        hh = _silu(_conv3(ua, h8a, cwa_ref[:, cols])) * _conv3(ub, h8b, cwb_ref[:, cols])
        hh_ref[:, cols] = hh.astype(bf16)
    out = _dot(hh_ref[...], wd_ref[...])
    o_ref[...] = _ln_rows(ALPHA * x_ref[...] + out, g_ref[...], b_ref[...])


def _ffn2(x, w_up_a, w_up_b, cw_a, cw_b, w_down, g, b, hist_a=None, hist_b=None,
          e_a=None, e_b=None, nb=1, blk=CH, valid=CH):
    m = x.shape[0]
    long_mode = hist_a is not None
    seq = m // nb
    tm = min(seq, FFN_TM if long_mode else FFN_TM_BLOCK)
    nmt = seq // tm
    rows = lambda w: pl.BlockSpec((tm, w), lambda i: (i, 0))
    const = lambda shape: pl.BlockSpec(shape, lambda i: (0,) * len(shape))
    in_specs = [rows(D_MODEL), const((D_MODEL, D_FF)), const((D_MODEL, D_FF)), const((CONV_F, D_FF)),
                const((CONV_F, D_FF)), const((D_FF, D_MODEL)), const((1, D_MODEL)), const((1, D_MODEL))]
    scratch = [pltpu.VMEM((tm, D_FF), bf16)]
    if long_mode:
        in_specs += [const((1, HDR, D_FF))] * 2
        st_spec = pl.BlockSpec((1, HDR, D_FF), lambda i: (i, 0, 0))
        st_shape = jax.ShapeDtypeStruct((m // tm, HDR, D_FF), f32)
        scratch += [pltpu.VMEM((HDR, D_FF), f32), pltpu.VMEM((HDR, D_FF), f32)]
        extra = (hist_a, hist_b)
    else:
        in_specs += [rows(D_FF)] * 2
        st_spec = rows(D_FF)
        st_shape = jax.ShapeDtypeStruct((m, D_FF), f32)
        extra = (e_a, e_b)
    return pl.pallas_call(
        functools.partial(_ffn2_kernel, long_mode, tm, nmt, blk, valid),
        grid=(m // tm,),
        in_specs=in_specs,
        out_specs=[rows(D_MODEL), st_spec, st_spec],
        out_shape=[jax.ShapeDtypeStruct((m, D_MODEL), f32), st_shape, st_shape],
        scratch_shapes=scratch,
        compiler_params=_cparams(("arbitrary",)),
        name="conv_ffn_ln",
    )(x, w_up_a, w_up_b, cw_a, cw_b, w_down, g, b, *extra)


def _layer_weights(l, w_in, mlstm_gate_bias, mlstm_norm_w, gdn_conv_w, gdn_A_log, gdn_dt_bias, gdn_norm_w,
                   w_branch_a, w_branch_b, w_out, ln1_g, ln1_b, w_up, ffn_conv_w, w_down, ln2_g, ln2_b):
    w = w_in[l]
    zc = lambda n: jnp.zeros((D_MODEL, n), w.dtype)
    w_mix = jnp.concatenate([w[:, A_Q:A_I], w[:, A_I:B_QKV], w[:, B_BETA:G_MERGE], zc(128 - 4 * NH),
                             w[:, B_QKV:B_Z], w[:, B_Z:B_BETA], zc(128)], axis=1).astype(bf16)
    z4 = jnp.zeros((NH,), f32)
    gb = jnp.concatenate([mlstm_gate_bias[l], z4, gdn_dt_bias[l], jnp.zeros((128 - 4 * NH,), f32)])[None]
    al = jnp.concatenate([z4, z4, z4, gdn_A_log[l], jnp.zeros((128 - 4 * NH,), f32)])[None]
    return dict(
        w_mix=w_mix, w_merge=w[:, G_MERGE:].astype(bf16), gb=gb, al=al,
        anw=mlstm_norm_w[l][None], bnw=gdn_norm_w[l][None], cw=gdn_conv_w[l],
        w_pa=w_branch_a[l].astype(bf16), w_pb=w_branch_b[l].astype(bf16), w_out=w_out[l].astype(bf16),
        ln1_g=ln1_g[l][None], ln1_b=ln1_b[l][None],
        w_up_a=w_up[l][:, :D_FF].astype(bf16), w_up_b=w_up[l][:, D_FF:].astype(bf16),
        cw_a=ffn_conv_w[l][:, :D_FF], cw_b=ffn_conv_w[l][:, D_FF:],
        w_down=w_down[l].astype(bf16), ln2_g=ln2_g[l][None], ln2_b=ln2_b[l][None])


def _block_layer(x, lw, st, blk, valid):
    c0, n0, m0, s0, gbuf, fbuf = st
    nseq_tot = c0.shape[0]
    nseq = CH // blk
    ng = nseq_tot // nseq
    front = blk - valid
    m0p = jnp.pad(m0.reshape(ng, nseq, NH), ((0, 0), (0, 8 - nseq), (0, 128 - NH)))
    e = jnp.pad(gbuf, ((0, 0), (front - (CONV_B - 1), valid), (0, 0))).reshape(nseq_tot * blk, 3 * WB)
    ef = jnp.pad(fbuf, ((0, 0), (front - (CONV_F - 1), valid), (0, 0))).reshape(nseq_tot * blk, 2 * D_FF)
    p = _inproj(x, lw["w_mix"])
    ha, hb, c, n, m, s, ext = _mixer_block(p, e, c0, n0, m0p, s0, lw["gb"], lw["al"], lw["anw"], lw["bnw"],
                                           lw["cw"], blk, valid)
    x1 = _outproj(x, ha, hb, lw["w_merge"], lw["w_pa"], lw["w_pb"], lw["w_out"], lw["ln1_g"], lw["ln1_b"])
    x2, ua, ub = _ffn2(x1, lw["w_up_a"], lw["w_up_b"], lw["cw_a"], lw["cw_b"], lw["w_down"],
                       lw["ln2_g"], lw["ln2_b"], e_a=ef[:, :D_FF], e_b=ef[:, D_FF:], blk=blk, valid=valid)
    m_new = m[:, :nseq, :NH].reshape(nseq_tot, NH)
    gconv = ext.reshape(nseq_tot, blk, 3 * WB)[:, blk - (CONV_B - 1):]
    fconv = jnp.concatenate([ua, ub], axis=1).reshape(nseq_tot, blk, 2 * D_FF)[:, blk - (CONV_F - 1):]
    return x2, (c, n, m_new, s, gconv, fconv)


def _long_layer(x, lw, st, nb, seq):
    c0, n0, m0, s0, gbuf, fbuf = st
    m0p = jnp.pad(m0.reshape(1, 1, NH), ((0, 0), (0, 7), (0, 128 - NH)))
    hist = jnp.pad(gbuf, ((0, 0), (HDR - (CONV_B - 1), 0), (0, 0)))
    hf = jnp.pad(fbuf, ((0, 0), (HDR - (CONV_F - 1), 0), (0, 0)))
    p = _inproj(x, lw["w_mix"])
    ha, hb, c, n, m, s, hist_out = _mixer_long(p, hist, c0, n0, m0p, s0, lw["gb"], lw["al"], lw["anw"],
                                               lw["bnw"], lw["cw"], nb, seq, 512)
    x1 = _outproj(x, ha, hb, lw["w_merge"], lw["w_pa"], lw["w_pb"], lw["w_out"], lw["ln1_g"], lw["ln1_b"])
    x2, sa, sb = _ffn2(x1, lw["w_up_a"], lw["w_up_b"], lw["cw_a"], lw["cw_b"], lw["w_down"],
                       lw["ln2_g"], lw["ln2_b"], hist_a=hf[:, :, :D_FF], hist_b=hf[:, :, D_FF:], nb=nb)
    gconv = hist_out[:, HDR - (CONV_B - 1):]
    last_tile = lambda a: a.reshape(nb, -1, HDR, D_FF)[:, -1]
    fconv = jnp.concatenate([last_tile(sa), last_tile(sb)], axis=2)[:, HDR - (CONV_F - 1):]
    return x2, (c, n, m[:, 0, :NH], s, gconv, fconv)


def kernel(x_prompt, x_sample, state_mlstm_C, state_mlstm_n, state_mlstm_m, state_gdn_S, state_gdn_conv, state_ffn_conv, meta_tokens, ln_emb_g, ln_emb_b, w_in, mlstm_gate_bias, mlstm_norm_w, gdn_conv_w, gdn_A_log, gdn_dt_bias, gdn_norm_w, w_branch_a, w_branch_b, w_out, ln1_g, ln1_b, w_up, ffn_conv_w, w_down, ln2_g, ln2_b):
    nb, seq, _ = x_prompt.shape
    ns, ls, _ = x_sample.shape
    sblk = 8
    assert seq % 512 == 0 and ls + CONV_B - 1 <= sblk and ns % (CH // sblk) == 0 and N_META + CONV_B - 1 <= CH
    lws = [_layer_weights(l, w_in, mlstm_gate_bias, mlstm_norm_w, gdn_conv_w, gdn_A_log, gdn_dt_bias,
                          gdn_norm_w, w_branch_a, w_branch_b, w_out, ln1_g, ln1_b, w_up, ffn_conv_w,
                          w_down, ln2_g, ln2_b) for l in range(DEPTH)]
    eg, eb = ln_emb_g[None], ln_emb_b[None]

    xm = _layer_norm(jnp.pad(meta_tokens, ((CH - N_META, 0), (0, 0))), eg, eb)
    xs = _layer_norm(jnp.pad(x_sample, ((0, 0), (sblk - ls, 0), (0, 0))).reshape(ns * sblk, D_MODEL), eg, eb)
    xp = _layer_norm(x_prompt.reshape(nb * seq, D_MODEL), eg, eb)

    zero_st = (jnp.zeros((1, NH, DH, DH), f32), jnp.zeros((1, NH, DH), f32), jnp.zeros((1, NH), f32),
               jnp.zeros((1, NH, DH, DH), f32), jnp.zeros((1, CONV_B - 1, 3 * WB), f32),
               jnp.zeros((1, CONV_F - 1, 2 * D_FF), f32))
    p_states, s_states = [], []
    for l in range(DEPTH):
        xm, st_m = _block_layer(xm, lws[l], zero_st, CH, N_META)
        xp, st_p = _long_layer(xp, lws[l], st_m, nb, seq)
        samp_st = (state_mlstm_C[l], state_mlstm_n[l], state_mlstm_m[l], state_gdn_S[l],
                   state_gdn_conv[l], state_ffn_conv[l])
        xs, st_s = _block_layer(xs, lws[l], samp_st, sblk, ls)
        p_states.append(st_p)
        s_states.append(st_s)

    stack = lambda states, i: jnp.stack([s[i] for s in states], axis=0)
    y_prompt = xp.reshape(nb, seq, D_MODEL)
    y_sample = xs.reshape(ns, sblk, D_MODEL)[:, sblk - ls:]
    return (y_prompt, y_sample,
            stack(p_states, 0), stack(p_states, 1), stack(p_states, 2), stack(p_states, 3),
            stack(p_states, 4), stack(p_states, 5),
            stack(s_states, 0), stack(s_states, 1), stack(s_states, 2), stack(s_states, 3),
            stack(s_states, 4), stack(s_states, 5))
```

```python
import functools

import jax
import jax.numpy as jnp
from jax import lax
from jax.experimental import pallas as pl
from jax.experimental.pallas import tpu as pltpu

f32 = jnp.float32
bf16 = jnp.bfloat16

D_MODEL = 1024
N_META = 16
NH = 4
DH = 128
WB = NH * DH
CONV_B = 4
D_FF = 2816
CONV_F = 3
DEPTH = 2
ALPHA = (2 * DEPTH) ** 0.25
LN_EPS = 1e-5
NORM_EPS = 1e-6
QSCALE = DH ** -0.5

A_Q = 0
A_I = 4 * WB
B_QKV = A_I + 2 * NH
B_Z = B_QKV + 3 * WB
B_BETA = B_Z + WB
G_MERGE = B_BETA + 2 * NH

C_QA, C_KA, C_VA, C_OA = 0, WB, 2 * WB, 3 * WB
C_G = 4 * WB
C_QKV = C_G + 128
C_Z = C_QKV + 3 * WB
N_MIX = C_Z + WB + 128

CH = 64
A_GROUP = 8
HDR = 8
NEG = -1e30
FT = 256
NF = D_FF // FT
FFN_TM_BLOCK = 256
FFN_TM = 512
VMEM_LIMIT = 56 * 1024 * 1024


def _cparams(sem):
    return pltpu.CompilerParams(dimension_semantics=sem, vmem_limit_bytes=VMEM_LIMIT)


def _dot(a, b):
    return jnp.dot(a, b, preferred_element_type=f32)


def _dot_nt(a, b):
    return lax.dot_general(a, b, (((1,), (1,)), ((), ())), preferred_element_type=f32)


def _dot_tn(a, b):
    return lax.dot_general(a, b, (((0,), (0,)), ((), ())), preferred_element_type=f32)


def _sel_dot(sel, x):
    hi = x.astype(bf16)
    r1 = x - hi.astype(f32)
    mid = r1.astype(bf16)
    lo = (r1 - mid.astype(f32)).astype(bf16)
    return (_dot(sel, hi) + _dot(sel, mid)) + _dot(sel, lo)


def _softplus(x):
    return jnp.maximum(x, 0.0) + jnp.log1p(jnp.exp(-jnp.abs(x)))


def _sigmoid(x):
    return 1.0 / (1.0 + jnp.exp(-x))


def _silu(x):
    return x * _sigmoid(x)


def _ln_rows(x, g, b):
    mu = jnp.mean(x, axis=-1, keepdims=True)
    xc = x - mu
    var = jnp.mean(xc * xc, axis=-1, keepdims=True)
    return xc * lax.rsqrt(var + LN_EPS) * g + b


def _ln_kernel(x_ref, g_ref, b_ref, o_ref):
    o_ref[...] = _ln_rows(x_ref[...], g_ref[...], b_ref[...])


def _layer_norm(x, g, b):
    m = x.shape[0]
    tm = min(m, 1024)
    return pl.pallas_call(
        _ln_kernel,
        grid=(m // tm,),
        in_specs=[pl.BlockSpec((tm, D_MODEL), lambda i: (i, 0)),
                  pl.BlockSpec((1, D_MODEL), lambda i: (0, 0)),
                  pl.BlockSpec((1, D_MODEL), lambda i: (0, 0))],
        out_specs=pl.BlockSpec((tm, D_MODEL), lambda i: (i, 0)),
        out_shape=jax.ShapeDtypeStruct((m, D_MODEL), f32),
        compiler_params=_cparams(("arbitrary",)),
        name="embed_ln",
    )(x, g, b)


def _inproj_kernel(x_ref, w_ref, o_ref):
    o_ref[...] = _dot(x_ref[...].astype(bf16), w_ref[...])


def _inproj(x, w_mix):
    m = x.shape[0]
    tm = min(m, 512)
    return pl.pallas_call(
        _inproj_kernel,
        grid=(m // tm,),
        in_specs=[pl.BlockSpec((tm, D_MODEL), lambda i: (i, 0)),
                  pl.BlockSpec((D_MODEL, N_MIX), lambda i: (0, 0))],
        out_specs=pl.BlockSpec((tm, N_MIX), lambda i: (i, 0)),
        out_shape=jax.ShapeDtypeStruct((m, N_MIX), f32),
        compiler_params=_cparams(("arbitrary",)),
        name="mixer_inproj",
    )(x, w_mix)


def _chunk_consts(blk, valid):
    lg = blk.bit_length() - 1
    r = lax.broadcasted_iota(jnp.int32, (CH, CH), 0)
    c = lax.broadcasted_iota(jnp.int32, (CH, CH), 1)
    same = (r >> lg) == (c >> lg)
    tri = jnp.logical_and(same, c <= r)
    stri = jnp.logical_and(same, c < r)
    sel = lambda cond: jnp.where(cond, 1.0, 0.0).astype(bf16)
    tri_m = sel(tri)
    blk_m = sel(same)
    last_m = sel(c == ((r >> lg) << lg) + (blk - 1))
    re = lax.broadcasted_iota(jnp.int32, (CH, 8), 0)
    ce = lax.broadcasted_iota(jnp.int32, (CH, 8), 1)
    expand_m = sel(ce == (re >> lg))
    rs = lax.broadcasted_iota(jnp.int32, (8, CH), 0)
    cs = lax.broadcasted_iota(jnp.int32, (8, CH), 1)
    rowsel_m = sel(cs == (rs << lg) + (blk - 1))
    rv = lax.broadcasted_iota(jnp.int32, (CH, 128), 0)
    valid_m = (rv & (blk - 1)) >= (blk - valid)
    return dict(tri=tri, stri=stri, tri_m=tri_m, blk_m=blk_m, last_m=last_m,
                expand_m=expand_m, rowsel_m=rowsel_m, valid_m=valid_m)


def _block_max(x, blk):
    if blk == CH:
        return jnp.broadcast_to(jnp.max(x, axis=0, keepdims=True), x.shape)
    x3 = x.reshape(CH // blk, blk, 128)
    return jnp.broadcast_to(jnp.max(x3, axis=1, keepdims=True), x3.shape).reshape(CH, 128)


def _mix_scratch(nc, nseq):
    return [pltpu.VMEM((nc, 6, CH, 128), f32),
            pltpu.VMEM((nc, NH, CH, DH), f32),
            pltpu.VMEM((nc, nseq * NH, DH, DH), f32),
            pltpu.VMEM((nc, nseq, 8, DH), f32),
            pltpu.VMEM((nc, nseq * NH, DH, DH), bf16),
            pltpu.VMEM((nc, nseq * NH, DH, DH), f32),
            pltpu.VMEM((nc, NH, CH, DH), bf16),
            pltpu.VMEM((nc, NH, CH, DH), f32)]


def _phase_a(blk, valid, k, chunks, par, sc):
    nseq = CH // blk
    gbias, alog = par[0], par[1]
    t_ref, numl_ref, kv_ref, nv_ref, m1_ref, m2_ref, qeff_ref, o2_ref = sc
    tri, stri = k["tri"], k["stri"]
    seqs = [slice(i * blk, (i + 1) * blk) for i in range(nseq)]
    nch = len(chunks)
    units = [(c, h) for c in range(nch) for h in range(NH)]
    lane = lax.broadcasted_iota(jnp.int32, (CH, 128), 1)
    cum_lane = jnp.logical_or(jnp.logical_and(lane >= 4, lane < 8), lane >= 12)

    gt = []
    for ci, pr, qkv in chunks:
        graw = pr(C_G, 128) + gbias
        g = jnp.where(lane < 4, graw,
                      jnp.where(lane < 8, -_softplus(-graw),
                                jnp.where(lane < 12, _sigmoid(graw),
                                          jnp.where(lane < 16, -jnp.exp(alog) * _softplus(graw), 0.0))))
        if valid < blk:
            g = jnp.where(k["valid_m"], g, jnp.where(lane < 4, NEG, 0.0))
        gt.append(g)
    cs = [_sel_dot(k["tri_m"], g) for g in gt]
    if nseq == 1:
        tot = [jnp.broadcast_to(x[CH - 1:CH, :], (CH, 128)) for x in cs]
    else:
        tot = [_sel_dot(k["blk_m"], g) for g in gt]
    mix_t = [jnp.transpose(jnp.where(cum_lane, cs[c], gt[c])) for c in range(nch)]
    b_all = [pltpu.roll(x, 124, 1) for x in cs]
    btot_all = [pltpu.roll(x, 124, 1) for x in tot]
    wlog = [btot_all[c] - b_all[c] + gt[c] for c in range(nch)]
    mloc_all = [_block_max(x, blk) for x in wlog]
    wsrc_all = [jnp.exp(wlog[c] - mloc_all[c]) for c in range(nch)]
    expg_all = [jnp.exp(x) for x in cs]
    expdiff_all = [jnp.exp(tot[c] - cs[c]) for c in range(nch)]

    qa, ka, va, qs, ks, vg = {}, {}, {}, {}, {}, {}
    for c, h in units:
        _, pr, qkv = chunks[c]
        qa[c, h] = (pr(C_QA + h * DH, DH) * QSCALE).astype(bf16)
        ka[c, h] = pr(C_KA + h * DH, DH)
        va[c, h] = pr(C_VA + h * DH, DH).astype(bf16)
        qs[c, h] = qkv(h * DH, DH)
        ks[c, h] = qkv(WB + h * DH, DH)
        vg[c, h] = qkv(2 * WB + h * DH, DH)
    qss = {u: jnp.sum(qs[u] * qs[u], axis=1, keepdims=True) for u in units}
    kss = {u: jnp.sum(ks[u] * ks[u], axis=1, keepdims=True) for u in units}
    qg = {u: qs[u] * lax.rsqrt(qss[u] + NORM_EPS) * QSCALE for u in units}
    kgf = {u: ks[u] * lax.rsqrt(kss[u] + NORM_EPS) for u in units}
    qgb = {u: qg[u].astype(bf16) for u in units}
    kg = {u: kgf[u].astype(bf16) for u in units}

    sq = {u: _dot_nt(qa[u], ka[u].astype(bf16)) for u in units}
    kk = {u: _dot_nt(kg[u], kg[u]) for u in units}
    qk = {u: _dot_nt(qgb[u], kg[u]) for u in units}

    dm = {(c, h): jnp.where(tri, cs[c][:, 4 + h:5 + h] - mix_t[c][4 + h:5 + h, :] + mix_t[c][h:h + 1, :], NEG)
          for c, h in units}
    dmax = {u: jnp.max(dm[u], axis=1, keepdims=True) for u in units}
    s = {u: sq[u] * jnp.exp(dm[u] - dmax[u]) for u in units}
    denl = {u: jnp.sum(s[u], axis=1, keepdims=True) for u in units}
    sloc = {u: s[u].astype(bf16) for u in units}
    kwl = {(c, h): ka[c, h] * wsrc_all[c][:, h:h + 1] for c, h in units}
    for c in range(nch):
        ci = chunks[c][0]
        dmax_all = jnp.zeros((CH, 128), f32)
        denl_all = jnp.zeros((CH, 128), f32)
        for h in range(NH):
            dmax_all = jnp.where(lane == h, dmax[c, h], dmax_all)
            denl_all = jnp.where(lane == h, denl[c, h], denl_all)
        t_ref[ci, 0] = gt[c]
        t_ref[ci, 1] = b_all[c]
        t_ref[ci, 2] = btot_all[c]
        t_ref[ci, 3] = dmax_all
        t_ref[ci, 4] = denl_all
        t_ref[ci, 5] = mloc_all[c]

    xs, pw, qkd, kd = {}, {}, {}, {}
    for c, h in units:
        beta = gt[c][:, 8 + h:9 + h]
        decay = jnp.exp(jnp.where(tri, cs[c][:, 12 + h:13 + h] - mix_t[c][12 + h:13 + h, :], NEG))
        pw[c, h] = -jnp.where(stri, beta * kk[c, h] * decay, 0.0)
        xs[c, h] = jnp.concatenate([beta * vg[c, h], (beta * expg_all[c][:, 12 + h:13 + h]) * kgf[c, h]], axis=1)
        qkd[c, h] = (qk[c, h] * decay).astype(bf16)
        kd[c, h] = (kgf[c, h] * expdiff_all[c][:, 12 + h:13 + h]).astype(bf16)

    for c, h in units:
        ci = chunks[c][0]
        numl_ref[ci, h] = _dot(sloc[c, h], va[c, h])
        kwb = kwl[c, h].astype(bf16)
        for i, sl in enumerate(seqs):
            kv_ref[ci, i * NH + h] = _dot_tn(kwb[sl], va[c, h][sl])
            nv_ref[ci, i, h:h + 1, :] = jnp.sum(kwl[c, h][sl], axis=0, keepdims=True)

    nsteps = blk.bit_length() - 1
    for it in range(nsteps):
        pwb = {u: pw[u].astype(bf16) for u in units}
        xs = {u: xs[u] + _dot(pwb[u], xs[u].astype(bf16)) for u in units}
        if it + 1 < nsteps:
            pw = {u: _dot(pwb[u], pwb[u]) for u in units}

    xb = {u: xs[u].astype(bf16) for u in units}
    qx = {u: _dot(qkd[u], xb[u]) for u in units}
    for c, h in units:
        ci = chunks[c][0]
        o2_ref[ci, h] = qx[c, h][:, :DH]
        qeff_ref[ci, h] = (expg_all[c][:, 12 + h:13 + h] * qg[c, h] - qx[c, h][:, DH:]).astype(bf16)
        for i, sl in enumerate(seqs):
            mm = _dot_tn(kd[c, h][sl], xb[c, h][sl])
            m2_ref[ci, i * NH + h] = mm[:, :DH]
            m1_ref[ci, i * NH + h] = mm[:, DH:].astype(bf16)


def _phase_b(blk, k, pr, par, st, sc, out, ci):
    nseq = CH // blk
    anw, bnw = par[2], par[3]
    c_ref, n_ref, m_ref, s_ref = st
    t_ref, numl_ref, kv_ref, nv_ref, m1_ref, m2_ref, qeff_ref, o2_ref = sc
    ha_ref, hb_ref, rows = out
    seqs = [slice(i * blk, (i + 1) * blk) for i in range(nseq)]
    heads = range(NH)
    gt, b_all, btot_all = t_ref[ci, 0], t_ref[ci, 1], t_ref[ci, 2]
    dmax_all, denl_all, mloc_all = t_ref[ci, 3], t_ref[ci, 4], t_ref[ci, 5]

    qa = [pr(C_QA + h * DH, DH) * QSCALE for h in heads]
    qc, ms, oq = [], [], []
    for h in heads:
        qb = qa[h].astype(bf16)
        qc.append([_dot(qb[sl], c_ref[i, h].astype(bf16)) for i, sl in enumerate(seqs)])
    for h in heads:
        qe = qeff_ref[ci, h]
        ms_h, oq_h = [], []
        for i, sl in enumerate(seqs):
            sb = s_ref[i, h].astype(bf16)
            ms_h.append(_dot(m1_ref[ci, i * NH + h], sb))
            oq_h.append(_dot(qe[sl], sb))
        ms.append(ms_h)
        oq.append(oq_h)

    if nseq == 1:
        mprev_all = jnp.broadcast_to(m_ref[0:1, :], (CH, 128))
    else:
        mprev_all = _sel_dot(k["expand_m"], m_ref[...])
    inter = b_all + mprev_all
    mt = jnp.maximum(inter, dmax_all)
    scale_all = jnp.exp(dmax_all - mt)
    winter_all = jnp.exp(inter - mt)
    emt_all = jnp.exp(-mt)
    carry = btot_all + mprev_all
    mnew = jnp.maximum(carry, mloc_all)
    sc2_all = jnp.exp(mloc_all - mnew)
    wold_all = jnp.exp(carry - mnew)
    egt_all = jnp.exp(btot_all)
    if nseq == 1:
        m_ref[0:1, :] = mnew[CH - 1:CH, :]
    else:
        m_ref[...] = _sel_dot(k["rowsel_m"], mnew)

    cat = lambda parts: parts[0] if nseq == 1 else jnp.concatenate(parts, axis=0)
    qn = [cat([jnp.sum(qa[h][sl] * n_ref[i, h:h + 1, :], axis=1, keepdims=True) for i, sl in enumerate(seqs)])
          for h in heads]
    hraw = []
    for h in heads:
        scale, winter = scale_all[:, h:h + 1], winter_all[:, h:h + 1]
        num = scale * numl_ref[ci, h] + winter * cat(qc[h])
        den = scale * denl_all[:, h:h + 1] + winter * qn[h]
        hraw.append(num / jnp.maximum(jnp.abs(den), emt_all[:, h:h + 1]))
    o = [cat(oq[h]) + o2_ref[ci, h] for h in heads]

    for h in heads:
        for i in range(nseq):
            last = (i + 1) * blk - 1
            wold = wold_all[last:last + 1, h:h + 1]
            sc2 = sc2_all[last:last + 1, h:h + 1]
            c_ref[i, h] = wold * c_ref[i, h] + sc2 * kv_ref[ci, i * NH + h]
            n_ref[i, h:h + 1, :] = wold * n_ref[i, h:h + 1, :] + sc2 * nv_ref[ci, i, h:h + 1, :]
            s_ref[i, h] = (egt_all[last:last + 1, 8 + h:9 + h] * s_ref[i, h] - ms[h][i]) + m2_ref[ci, i * NH + h]

    mu = [jnp.mean(x, axis=1, keepdims=True) for x in hraw]
    osq = [jnp.mean(x * x, axis=1, keepdims=True) for x in o]
    hc = [hraw[h] - mu[h] for h in heads]
    var = [jnp.mean(x * x, axis=1, keepdims=True) for x in hc]
    for h in heads:
        hn = hc[h] * lax.rsqrt(var[h] + NORM_EPS) * anw[:, h * DH:(h + 1) * DH]
        ha_ref[rows, h * DH:(h + 1) * DH] = _sigmoid(pr(C_OA + h * DH, DH)) * hn
        on = o[h] * lax.rsqrt(osq[h] + NORM_EPS) * bnw
        hb_ref[rows, h * DH:(h + 1) * DH] = on * _silu(pr(C_Z + h * DH, DH))


def _conv_silu(ext_ref, cw_ref, dst_ref, t):
    for cb in range(3 * WB // 128):
        cols = slice(cb * 128, (cb + 1) * 128)
        acc = ext_ref[HDR - 3:HDR - 3 + t, cols] * cw_ref[0:1, cols]
        for j in range(1, CONV_B):
            acc = acc + ext_ref[HDR - 3 + j:HDR - 3 + j + t, cols] * cw_ref[j:j + 1, cols]
        dst_ref[:, cols] = _silu(acc)


def _mixer_long_kernel(t, p_ref, hist_ref, c0_ref, n0_ref, m0_ref, s0_ref,
                       gb_ref, al_ref, anw_ref, bnw_ref, cw_ref,
                       ha_ref, hb_ref, c_ref, n_ref, m_ref, s_ref, hist_out_ref,
                       ext_ref, qkv_ref, *sc):
    g = pl.program_id(1)

    @pl.when(g == 0)
    def _():
        ext_ref[0:HDR, :] = hist_ref[0]
        c_ref[...] = c0_ref[...]
        n_ref[...] = n0_ref[...]
        m_ref[...] = m0_ref[...]
        s_ref[...] = s0_ref[...]

    ext_ref[HDR:HDR + t, :] = p_ref[:, C_QKV:C_QKV + 3 * WB]
    _conv_silu(ext_ref, cw_ref, qkv_ref, t)
    hist = ext_ref[t:t + HDR, :]
    ext_ref[0:HDR, :] = hist
    hist_out_ref[0] = hist

    k = _chunk_consts(CH, CH)
    par = (gb_ref[...], al_ref[...], anw_ref[...], bnw_ref[...])

    st = (c_ref, n_ref, _M0(m_ref), s_ref)

    def chunk_views(ci):
        rows = pl.ds(pl.multiple_of(ci * CH, CH), CH)
        return ci, (lambda c0, n: p_ref[rows, c0:c0 + n]), (lambda c0, n: qkv_ref[rows, c0:c0 + n])

    def body_a(j, carry):
        _phase_a(CH, CH, k, [chunk_views(A_GROUP * j + c) for c in range(A_GROUP)], par, sc)
        return carry

    def body_b(ci, carry):
        rows = pl.ds(pl.multiple_of(ci * CH, CH), CH)
        _phase_b(CH, k, lambda c0, n: p_ref[rows, c0:c0 + n], par, st, sc, (ha_ref, hb_ref, rows), ci)
        return carry

    lax.fori_loop(0, t // (CH * A_GROUP), body_a, 0)
    lax.fori_loop(0, t // CH, body_b, 0)


class _M0:
    def __init__(self, ref):
        self.ref = ref

    def __getitem__(self, idx):
        return self.ref[0] if idx is Ellipsis else self.ref[(0,) + idx]

    def __setitem__(self, idx, val):
        if idx is Ellipsis:
            self.ref[0] = val
        else:
            self.ref[(0,) + idx] = val


def _mixer_long(p, hist, c0, n0, m0, s0, gb, al, anw, bnw, cw, nb, seq, t):
    nt = seq // t
    row_spec = lambda w: pl.BlockSpec((t, w), lambda b, g: (b * nt + g, 0))
    const = lambda shape: pl.BlockSpec(shape, lambda b, g: (0,) * len(shape))
    perb = lambda shape: pl.BlockSpec((1,) + shape, lambda b, g: (b,) + (0,) * len(shape))
    return pl.pallas_call(
        functools.partial(_mixer_long_kernel, t),
        grid=(nb, nt),
        in_specs=[row_spec(N_MIX), const((1, HDR, 3 * WB)),
                  const((1, NH, DH, DH)), const((1, NH, DH)), const((1, 8, 128)), const((1, NH, DH, DH)),
                  const((1, 128)), const((1, 128)), const((1, WB)), const((1, DH)), const((CONV_B, 3 * WB))],
        out_specs=[row_spec(WB), row_spec(WB),
                   perb((NH, DH, DH)), perb((NH, DH)), perb((8, 128)), perb((NH, DH, DH)),
                   perb((HDR, 3 * WB))],
        out_shape=[jax.ShapeDtypeStruct((nb * seq, WB), f32), jax.ShapeDtypeStruct((nb * seq, WB), f32),
                   jax.ShapeDtypeStruct((nb, NH, DH, DH), f32), jax.ShapeDtypeStruct((nb, NH, DH), f32),
                   jax.ShapeDtypeStruct((nb, 8, 128), f32), jax.ShapeDtypeStruct((nb, NH, DH, DH), f32),
                   jax.ShapeDtypeStruct((nb, HDR, 3 * WB), f32)],
        scratch_shapes=[pltpu.VMEM((t + HDR, 3 * WB), f32), pltpu.VMEM((t, 3 * WB), f32)] + _mix_scratch(t // CH, 1),
        compiler_params=_cparams(("arbitrary", "arbitrary")),
        name="mixer_long",
    )(p, hist, c0, n0, m0, s0, gb, al, anw, bnw, cw)


def _mixer_block_kernel(blk, valid, n_alias, p_ref, e_ref, c0_ref, n0_ref, m0_ref, s0_ref,
                        gb_ref, al_ref, anw_ref, bnw_ref, cw_ref, *rest):
    ha_ref, hb_ref, c_ref, n_ref, m_ref, s_ref, ext_out_ref, ext_ref, qkv_ref = rest[n_alias:n_alias + 9]
    sc = rest[n_alias + 9:]
    c_ref[...] = c0_ref[...]
    n_ref[...] = n0_ref[...]
    m_ref[...] = m0_ref[...]
    s_ref[...] = s0_ref[...]
    k = _chunk_consts(blk, valid)
    rv = lax.broadcasted_iota(jnp.int32, (CH, 3 * WB), 0)
    ext = jnp.where((rv & (blk - 1)) >= (blk - valid), p_ref[:, C_QKV:C_QKV + 3 * WB], e_ref[...])
    ext_ref[0:HDR, :] = jnp.zeros((HDR, 3 * WB), f32)
    ext_ref[HDR:HDR + CH, :] = ext
    ext_out_ref[...] = ext
    _conv_silu(ext_ref, cw_ref, qkv_ref, CH)
    par = (gb_ref[...], al_ref[...], anw_ref[...], bnw_ref[...])
    pr = lambda c0, n: p_ref[:, c0:c0 + n]
    _phase_a(blk, valid, k, [(0, pr, lambda c0, n: qkv_ref[:, c0:c0 + n])], par, sc)
    _phase_b(blk, k, pr, par, (c_ref, n_ref, _M0(m_ref), s_ref), sc, (ha_ref, hb_ref, slice(0, CH)), 0)


def _mixer_block(p, e, c0, n0, m0, s0, gb, al, anw, bnw, cw, blk, valid, layer=None, prev=None):
    ng = p.shape[0] // CH
    nseq = CH // blk
    row_spec = lambda w: pl.BlockSpec((CH, w), lambda g: (g, 0))
    const = lambda shape: pl.BlockSpec(shape, lambda g: (0,) * len(shape))
    perg = lambda shape: pl.BlockSpec(shape, lambda g: (g,) + (0,) * (len(shape) - 1))
    big_shape = (ng * nseq, NH, DH, DH)
    if layer is None:
        big = perg((nseq, NH, DH, DH))
    else:
        big = pl.BlockSpec((None, nseq, NH, DH, DH), lambda g: (layer, g, 0, 0, 0))
        big_shape = (c0.shape[0],) + big_shape
    aliased = () if prev is None else tuple(prev)
    n_in = 11
    return pl.pallas_call(
        functools.partial(_mixer_block_kernel, blk, valid, len(aliased)),
        grid=(ng,),
        in_specs=[row_spec(N_MIX), row_spec(3 * WB),
                  big, perg((nseq, NH, DH)), perg((1, 8, 128)), big,
                  const((1, 128)), const((1, 128)), const((1, WB)), const((1, DH)), const((CONV_B, 3 * WB))]
                 + [pl.BlockSpec(memory_space=pl.ANY)] * len(aliased),
        out_specs=[row_spec(WB), row_spec(WB),
                   big, perg((nseq, NH, DH)), perg((1, 8, 128)), big,
                   row_spec(3 * WB)],
        out_shape=[jax.ShapeDtypeStruct((ng * CH, WB), f32), jax.ShapeDtypeStruct((ng * CH, WB), f32),
                   jax.ShapeDtypeStruct(big_shape, f32), jax.ShapeDtypeStruct((ng * nseq, NH, DH), f32),
                   jax.ShapeDtypeStruct((ng, 8, 128), f32), jax.ShapeDtypeStruct(big_shape, f32),
                   jax.ShapeDtypeStruct((ng * CH, 3 * WB), f32)],
        input_output_aliases={n_in: 2, n_in + 1: 5} if aliased else {},
        scratch_shapes=[pltpu.VMEM((CH + HDR, 3 * WB), f32), pltpu.VMEM((CH, 3 * WB), f32)] + _mix_scratch(1, nseq),
        compiler_params=_cparams(("arbitrary",)),
        name="mixer_block",
    )(p, e, c0, n0, m0, s0, gb, al, anw, bnw, cw, *aliased)


def _outproj_kernel(x_ref, ha_ref, hb_ref, wm_ref, wpa_ref, wpb_ref, wo_ref, g_ref, b_ref, o_ref):
    x = x_ref[...]
    mg = _sigmoid(_dot(x.astype(bf16), wm_ref[...]))
    ya = _dot(ha_ref[...].astype(bf16), wpa_ref[...])
    yb = _dot(hb_ref[...].astype(bf16), wpb_ref[...])
    y = mg[:, :D_MODEL] * ya + mg[:, D_MODEL:] * yb
    mix = _dot(y.astype(bf16), wo_ref[...])
    o_ref[...] = _ln_rows(ALPHA * x + mix, g_ref[...], b_ref[...])


def _outproj(x, ha, hb, w_merge, w_pa, w_pb, w_out, g, b):
    m = x.shape[0]
    tm = min(m, 512)
    row_spec = lambda w: pl.BlockSpec((tm, w), lambda i: (i, 0))
    const = lambda shape: pl.BlockSpec(shape, lambda i: (0, 0))
    return pl.pallas_call(
        _outproj_kernel,
        grid=(m // tm,),
        in_specs=[row_spec(D_MODEL), row_spec(WB), row_spec(WB),
                  const((D_MODEL, 2 * D_MODEL)), const((WB, D_MODEL)), const((WB, D_MODEL)),
                  const((D_MODEL, D_MODEL)), const((1, D_MODEL)), const((1, D_MODEL))],
        out_specs=row_spec(D_MODEL),
        out_shape=jax.ShapeDtypeStruct((m, D_MODEL), f32),
        compiler_params=_cparams(("arbitrary",)),
        name="merge_outproj_ln",
    )(x, ha, hb, w_merge, w_pa, w_pb, w_out, g, b)


def _conv3(u, h8, cw):
    p1 = pltpu.roll(u, 1, 0)
    p2 = pltpu.roll(u, 2, 0)
    if h8 is not None:
        row = lax.broadcasted_iota(jnp.int32, u.shape, 0)
        p1 = jnp.where(row == 0, h8[7:8], p1)
        p2 = jnp.where(row == 0, h8[6:7], jnp.where(row == 1, h8[7:8], p2))
    return (p2 * cw[0:1] + p1 * cw[1:2]) + u * cw[2:3]


def _ffn2_kernel(long_mode, tm, nmt, blk, valid, *refs):
    if long_mode:
        (x_ref, wa_ref, wb_ref, cwa_ref, cwb_ref, wd_ref, g_ref, b_ref, ha_ref, hb_ref,
         o_ref, sa_ref, sb_ref, hh_ref, ca_ref, cb_ref) = refs

        @pl.when(pl.program_id(0) % nmt == 0)
        def _():
            ca_ref[...] = ha_ref[0]
            cb_ref[...] = hb_ref[0]
    else:
        (x_ref, wa_ref, wb_ref, cwa_ref, cwb_ref, wd_ref, g_ref, b_ref, ea_ref, eb_ref,
         o_ref, sa_ref, sb_ref, hh_ref) = refs
        row = lax.broadcasted_iota(jnp.int32, (tm, FT), 0)
        vm = (row & (blk - 1)) >= (blk - valid)
    xb = x_ref[...].astype(bf16)
    for cb in range(NF):
        cols = slice(cb * FT, (cb + 1) * FT)
        ua = _dot(xb, wa_ref[:, cols])
        ub = _dot(xb, wb_ref[:, cols])
        if long_mode:
            h8a, h8b = ca_ref[:, cols], cb_ref[:, cols]
            ta, tb = ua[tm - HDR:tm], ub[tm - HDR:tm]
            ca_ref[:, cols] = ta
            cb_ref[:, cols] = tb
            sa_ref[0, :, cols] = ta
            sb_ref[0, :, cols] = tb
        else:
            h8a = h8b = None
            ua = jnp.where(vm, ua, ea_ref[:, cols])
            ub = jnp.where(vm, ub, eb_ref[:, cols])
            sa_ref[:, cols] = ua
            sb_ref[:, cols] = ub
        hh = _silu(_conv3(ua, h8a, cwa_ref[:, cols])) * _conv3(ub, h8b, cwb_ref[:, cols])
        hh_ref[:, cols] = hh.astype(bf16)
    out = _dot(hh_ref[...], wd_ref[...])
    o_ref[...] = _ln_rows(ALPHA * x_ref[...] + out, g_ref[...], b_ref[...])


def _ffn2(x, w_up_a, w_up_b, cw_a, cw_b, w_down, g, b, hist_a=None, hist_b=None,
          e_ab=None, nb=1, blk=CH, valid=CH):
    m = x.shape[0]
    long_mode = hist_a is not None
    seq = m // nb
    tm = min(seq, FFN_TM if long_mode else FFN_TM_BLOCK)
    nmt = seq // tm
    rows = lambda w: pl.BlockSpec((tm, w), lambda i: (i, 0))
    const = lambda shape: pl.BlockSpec(shape, lambda i: (0,) * len(shape))
    in_specs = [rows(D_MODEL), const((D_MODEL, D_FF)), const((D_MODEL, D_FF)), const((CONV_F, D_FF)),
                const((CONV_F, D_FF)), const((D_FF, D_MODEL)), const((1, D_MODEL)), const((1, D_MODEL))]
    scratch = [pltpu.VMEM((tm, D_FF), bf16)]
    if long_mode:
        in_specs += [const((1, HDR, D_FF))] * 2
        st_spec = pl.BlockSpec((1, HDR, D_FF), lambda i: (i, 0, 0))
        st_shape = jax.ShapeDtypeStruct((m // tm, HDR, D_FF), f32)
        scratch += [pltpu.VMEM((HDR, D_FF), f32), pltpu.VMEM((HDR, D_FF), f32)]
        extra = (hist_a, hist_b)
    else:
        in_specs += [pl.BlockSpec((tm, D_FF), lambda i: (i, 0)), pl.BlockSpec((tm, D_FF), lambda i: (i, 1))]
        st_spec = rows(D_FF)
        st_shape = jax.ShapeDtypeStruct((m, D_FF), f32)
        extra = (e_ab, e_ab)
    return pl.pallas_call(
        functools.partial(_ffn2_kernel, long_mode, tm, nmt, blk, valid),
        grid=(m // tm,),
        in_specs=in_specs,
        out_specs=[rows(D_MODEL), st_spec, st_spec],
        out_shape=[jax.ShapeDtypeStruct((m, D_MODEL), f32), st_shape, st_shape],
        scratch_shapes=scratch,
        compiler_params=_cparams(("arbitrary",)),
        name="conv_ffn_ln",
    )(x, w_up_a, w_up_b, cw_a, cw_b, w_down, g, b, *extra)


def _layer_weights(l, w_in, mlstm_gate_bias, mlstm_norm_w, gdn_conv_w, gdn_A_log, gdn_dt_bias, gdn_norm_w,
                   w_branch_a, w_branch_b, w_out, ln1_g, ln1_b, w_up, ffn_conv_w, w_down, ln2_g, ln2_b):
    w = w_in[l].astype(bf16)
    zc = lambda n: jnp.zeros((D_MODEL, n), bf16)
    w_mix = jnp.concatenate([w[:, A_Q:A_I], w[:, A_I:B_QKV], w[:, B_BETA:G_MERGE], zc(128 - 4 * NH),
                             w[:, B_QKV:B_Z], w[:, B_Z:B_BETA], zc(128)], axis=1)
    z4 = jnp.zeros((NH,), f32)
    gb = jnp.concatenate([mlstm_gate_bias[l], z4, gdn_dt_bias[l], jnp.zeros((128 - 4 * NH,), f32)])[None]
    al = jnp.concatenate([z4, z4, z4, gdn_A_log[l], jnp.zeros((128 - 4 * NH,), f32)])[None]
    return dict(
        w_mix=w_mix, w_merge=w[:, G_MERGE:].astype(bf16), gb=gb, al=al,
        anw=mlstm_norm_w[l][None], bnw=gdn_norm_w[l][None], cw=gdn_conv_w[l],
        w_pa=w_branch_a[l].astype(bf16), w_pb=w_branch_b[l].astype(bf16), w_out=w_out[l].astype(bf16),
        ln1_g=ln1_g[l][None], ln1_b=ln1_b[l][None],
        w_up_a=w_up[l][:, :D_FF].astype(bf16), w_up_b=w_up[l][:, D_FF:].astype(bf16),
        cw_a=ffn_conv_w[l][:, :D_FF], cw_b=ffn_conv_w[l][:, D_FF:],
        w_down=w_down[l].astype(bf16), ln2_g=ln2_g[l][None], ln2_b=ln2_b[l][None])


def _block_layer(x, lw, st, blk, valid, layer=None, prev=None):
    c0, n0, m0, s0, gbuf, fbuf = st
    nseq_tot = n0.shape[0]
    nseq = CH // blk
    ng = nseq_tot // nseq
    front = blk - valid
    m0p = jnp.pad(m0.reshape(ng, nseq, NH), ((0, 0), (0, 8 - nseq), (0, 128 - NH)))
    e = jnp.pad(gbuf, ((0, 0), (front - (CONV_B - 1), valid), (0, 0))).reshape(nseq_tot * blk, 3 * WB)
    ef = jnp.pad(fbuf, ((0, 0), (front - (CONV_F - 1), valid), (0, 0))).reshape(nseq_tot * blk, 2 * D_FF)
    p = _inproj(x, lw["w_mix"])
    ha, hb, c, n, m, s, ext = _mixer_block(p, e, c0, n0, m0p, s0, lw["gb"], lw["al"], lw["anw"], lw["bnw"],
                                           lw["cw"], blk, valid, layer, prev)
    x1 = _outproj(x, ha, hb, lw["w_merge"], lw["w_pa"], lw["w_pb"], lw["w_out"], lw["ln1_g"], lw["ln1_b"])
    x2, ua, ub = _ffn2(x1, lw["w_up_a"], lw["w_up_b"], lw["cw_a"], lw["cw_b"], lw["w_down"],
                       lw["ln2_g"], lw["ln2_b"], e_ab=ef, blk=blk, valid=valid)
    m_new = m[:, :nseq, :NH].reshape(nseq_tot, NH)
    tail = lambda a, w, k: a.reshape(nseq_tot, blk, w)[:, blk - k:]
    gconv = tail(ext, 3 * WB, CONV_B - 1)
    fconv = jnp.concatenate([tail(ua, D_FF, CONV_F - 1), tail(ub, D_FF, CONV_F - 1)], axis=2)
    return x2, (c, n, m_new, s, gconv, fconv)


def _long_layer(x, lw, st, nb, seq):
    c0, n0, m0, s0, gbuf, fbuf = st
    m0p = jnp.pad(m0.reshape(1, 1, NH), ((0, 0), (0, 7), (0, 128 - NH)))
    hist = jnp.pad(gbuf, ((0, 0), (HDR - (CONV_B - 1), 0), (0, 0)))
    hf = jnp.pad(fbuf, ((0, 0), (HDR - (CONV_F - 1), 0), (0, 0)))
    p = _inproj(x, lw["w_mix"])
    ha, hb, c, n, m, s, hist_out = _mixer_long(p, hist, c0, n0, m0p, s0, lw["gb"], lw["al"], lw["anw"],
                                               lw["bnw"], lw["cw"], nb, seq, 512)
    x1 = _outproj(x, ha, hb, lw["w_merge"], lw["w_pa"], lw["w_pb"], lw["w_out"], lw["ln1_g"], lw["ln1_b"])
    x2, sa, sb = _ffn2(x1, lw["w_up_a"], lw["w_up_b"], lw["cw_a"], lw["cw_b"], lw["w_down"],
                       lw["ln2_g"], lw["ln2_b"], hist_a=hf[:, :, :D_FF], hist_b=hf[:, :, D_FF:], nb=nb)
    gconv = hist_out[:, HDR - (CONV_B - 1):]
    last_tile = lambda a: a.reshape(nb, -1, HDR, D_FF)[:, -1]
    fconv = jnp.concatenate([last_tile(sa), last_tile(sb)], axis=2)[:, HDR - (CONV_F - 1):]
    return x2, (c, n, m[:, 0, :NH], s, gconv, fconv)


def kernel(x_prompt, x_sample, state_mlstm_C, state_mlstm_n, state_mlstm_m, state_gdn_S, state_gdn_conv, state_ffn_conv, meta_tokens, ln_emb_g, ln_emb_b, w_in, mlstm_gate_bias, mlstm_norm_w, gdn_conv_w, gdn_A_log, gdn_dt_bias, gdn_norm_w, w_branch_a, w_branch_b, w_out, ln1_g, ln1_b, w_up, ffn_conv_w, w_down, ln2_g, ln2_b):
    nb, seq, _ = x_prompt.shape
    ns, ls, _ = x_sample.shape
    sblk = 8
    assert seq % 512 == 0 and ls + CONV_B - 1 <= sblk and ns % (CH // sblk) == 0 and N_META + CONV_B - 1 <= CH
    lws = [_layer_weights(l, w_in, mlstm_gate_bias, mlstm_norm_w, gdn_conv_w, gdn_A_log, gdn_dt_bias,
                          gdn_norm_w, w_branch_a, w_branch_b, w_out, ln1_g, ln1_b, w_up, ffn_conv_w,
                          w_down, ln2_g, ln2_b) for l in range(DEPTH)]
    eg, eb = ln_emb_g[None], ln_emb_b[None]

    xm = _layer_norm(jnp.pad(meta_tokens, ((CH - N_META, 0), (0, 0))), eg, eb)
    xs = _layer_norm(jnp.pad(x_sample, ((0, 0), (sblk - ls, 0), (0, 0))).reshape(ns * sblk, D_MODEL), eg, eb)
    xp = _layer_norm(x_prompt.reshape(nb * seq, D_MODEL), eg, eb)

    zero_st = (jnp.zeros((1, NH, DH, DH), f32), jnp.zeros((1, NH, DH), f32), jnp.zeros((1, NH), f32),
               jnp.zeros((1, NH, DH, DH), f32), jnp.zeros((1, CONV_B - 1, 3 * WB), f32),
               jnp.zeros((1, CONV_F - 1, 2 * D_FF), f32))
    p_states, s_states = [], []
    big = None
    for l in range(DEPTH):
        xm, st_m = _block_layer(xm, lws[l], zero_st, CH, N_META)
        xp, st_p = _long_layer(xp, lws[l], st_m, nb, seq)
        samp_st = (state_mlstm_C, state_mlstm_n[l], state_mlstm_m[l], state_gdn_S,
                   state_gdn_conv[l], state_ffn_conv[l])
        xs, st_s = _block_layer(xs, lws[l], samp_st, sblk, ls, layer=l, prev=big)
        big = (st_s[0], st_s[3])
        p_states.append(st_p)
        s_states.append(st_s)

    stack = lambda states, i: jnp.stack([s[i] for s in states], axis=0)
    y_prompt = xp.reshape(nb, seq, D_MODEL)
    y_sample = xs.reshape(ns, sblk, D_MODEL)[:, sblk - ls:]
    return (y_prompt, y_sample,
            stack(p_states, 0), stack(p_states, 1), stack(p_states, 2), stack(p_states, 3),
            stack(p_states, 4), stack(p_states, 5),
            big[0], stack(s_states, 1), stack(s_states, 2), big[1],
            stack(s_states, 4), stack(s_states, 5))
```

```python
import functools

import jax
import jax.numpy as jnp
from jax import lax
from jax.experimental import pallas as pl
from jax.experimental.pallas import tpu as pltpu

f32 = jnp.float32
bf16 = jnp.bfloat16

D_MODEL = 1024
N_META = 16
NH = 4
DH = 128
WB = NH * DH
CONV_B = 4
D_FF = 2816
CONV_F = 3
DEPTH = 2
ALPHA = (2 * DEPTH) ** 0.25
LN_EPS = 1e-5
NORM_EPS = 1e-6
QSCALE = DH ** -0.5

A_Q = 0
A_I = 4 * WB
B_QKV = A_I + 2 * NH
B_Z = B_QKV + 3 * WB
B_BETA = B_Z + WB
G_MERGE = B_BETA + 2 * NH

C_QA, C_KA, C_VA, C_OA = 0, WB, 2 * WB, 3 * WB
C_Z = 4 * WB
C_G = 5 * WB
C_QKV = C_G + 128
N_MIX = C_QKV + 3 * WB

CH = 64
A_GROUP = 8
HDR = 8
NEG = -1e30
FT = 256
NF = D_FF // FT
FFN_TM_BLOCK = 256
FFN_TM = 512
MIX_T = 512
PROJ_TM = 512
VMEM_LIMIT = 56 * 1024 * 1024


def _cparams(sem):
    return pltpu.CompilerParams(dimension_semantics=sem, vmem_limit_bytes=VMEM_LIMIT)


def _dot(a, b):
    return jnp.dot(a, b, preferred_element_type=f32)


def _dot_nt(a, b):
    return lax.dot_general(a, b, (((1,), (1,)), ((), ())), preferred_element_type=f32)


def _dot_tn(a, b):
    return lax.dot_general(a, b, (((0,), (0,)), ((), ())), preferred_element_type=f32)


def _sel_dot(sel, x):
    hi = x.astype(bf16)
    r1 = x - hi.astype(f32)
    mid = r1.astype(bf16)
    lo = (r1 - mid.astype(f32)).astype(bf16)
    return (_dot(sel, hi) + _dot(sel, mid)) + _dot(sel, lo)


def _softplus(x):
    return jnp.maximum(x, 0.0) + jnp.log1p(jnp.exp(-jnp.abs(x)))


def _sigmoid(x):
    return 1.0 / (1.0 + jnp.exp(-x))


def _silu(x):
    return x * _sigmoid(x)


def _ln_rows(x, g, b):
    mu = jnp.mean(x, axis=-1, keepdims=True)
    xc = x - mu
    var = jnp.mean(xc * xc, axis=-1, keepdims=True)
    return xc * lax.rsqrt(var + LN_EPS) * g + b


def _ln_kernel(x_ref, g_ref, b_ref, o_ref):
    o_ref[...] = _ln_rows(x_ref[...], g_ref[...], b_ref[...])


def _layer_norm(x, g, b):
    m = x.shape[0]
    tm = min(m, 1024)
    return pl.pallas_call(
        _ln_kernel,
        grid=(m // tm,),
        in_specs=[pl.BlockSpec((tm, D_MODEL), lambda i: (i, 0)),
                  pl.BlockSpec((1, D_MODEL), lambda i: (0, 0)),
                  pl.BlockSpec((1, D_MODEL), lambda i: (0, 0))],
        out_specs=pl.BlockSpec((tm, D_MODEL), lambda i: (i, 0)),
        out_shape=jax.ShapeDtypeStruct((m, D_MODEL), f32),
        compiler_params=_cparams(("arbitrary",)),
        name="embed_ln",
    )(x, g, b)


def _conv_rows(u, h8, cw):
    width = cw.shape[0]
    row = lax.broadcasted_iota(jnp.int32, (HDR, u.shape[1]), 0) if h8 is not None else None
    acc = None
    for j in range(width):
        sh = width - 1 - j
        term = u if sh == 0 else pltpu.roll(u, sh, 0)
        if h8 is not None and sh > 0:
            head = term[0:HDR]
            for r in range(sh):
                head = jnp.where(row == r, h8[HDR - sh + r:HDR - sh + r + 1], head)
            term = jnp.concatenate([head, term[HDR:]], axis=0)
        term = term * cw[j:j + 1]
        acc = term if acc is None else acc + term
    return acc


def _inproj_kernel(long_mode, tm, nmt, blk, valid, *refs):
    if long_mode:
        x_ref, w_ref, cw_ref, hist_ref, o_ref, tail_ref, carry_ref = refs

        @pl.when(pl.program_id(0) % nmt == 0)
        def _():
            carry_ref[...] = hist_ref[0]
    else:
        x_ref, w_ref, cw_ref, e_ref, o_ref, tail_ref = refs
        row = lax.broadcasted_iota(jnp.int32, (tm, 2 * DH), 0)
        vm = (row & (blk - 1)) >= (blk - valid)
    xb = x_ref[...].astype(bf16)
    o_ref[:, C_QA:C_QA + WB] = _dot(xb, w_ref[:, C_QA:C_QA + WB]) * QSCALE
    o_ref[:, C_KA:C_QKV] = _dot(xb, w_ref[:, C_KA:C_QKV])
    for cb in range(3 * NH // 2):
        cols = slice(cb * 2 * DH, (cb + 1) * 2 * DH)
        u = _dot(xb, w_ref[:, C_QKV + cb * 2 * DH:C_QKV + (cb + 1) * 2 * DH])
        if long_mode:
            h8 = carry_ref[:, cols]
            t8 = u[tm - HDR:tm]
            carry_ref[:, cols] = t8
            tail_ref[0, :, cols] = t8
        else:
            h8 = None
            u = jnp.where(vm, u, e_ref[:, cols])
            tail_ref[:, cols] = u
        c = _silu(_conv_rows(u, h8, cw_ref[:, cols]))
        for half in range(2):
            ch = c[:, half * DH:(half + 1) * DH]
            if cb < NH:
                ch = ch * lax.rsqrt(jnp.sum(ch * ch, axis=1, keepdims=True) + NORM_EPS)
            if cb < NH // 2:
                ch = ch * QSCALE
            c0 = C_QKV + (2 * cb + half) * DH
            o_ref[:, c0:c0 + DH] = ch


def _inproj(x, w_mix, cw, hist=None, e=None, nb=1, blk=CH, valid=CH):
    m = x.shape[0]
    long_mode = hist is not None
    seq = m // nb
    tm = min(seq, PROJ_TM)
    nmt = seq // tm
    rows = lambda w: pl.BlockSpec((tm, w), lambda i: (i, 0))
    const = lambda shape: pl.BlockSpec(shape, lambda i: (0,) * len(shape))
    in_specs = [rows(D_MODEL), const((D_MODEL, N_MIX)), const((CONV_B, 3 * WB))]
    if long_mode:
        in_specs += [const((1, HDR, 3 * WB))]
        tail_spec = pl.BlockSpec((1, HDR, 3 * WB), lambda i: (i, 0, 0))
        tail_shape = jax.ShapeDtypeStruct((m // tm, HDR, 3 * WB), f32)
        scratch = [pltpu.VMEM((HDR, 3 * WB), f32)]
        extra = hist
    else:
        in_specs += [rows(3 * WB)]
        tail_spec = rows(3 * WB)
        tail_shape = jax.ShapeDtypeStruct((m, 3 * WB), f32)
        scratch = []
        extra = e
    return pl.pallas_call(
        functools.partial(_inproj_kernel, long_mode, tm, nmt, blk, valid),
        grid=(m // tm,),
        in_specs=in_specs,
        out_specs=[rows(N_MIX), tail_spec],
        out_shape=[jax.ShapeDtypeStruct((m, N_MIX), f32), tail_shape],
        scratch_shapes=scratch,
        compiler_params=_cparams(("arbitrary",)),
        name="mixer_inproj",
    )(x, w_mix, cw, extra)


def _chunk_consts(blk, valid):
    lg = blk.bit_length() - 1
    r = lax.broadcasted_iota(jnp.int32, (CH, CH), 0)
    c = lax.broadcasted_iota(jnp.int32, (CH, CH), 1)
    same = (r >> lg) == (c >> lg)
    tri = jnp.logical_and(same, c <= r)
    stri = jnp.logical_and(same, c < r)
    sel = lambda cond: jnp.where(cond, 1.0, 0.0).astype(bf16)
    tri_m = sel(tri)
    blk_m = sel(same)
    re = lax.broadcasted_iota(jnp.int32, (CH, 8), 0)
    ce = lax.broadcasted_iota(jnp.int32, (CH, 8), 1)
    expand_m = sel(ce == (re >> lg))
    rs = lax.broadcasted_iota(jnp.int32, (8, CH), 0)
    cs = lax.broadcasted_iota(jnp.int32, (8, CH), 1)
    rowsel_m = sel(cs == (rs << lg) + (blk - 1))
    rv = lax.broadcasted_iota(jnp.int32, (CH, 128), 0)
    valid_m = (rv & (blk - 1)) >= (blk - valid)
    return dict(tri=tri, stri=stri, tri_m=tri_m, blk_m=blk_m,
                expand_m=expand_m, rowsel_m=rowsel_m, valid_m=valid_m)


def _block_max(x, blk):
    if blk == CH:
        return jnp.broadcast_to(jnp.max(x, axis=0, keepdims=True), x.shape)
    x3 = x.reshape(CH // blk, blk, 128)
    return jnp.broadcast_to(jnp.max(x3, axis=1, keepdims=True), x3.shape).reshape(CH, 128)


def _mix_scratch(nc, nseq):
    return [pltpu.VMEM((nc, 6, CH, 128), f32),
            pltpu.VMEM((nc, NH, CH, DH), f32),
            pltpu.VMEM((nc, nseq * NH, DH, DH), f32),
            pltpu.VMEM((nc, nseq, 8, DH), f32),
            pltpu.VMEM((nc, nseq * NH, DH, DH), bf16),
            pltpu.VMEM((nc, nseq * NH, DH, DH), f32),
            pltpu.VMEM((nc, NH, CH, DH), bf16),
            pltpu.VMEM((nc, NH, CH, DH), f32)]


def _phase_a(blk, valid, k, chunks, par, sc):
    nseq = CH // blk
    gbias, alog = par[0], par[1]
    t_ref, numl_ref, kv_ref, nv_ref, m1_ref, m2_ref, qeff_ref, o2_ref = sc
    tri, stri = k["tri"], k["stri"]
    seqs = [slice(i * blk, (i + 1) * blk) for i in range(nseq)]
    nch = len(chunks)
    units = [(c, h) for c in range(nch) for h in range(NH)]
    lane = lax.broadcasted_iota(jnp.int32, (CH, 128), 1)
    cum_lane = jnp.logical_or(jnp.logical_and(lane >= 4, lane < 8), lane >= 12)

    gt = []
    for ci, pr in chunks:
        graw = pr(C_G, 128) + gbias
        g = jnp.where(lane < 4, graw,
                      jnp.where(lane < 8, -_softplus(-graw),
                                jnp.where(lane < 12, _sigmoid(graw),
                                          jnp.where(lane < 16, -jnp.exp(alog) * _softplus(graw), 0.0))))
        if valid < blk:
            g = jnp.where(k["valid_m"], g, jnp.where(lane < 4, NEG, 0.0))
        gt.append(g)
    cs = [_sel_dot(k["tri_m"], g) for g in gt]
    if nseq == 1:
        tot = [jnp.broadcast_to(x[CH - 1:CH, :], (CH, 128)) for x in cs]
    else:
        tot = [_sel_dot(k["blk_m"], g) for g in gt]
    mix_t = [jnp.transpose(jnp.where(cum_lane, cs[c], gt[c])) for c in range(nch)]
    b_all = [pltpu.roll(x, 124, 1) for x in cs]
    btot_all = [pltpu.roll(x, 124, 1) for x in tot]
    wlog = [btot_all[c] - b_all[c] + gt[c] for c in range(nch)]
    mloc_all = [_block_max(x, blk) for x in wlog]
    wsrc_all = [jnp.exp(wlog[c] - mloc_all[c]) for c in range(nch)]
    expg_all = [jnp.exp(x) for x in cs]
    expdiff_all = [jnp.exp(tot[c] - cs[c]) for c in range(nch)]

    qa, ka, va, qg, kgf, vg = {}, {}, {}, {}, {}, {}
    for c, h in units:
        _, pr = chunks[c]
        qa[c, h] = pr(C_QA + h * DH, DH).astype(bf16)
        ka[c, h] = pr(C_KA + h * DH, DH)
        va[c, h] = pr(C_VA + h * DH, DH).astype(bf16)
        qg[c, h] = pr(C_QKV + h * DH, DH)
        kgf[c, h] = pr(C_QKV + WB + h * DH, DH)
        vg[c, h] = pr(C_QKV + 2 * WB + h * DH, DH)
    qgb = {u: qg[u].astype(bf16) for u in units}
    kg = {u: kgf[u].astype(bf16) for u in units}

    sq = {u: _dot_nt(qa[u], ka[u].astype(bf16)) for u in units}
    kk = {u: _dot_nt(kg[u], kg[u]) for u in units}
    qk = {u: _dot_nt(qgb[u], kg[u]) for u in units}

    dm = {(c, h): jnp.where(tri, cs[c][:, 4 + h:5 + h] - mix_t[c][4 + h:5 + h, :] + mix_t[c][h:h + 1, :], NEG)
          for c, h in units}
    dmax = {u: jnp.max(dm[u], axis=1, keepdims=True) for u in units}
    s = {u: sq[u] * jnp.exp(dm[u] - dmax[u]) for u in units}
    denl = {u: jnp.sum(s[u], axis=1, keepdims=True) for u in units}
    sloc = {u: s[u].astype(bf16) for u in units}
    kwl = {(c, h): ka[c, h] * wsrc_all[c][:, h:h + 1] for c, h in units}
    for c in range(nch):
        ci = chunks[c][0]
        dmax_all = jnp.zeros((CH, 128), f32)
        denl_all = jnp.zeros((CH, 128), f32)
        for h in range(NH):
            dmax_all = jnp.where(lane == h, dmax[c, h], dmax_all)
            denl_all = jnp.where(lane == h, denl[c, h], denl_all)
        t_ref[ci, 0] = gt[c]
        t_ref[ci, 1] = b_all[c]
        t_ref[ci, 2] = btot_all[c]
        t_ref[ci, 3] = dmax_all
        t_ref[ci, 4] = denl_all
        t_ref[ci, 5] = mloc_all[c]

    xs, pw, qkd, kd = {}, {}, {}, {}
    for c, h in units:
        beta = gt[c][:, 8 + h:9 + h]
        decay = jnp.exp(jnp.where(tri, cs[c][:, 12 + h:13 + h] - mix_t[c][12 + h:13 + h, :], NEG))
        pw[c, h] = -jnp.where(stri, beta * kk[c, h] * decay, 0.0)
        xs[c, h] = jnp.concatenate([beta * vg[c, h], (beta * expg_all[c][:, 12 + h:13 + h]) * kgf[c, h]], axis=1)
        qkd[c, h] = (qk[c, h] * decay).astype(bf16)
        kd[c, h] = (kgf[c, h] * expdiff_all[c][:, 12 + h:13 + h]).astype(bf16)

    for c, h in units:
        ci = chunks[c][0]
        numl_ref[ci, h] = _dot(sloc[c, h], va[c, h])
        kwb = kwl[c, h].astype(bf16)
        for i, sl in enumerate(seqs):
            kv_ref[ci, i * NH + h] = _dot_tn(kwb[sl], va[c, h][sl])
            nv_ref[ci, i, h:h + 1, :] = jnp.sum(kwl[c, h][sl], axis=0, keepdims=True)

    nsteps = blk.bit_length() - 1
    for it in range(nsteps):
        pwb = {u: pw[u].astype(bf16) for u in units}
        xs = {u: xs[u] + _dot(pwb[u], xs[u].astype(bf16)) for u in units}
        if it + 1 < nsteps:
            pw = {u: _dot(pwb[u], pwb[u]) for u in units}

    xb = {u: xs[u].astype(bf16) for u in units}
    qx = {u: _dot(qkd[u], xb[u]) for u in units}
    for c, h in units:
        ci = chunks[c][0]
        o2_ref[ci, h] = qx[c, h][:, :DH]
        qeff_ref[ci, h] = (expg_all[c][:, 12 + h:13 + h] * qg[c, h] - qx[c, h][:, DH:]).astype(bf16)
        for i, sl in enumerate(seqs):
            mm = _dot_tn(kd[c, h][sl], xb[c, h][sl])
            m2_ref[ci, i * NH + h] = mm[:, :DH]
            m1_ref[ci, i * NH + h] = mm[:, DH:].astype(bf16)


def _phase_b(blk, k, pr, par, st, sc, out, ci):
    nseq = CH // blk
    c_ref, n_ref, m_ref, s_ref = st
    t_ref, numl_ref, kv_ref, nv_ref, m1_ref, m2_ref, qeff_ref, o2_ref = sc
    ha_ref, hb_ref, rows = out
    seqs = [slice(i * blk, (i + 1) * blk) for i in range(nseq)]
    heads = range(NH)
    gt, b_all, btot_all = t_ref[ci, 0], t_ref[ci, 1], t_ref[ci, 2]
    dmax_all, denl_all, mloc_all = t_ref[ci, 3], t_ref[ci, 4], t_ref[ci, 5]

    qa = [pr(C_QA + h * DH, DH) for h in heads]
    qc, ms, oq = [], [], []
    for h in heads:
        qb = qa[h].astype(bf16)
        qc.append([_dot(qb[sl], c_ref[i, h].astype(bf16)) for i, sl in enumerate(seqs)])
    for h in heads:
        qe = qeff_ref[ci, h]
        ms_h, oq_h = [], []
        for i, sl in enumerate(seqs):
            sb = s_ref[i, h].astype(bf16)
            ms_h.append(_dot(m1_ref[ci, i * NH + h], sb))
            oq_h.append(_dot(qe[sl], sb))
        ms.append(ms_h)
        oq.append(oq_h)

    if nseq == 1:
        mprev_all = jnp.broadcast_to(m_ref[0:1, :], (CH, 128))
    else:
        mprev_all = _sel_dot(k["expand_m"], m_ref[...])
    inter = b_all + mprev_all
    mt = jnp.maximum(inter, dmax_all)
    scale_all = jnp.exp(dmax_all - mt)
    winter_all = jnp.exp(inter - mt)
    emt_all = jnp.exp(-mt)
    carry = btot_all + mprev_all
    mnew = jnp.maximum(carry, mloc_all)
    sc2_all = jnp.exp(mloc_all - mnew)
    wold_all = jnp.exp(carry - mnew)
    egt_all = jnp.exp(btot_all)
    if nseq == 1:
        m_ref[0:1, :] = mnew[CH - 1:CH, :]
    else:
        m_ref[...] = _sel_dot(k["rowsel_m"], mnew)

    cat = lambda parts: parts[0] if nseq == 1 else jnp.concatenate(parts, axis=0)
    qn = [cat([jnp.sum(qa[h][sl] * n_ref[i, h:h + 1, :], axis=1, keepdims=True) for i, sl in enumerate(seqs)])
          for h in heads]
    hraw = []
    for h in heads:
        scale, winter = scale_all[:, h:h + 1], winter_all[:, h:h + 1]
        num = scale * numl_ref[ci, h] + winter * cat(qc[h])
        den = scale * denl_all[:, h:h + 1] + winter * qn[h]
        hraw.append(num / jnp.maximum(jnp.abs(den), emt_all[:, h:h + 1]))
    o = [cat(oq[h]) + o2_ref[ci, h] for h in heads]

    for h in heads:
        for i in range(nseq):
            last = (i + 1) * blk - 1
            wold = wold_all[last:last + 1, h:h + 1]
            sc2 = sc2_all[last:last + 1, h:h + 1]
            c_ref[i, h] = wold * c_ref[i, h] + sc2 * kv_ref[ci, i * NH + h]
            n_ref[i, h:h + 1, :] = wold * n_ref[i, h:h + 1, :] + sc2 * nv_ref[ci, i, h:h + 1, :]
            s_ref[i, h] = (egt_all[last:last + 1, 8 + h:9 + h] * s_ref[i, h] - ms[h][i]) + m2_ref[ci, i * NH + h]

    for h in heads:
        ha_ref[rows, h * DH:(h + 1) * DH] = hraw[h]
        hb_ref[rows, h * DH:(h + 1) * DH] = o[h]


def _mixer_long_kernel(t, p_ref, c0_ref, n0_ref, m0_ref, s0_ref, gb_ref, al_ref,
                       ha_ref, hb_ref, c_ref, n_ref, m_ref, s_ref, *sc):
    g = pl.program_id(1)

    @pl.when(g == 0)
    def _():
        c_ref[...] = c0_ref[...]
        n_ref[...] = n0_ref[...]
        m_ref[...] = m0_ref[...]
        s_ref[...] = s0_ref[...]

    k = _chunk_consts(CH, CH)
    par = (gb_ref[...], al_ref[...])
    st = (c_ref, n_ref, _M0(m_ref), s_ref)

    def chunk_views(ci):
        rows = pl.ds(pl.multiple_of(ci * CH, CH), CH)
        return ci, (lambda c0, n: p_ref[rows, c0:c0 + n])

    def body_a(j, carry):
        _phase_a(CH, CH, k, [chunk_views(A_GROUP * j + c) for c in range(A_GROUP)], par, sc)
        return carry

    def body_b(ci, carry):
        rows = pl.ds(pl.multiple_of(ci * CH, CH), CH)
        _phase_b(CH, k, lambda c0, n: p_ref[rows, c0:c0 + n], par, st, sc, (ha_ref, hb_ref, rows), ci)
        return carry

    lax.fori_loop(0, t // (CH * A_GROUP), body_a, 0)
    lax.fori_loop(0, t // CH, body_b, 0)


class _M0:
    def __init__(self, ref):
        self.ref = ref

    def __getitem__(self, idx):
        return self.ref[0] if idx is Ellipsis else self.ref[(0,) + idx]

    def __setitem__(self, idx, val):
        if idx is Ellipsis:
            self.ref[0] = val
        else:
            self.ref[(0,) + idx] = val


def _mixer_long(p, c0, n0, m0, s0, gb, al, nb, seq, t):
    nt = seq // t
    row_spec = lambda w: pl.BlockSpec((t, w), lambda b, g: (b * nt + g, 0))
    const = lambda shape: pl.BlockSpec(shape, lambda b, g: (0,) * len(shape))
    perb = lambda shape: pl.BlockSpec((1,) + shape, lambda b, g: (b,) + (0,) * len(shape))
    return pl.pallas_call(
        functools.partial(_mixer_long_kernel, t),
        grid=(nb, nt),
        in_specs=[row_spec(N_MIX),
                  const((1, NH, DH, DH)), const((1, NH, DH)), const((1, 8, 128)), const((1, NH, DH, DH)),
                  const((1, 128)), const((1, 128))],
        out_specs=[row_spec(WB), row_spec(WB),
                   perb((NH, DH, DH)), perb((NH, DH)), perb((8, 128)), perb((NH, DH, DH))],
        out_shape=[jax.ShapeDtypeStruct((nb * seq, WB), f32), jax.ShapeDtypeStruct((nb * seq, WB), f32),
                   jax.ShapeDtypeStruct((nb, NH, DH, DH), f32), jax.ShapeDtypeStruct((nb, NH, DH), f32),
                   jax.ShapeDtypeStruct((nb, 8, 128), f32), jax.ShapeDtypeStruct((nb, NH, DH, DH), f32)],
        scratch_shapes=_mix_scratch(t // CH, 1),
        compiler_params=_cparams(("arbitrary", "arbitrary")),
        name="mixer_long",
    )(p, c0, n0, m0, s0, gb, al)


def _mixer_block_kernel(blk, valid, n_alias, p_ref, c0_ref, n0_ref, m0_ref, s0_ref, gb_ref, al_ref, *rest):
    ha_ref, hb_ref, c_ref, n_ref, m_ref, s_ref = rest[n_alias:n_alias + 6]
    sc = rest[n_alias + 6:]
    c_ref[...] = c0_ref[...]
    n_ref[...] = n0_ref[...]
    m_ref[...] = m0_ref[...]
    s_ref[...] = s0_ref[...]
    k = _chunk_consts(blk, valid)
    par = (gb_ref[...], al_ref[...])
    pr = lambda c0, n: p_ref[:, c0:c0 + n]
    _phase_a(blk, valid, k, [(0, pr)], par, sc)
    _phase_b(blk, k, pr, par, (c_ref, n_ref, _M0(m_ref), s_ref), sc, (ha_ref, hb_ref, slice(0, CH)), 0)


def _mixer_block(p, c0, n0, m0, s0, gb, al, blk, valid, layer=None, prev=None):
    ng = p.shape[0] // CH
    nseq = CH // blk
    row_spec = lambda w: pl.BlockSpec((CH, w), lambda g: (g, 0))
    const = lambda shape: pl.BlockSpec(shape, lambda g: (0,) * len(shape))
    perg = lambda shape: pl.BlockSpec(shape, lambda g: (g,) + (0,) * (len(shape) - 1))
    big_shape = (ng * nseq, NH, DH, DH)
    if layer is None:
        big = perg((nseq, NH, DH, DH))
    else:
        big = pl.BlockSpec((None, nseq, NH, DH, DH), lambda g: (layer, g, 0, 0, 0))
        big_shape = (c0.shape[0],) + big_shape
    aliased = () if prev is None else tuple(prev)
    n_in = 7
    return pl.pallas_call(
        functools.partial(_mixer_block_kernel, blk, valid, len(aliased)),
        grid=(ng,),
        in_specs=[row_spec(N_MIX), big, perg((nseq, NH, DH)), perg((1, 8, 128)), big,
                  const((1, 128)), const((1, 128))]
                 + [pl.BlockSpec(memory_space=pl.ANY)] * len(aliased),
        out_specs=[row_spec(WB), row_spec(WB), big, perg((nseq, NH, DH)), perg((1, 8, 128)), big],
        out_shape=[jax.ShapeDtypeStruct((ng * CH, WB), f32), jax.ShapeDtypeStruct((ng * CH, WB), f32),
                   jax.ShapeDtypeStruct(big_shape, f32), jax.ShapeDtypeStruct((ng * nseq, NH, DH), f32),
                   jax.ShapeDtypeStruct((ng, 8, 128), f32), jax.ShapeDtypeStruct(big_shape, f32)],
        input_output_aliases={n_in: 2, n_in + 1: 5} if aliased else {},
        scratch_shapes=_mix_scratch(1, nseq),
        compiler_params=_cparams(("arbitrary",)),
        name="mixer_block",
    )(p, c0, n0, m0, s0, gb, al, *aliased)


def _outproj_kernel(x_ref, ha_ref, hb_ref, po_ref, pz_ref, anw_ref, bnw_ref,
                    wm_ref, wpa_ref, wpb_ref, wo_ref, g_ref, b_ref, o_ref):
    x = x_ref[...]
    mg = _sigmoid(_dot(x.astype(bf16), wm_ref[...]))
    cols = [slice(h * DH, (h + 1) * DH) for h in range(NH)]
    hr = [ha_ref[:, c] for c in cols]
    ob = [hb_ref[:, c] for c in cols]
    mu = [jnp.mean(v, axis=1, keepdims=True) for v in hr]
    osq = [jnp.mean(v * v, axis=1, keepdims=True) for v in ob]
    hc = [hr[h] - mu[h] for h in range(NH)]
    var = [jnp.mean(v * v, axis=1, keepdims=True) for v in hc]
    ha = [_sigmoid(po_ref[:, cols[h]]) * (hc[h] * lax.rsqrt(var[h] + NORM_EPS) * anw_ref[:, cols[h]])
          for h in range(NH)]
    hb = [(ob[h] * lax.rsqrt(osq[h] + NORM_EPS) * bnw_ref[...]) * _silu(pz_ref[:, cols[h]]) for h in range(NH)]
    ya = _dot(jnp.concatenate(ha, axis=1).astype(bf16), wpa_ref[...])
    yb = _dot(jnp.concatenate(hb, axis=1).astype(bf16), wpb_ref[...])
    y = mg[:, :D_MODEL] * ya + mg[:, D_MODEL:] * yb
    mix = _dot(y.astype(bf16), wo_ref[...])
    o_ref[...] = _ln_rows(ALPHA * x + mix, g_ref[...], b_ref[...])


def _outproj(x, ha, hb, p, anw, bnw, w_merge, w_pa, w_pb, w_out, g, b):
    m = x.shape[0]
    tm = min(m, PROJ_TM)
    row_spec = lambda w: pl.BlockSpec((tm, w), lambda i: (i, 0))
    pcol = lambda c0: pl.BlockSpec((tm, WB), lambda i: (i, c0 // WB))
    const = lambda shape: pl.BlockSpec(shape, lambda i: (0, 0))
    return pl.pallas_call(
        _outproj_kernel,
        grid=(m // tm,),
        in_specs=[row_spec(D_MODEL), row_spec(WB), row_spec(WB), pcol(C_OA), pcol(C_Z),
                  const((1, WB)), const((1, DH)),
                  const((D_MODEL, 2 * D_MODEL)), const((WB, D_MODEL)), const((WB, D_MODEL)),
                  const((D_MODEL, D_MODEL)), const((1, D_MODEL)), const((1, D_MODEL))],
        out_specs=row_spec(D_MODEL),
        out_shape=jax.ShapeDtypeStruct((m, D_MODEL), f32),
        compiler_params=_cparams(("arbitrary",)),
        name="merge_outproj_ln",
    )(x, ha, hb, p, p, anw, bnw, w_merge, w_pa, w_pb, w_out, g, b)


def _ffn2_kernel(long_mode, tm, nmt, blk, valid, *refs):
    if long_mode:
        (x_ref, wa_ref, wb_ref, cwa_ref, cwb_ref, wd_ref, g_ref, b_ref, ha_ref, hb_ref,
         o_ref, sa_ref, sb_ref, hh_ref, ca_ref, cb_ref) = refs

        @pl.when(pl.program_id(0) % nmt == 0)
        def _():
            ca_ref[...] = ha_ref[0]
            cb_ref[...] = hb_ref[0]
    else:
        (x_ref, wa_ref, wb_ref, cwa_ref, cwb_ref, wd_ref, g_ref, b_ref, ea_ref, eb_ref,
         o_ref, sa_ref, sb_ref, hh_ref) = refs
        row = lax.broadcasted_iota(jnp.int32, (tm, FT), 0)
        vm = (row & (blk - 1)) >= (blk - valid)
    xb = x_ref[...].astype(bf16)
    for cb in range(NF):
        cols = slice(cb * FT, (cb + 1) * FT)
        ua = _dot(xb, wa_ref[:, cols])
        ub = _dot(xb, wb_ref[:, cols])
        if long_mode:
            h8a, h8b = ca_ref[:, cols], cb_ref[:, cols]
            ta, tb = ua[tm - HDR:tm], ub[tm - HDR:tm]
            ca_ref[:, cols] = ta
            cb_ref[:, cols] = tb
            sa_ref[0, :, cols] = ta
            sb_ref[0, :, cols] = tb
        else:
            h8a = h8b = None
            ua = jnp.where(vm, ua, ea_ref[:, cols])
            ub = jnp.where(vm, ub, eb_ref[:, cols])
            sa_ref[:, cols] = ua
            sb_ref[:, cols] = ub
        hh = _silu(_conv_rows(ua, h8a, cwa_ref[:, cols])) * _conv_rows(ub, h8b, cwb_ref[:, cols])
        hh_ref[:, cols] = hh.astype(bf16)
    out = _dot(hh_ref[...], wd_ref[...])
    o_ref[...] = _ln_rows(ALPHA * x_ref[...] + out, g_ref[...], b_ref[...])


def _ffn2(x, w_up_a, w_up_b, cw_a, cw_b, w_down, g, b, hist_a=None, hist_b=None,
          e_ab=None, nb=1, blk=CH, valid=CH):
    m = x.shape[0]
    long_mode = hist_a is not None
    seq = m // nb
    tm = min(seq, FFN_TM if long_mode else FFN_TM_BLOCK)
    nmt = seq // tm
    rows = lambda w: pl.BlockSpec((tm, w), lambda i: (i, 0))
    const = lambda shape: pl.BlockSpec(shape, lambda i: (0,) * len(shape))
    in_specs = [rows(D_MODEL), const((D_MODEL, D_FF)), const((D_MODEL, D_FF)), const((CONV_F, D_FF)),
                const((CONV_F, D_FF)), const((D_FF, D_MODEL)), const((1, D_MODEL)), const((1, D_MODEL))]
    scratch = [pltpu.VMEM((tm, D_FF), bf16)]
    if long_mode:
        in_specs += [const((1, HDR, D_FF))] * 2
        st_spec = pl.BlockSpec((1, HDR, D_FF), lambda i: (i, 0, 0))
        st_shape = jax.ShapeDtypeStruct((m // tm, HDR, D_FF), f32)
        scratch += [pltpu.VMEM((HDR, D_FF), f32), pltpu.VMEM((HDR, D_FF), f32)]
        extra = (hist_a, hist_b)
    else:
        in_specs += [pl.BlockSpec((tm, D_FF), lambda i: (i, 0)), pl.BlockSpec((tm, D_FF), lambda i: (i, 1))]
        st_spec = rows(D_FF)
        st_shape = jax.ShapeDtypeStruct((m, D_FF), f32)
        extra = (e_ab, e_ab)
    return pl.pallas_call(
        functools.partial(_ffn2_kernel, long_mode, tm, nmt, blk, valid),
        grid=(m // tm,),
        in_specs=in_specs,
        out_specs=[rows(D_MODEL), st_spec, st_spec],
        out_shape=[jax.ShapeDtypeStruct((m, D_MODEL), f32), st_shape, st_shape],
        scratch_shapes=scratch,
        compiler_params=_cparams(("arbitrary",)),
        name="conv_ffn_ln",
    )(x, w_up_a, w_up_b, cw_a, cw_b, w_down, g, b, *extra)


def _layer_weights(l, w_in, mlstm_gate_bias, mlstm_norm_w, gdn_conv_w, gdn_A_log, gdn_dt_bias, gdn_norm_w,
                   w_branch_a, w_branch_b, w_out, ln1_g, ln1_b, w_up, ffn_conv_w, w_down, ln2_g, ln2_b):
    w = w_in[l].astype(bf16)
    zc = lambda n: jnp.zeros((D_MODEL, n), bf16)
    w_mix = jnp.concatenate([w[:, A_Q:A_I], w[:, B_Z:B_BETA], w[:, A_I:B_QKV], w[:, B_BETA:G_MERGE],
                             zc(128 - 4 * NH), w[:, B_QKV:B_Z]], axis=1)
    z4 = jnp.zeros((NH,), f32)
    gb = jnp.concatenate([mlstm_gate_bias[l], z4, gdn_dt_bias[l], jnp.zeros((128 - 4 * NH,), f32)])[None]
    al = jnp.concatenate([z4, z4, z4, gdn_A_log[l], jnp.zeros((128 - 4 * NH,), f32)])[None]
    return dict(
        w_mix=w_mix, w_merge=w[:, G_MERGE:].astype(bf16), gb=gb, al=al,
        anw=mlstm_norm_w[l][None], bnw=gdn_norm_w[l][None], cw=gdn_conv_w[l],
        w_pa=w_branch_a[l].astype(bf16), w_pb=w_branch_b[l].astype(bf16), w_out=w_out[l].astype(bf16),
        ln1_g=ln1_g[l][None], ln1_b=ln1_b[l][None],
        w_up_a=w_up[l][:, :D_FF].astype(bf16), w_up_b=w_up[l][:, D_FF:].astype(bf16),
        cw_a=ffn_conv_w[l][:, :D_FF], cw_b=ffn_conv_w[l][:, D_FF:],
        w_down=w_down[l].astype(bf16), ln2_g=ln2_g[l][None], ln2_b=ln2_b[l][None])


def _block_layer(x, lw, st, blk, valid, layer=None, prev=None):
    c0, n0, m0, s0, gbuf, fbuf = st
    nseq_tot = n0.shape[0]
    nseq = CH // blk
    ng = nseq_tot // nseq
    front = blk - valid
    m0p = jnp.pad(m0.reshape(ng, nseq, NH), ((0, 0), (0, 8 - nseq), (0, 128 - NH)))
    e = jnp.pad(gbuf, ((0, 0), (front - (CONV_B - 1), valid), (0, 0))).reshape(nseq_tot * blk, 3 * WB)
    ef = jnp.pad(fbuf, ((0, 0), (front - (CONV_F - 1), valid), (0, 0))).reshape(nseq_tot * blk, 2 * D_FF)
    p, ext = _inproj(x, lw["w_mix"], lw["cw"], e=e, blk=blk, valid=valid)
    ha, hb, c, n, m, s = _mixer_block(p, c0, n0, m0p, s0, lw["gb"], lw["al"], blk, valid, layer, prev)
    x1 = _outproj(x, ha, hb, p, lw["anw"], lw["bnw"], lw["w_merge"], lw["w_pa"], lw["w_pb"], lw["w_out"],
                  lw["ln1_g"], lw["ln1_b"])
    x2, ua, ub = _ffn2(x1, lw["w_up_a"], lw["w_up_b"], lw["cw_a"], lw["cw_b"], lw["w_down"],
                       lw["ln2_g"], lw["ln2_b"], e_ab=ef, blk=blk, valid=valid)
    m_new = m[:, :nseq, :NH].reshape(nseq_tot, NH)
    tail = lambda a, w, k: a.reshape(nseq_tot, blk, w)[:, blk - k:]
    gconv = tail(ext, 3 * WB, CONV_B - 1)
    fconv = jnp.concatenate([tail(ua, D_FF, CONV_F - 1), tail(ub, D_FF, CONV_F - 1)], axis=2)
    return x2, (c, n, m_new, s, gconv, fconv)


def _long_layer(x, lw, st, nb, seq):
    c0, n0, m0, s0, gbuf, fbuf = st
    m0p = jnp.pad(m0.reshape(1, 1, NH), ((0, 0), (0, 7), (0, 128 - NH)))
    hist = jnp.pad(gbuf, ((0, 0), (HDR - (CONV_B - 1), 0), (0, 0)))
    hf = jnp.pad(fbuf, ((0, 0), (HDR - (CONV_F - 1), 0), (0, 0)))
    p, tails = _inproj(x, lw["w_mix"], lw["cw"], hist=hist, nb=nb)
    ha, hb, c, n, m, s = _mixer_long(p, c0, n0, m0p, s0, lw["gb"], lw["al"], nb, seq, MIX_T)
    x1 = _outproj(x, ha, hb, p, lw["anw"], lw["bnw"], lw["w_merge"], lw["w_pa"], lw["w_pb"], lw["w_out"],
                  lw["ln1_g"], lw["ln1_b"])
    x2, sa, sb = _ffn2(x1, lw["w_up_a"], lw["w_up_b"], lw["cw_a"], lw["cw_b"], lw["w_down"],
                       lw["ln2_g"], lw["ln2_b"], hist_a=hf[:, :, :D_FF], hist_b=hf[:, :, D_FF:], nb=nb)
    last_tile = lambda a: a.reshape((nb, -1) + a.shape[1:])[:, -1]
    gconv = last_tile(tails)[:, HDR - (CONV_B - 1):]
    fconv = jnp.concatenate([last_tile(sa), last_tile(sb)], axis=2)[:, HDR - (CONV_F - 1):]
    return x2, (c, n, m[:, 0, :NH], s, gconv, fconv)


def kernel(x_prompt, x_sample, state_mlstm_C, state_mlstm_n, state_mlstm_m, state_gdn_S, state_gdn_conv, state_ffn_conv, meta_tokens, ln_emb_g, ln_emb_b, w_in, mlstm_gate_bias, mlstm_norm_w, gdn_conv_w, gdn_A_log, gdn_dt_bias, gdn_norm_w, w_branch_a, w_branch_b, w_out, ln1_g, ln1_b, w_up, ffn_conv_w, w_down, ln2_g, ln2_b):
    nb, seq, _ = x_prompt.shape
    ns, ls, _ = x_sample.shape
    sblk = 8
    assert seq % max(MIX_T, PROJ_TM, FFN_TM) == 0 and ls + CONV_B - 1 <= sblk and ns % (CH // sblk) == 0 and N_META + CONV_B - 1 <= CH
    lws = [_layer_weights(l, w_in, mlstm_gate_bias, mlstm_norm_w, gdn_conv_w, gdn_A_log, gdn_dt_bias,
                          gdn_norm_w, w_branch_a, w_branch_b, w_out, ln1_g, ln1_b, w_up, ffn_conv_w,
                          w_down, ln2_g, ln2_b) for l in range(DEPTH)]
    eg, eb = ln_emb_g[None], ln_emb_b[None]

    xm = _layer_norm(jnp.pad(meta_tokens, ((CH - N_META, 0), (0, 0))), eg, eb)
    xs = _layer_norm(jnp.pad(x_sample, ((0, 0), (sblk - ls, 0), (0, 0))).reshape(ns * sblk, D_MODEL), eg, eb)
    xp = _layer_norm(x_prompt.reshape(nb * seq, D_MODEL), eg, eb)

    zero_st = (jnp.zeros((1, NH, DH, DH), f32), jnp.zeros((1, NH, DH), f32), jnp.zeros((1, NH), f32),
               jnp.zeros((1, NH, DH, DH), f32), jnp.zeros((1, CONV_B - 1, 3 * WB), f32),
               jnp.zeros((1, CONV_F - 1, 2 * D_FF), f32))
    p_states, s_states = [], []
    big = None
    for l in range(DEPTH):
        xm, st_m = _block_layer(xm, lws[l], zero_st, CH, N_META)
        xp, st_p = _long_layer(xp, lws[l], st_m, nb, seq)
        samp_st = (state_mlstm_C, state_mlstm_n[l], state_mlstm_m[l], state_gdn_S,
                   state_gdn_conv[l], state_ffn_conv[l])
        xs, st_s = _block_layer(xs, lws[l], samp_st, sblk, ls, layer=l, prev=big)
        big = (st_s[0], st_s[3])
        p_states.append(st_p)
        s_states.append(st_s)

    stack = lambda states, i: jnp.stack([s[i] for s in states], axis=0)
    y_prompt = xp.reshape(nb, seq, D_MODEL)
    y_sample = xs.reshape(ns, sblk, D_MODEL)[:, sblk - ls:]
    return (y_prompt, y_sample,
            stack(p_states, 0), stack(p_states, 1), stack(p_states, 2), stack(p_states, 3),
            stack(p_states, 4), stack(p_states, 5),
            big[0], stack(s_states, 1), stack(s_states, 2), big[1],
            stack(s_states, 4), stack(s_states, 5))
```

```python
import functools

import jax
import jax.numpy as jnp
from jax import lax
from jax.experimental import pallas as pl
from jax.experimental.pallas import tpu as pltpu

f32 = jnp.float32
bf16 = jnp.bfloat16

D_MODEL = 1024
N_META = 16
NH = 4
DH = 128
WB = NH * DH
CONV_B = 4
D_FF = 2816
CONV_F = 3
DEPTH = 2
ALPHA = (2 * DEPTH) ** 0.25
LN_EPS = 1e-5
NORM_EPS = 1e-6
QSCALE = DH ** -0.5

A_Q = 0
A_I = 4 * WB
B_QKV = A_I + 2 * NH
B_Z = B_QKV + 3 * WB
B_BETA = B_Z + WB
G_MERGE = B_BETA + 2 * NH

C_QA, C_KA, C_VA, C_OA = 0, WB, 2 * WB, 3 * WB
C_Z = 4 * WB
C_G = 5 * WB
C_QKV = C_G + 128
N_MIX = C_QKV + 3 * WB

CH = 64
A_GROUP = 8
HDR = 8
NEG = -1e30
FT = 256
NF = D_FF // FT
FFN_TM_BLOCK = 256
FFN_TM = 512
MIX_T = 512
PROJ_TM = 512
VMEM_LIMIT = 56 * 1024 * 1024


def _cparams(sem):
    return pltpu.CompilerParams(dimension_semantics=sem, vmem_limit_bytes=VMEM_LIMIT)


def _dot(a, b):
    return jnp.dot(a, b, preferred_element_type=f32)


def _dot_nt(a, b):
    return lax.dot_general(a, b, (((1,), (1,)), ((), ())), preferred_element_type=f32)


def _dot_tn(a, b):
    return lax.dot_general(a, b, (((0,), (0,)), ((), ())), preferred_element_type=f32)


def _sel_dot(sel, x):
    hi = x.astype(bf16)
    r1 = x - hi.astype(f32)
    mid = r1.astype(bf16)
    lo = (r1 - mid.astype(f32)).astype(bf16)
    return (_dot(sel, hi) + _dot(sel, mid)) + _dot(sel, lo)


def _softplus(x):
    return jnp.maximum(x, 0.0) + jnp.log1p(jnp.exp(-jnp.abs(x)))


def _sigmoid(x):
    return 1.0 / (1.0 + jnp.exp(-x))


def _silu(x):
    return x * _sigmoid(x)


def _ln_rows(x, g, b):
    mu = jnp.mean(x, axis=-1, keepdims=True)
    xc = x - mu
    var = jnp.mean(xc * xc, axis=-1, keepdims=True)
    return xc * lax.rsqrt(var + LN_EPS) * g + b


def _ln_kernel(x_ref, g_ref, b_ref, o_ref):
    o_ref[...] = _ln_rows(x_ref[...], g_ref[...], b_ref[...])


def _layer_norm(x, g, b):
    m = x.shape[0]
    tm = min(m, 1024)
    return pl.pallas_call(
        _ln_kernel,
        grid=(m // tm,),
        in_specs=[pl.BlockSpec((tm, D_MODEL), lambda i: (i, 0)),
                  pl.BlockSpec((1, D_MODEL), lambda i: (0, 0)),
                  pl.BlockSpec((1, D_MODEL), lambda i: (0, 0))],
        out_specs=pl.BlockSpec((tm, D_MODEL), lambda i: (i, 0)),
        out_shape=jax.ShapeDtypeStruct((m, D_MODEL), f32),
        compiler_params=_cparams(("arbitrary",)),
        name="embed_ln",
    )(x, g, b)


def _conv_rows(u, h8, cw):
    width = cw.shape[0]
    row = lax.broadcasted_iota(jnp.int32, (HDR, u.shape[1]), 0) if h8 is not None else None
    acc = None
    for j in range(width):
        sh = width - 1 - j
        term = u if sh == 0 else pltpu.roll(u, sh, 0)
        if h8 is not None and sh > 0:
            head = term[0:HDR]
            for r in range(sh):
                head = jnp.where(row == r, h8[HDR - sh + r:HDR - sh + r + 1], head)
            term = jnp.concatenate([head, term[HDR:]], axis=0)
        term = term * cw[j:j + 1]
        acc = term if acc is None else acc + term
    return acc


def _inproj_kernel(long_mode, tm, nmt, blk, valid, *refs):
    if long_mode:
        x_ref, w_ref, cw_ref, hist_ref, o_ref, tail_ref, carry_ref = refs

        @pl.when(pl.program_id(0) % nmt == 0)
        def _():
            carry_ref[...] = hist_ref[0]
    else:
        x_ref, w_ref, cw_ref, e_ref, o_ref, tail_ref = refs
        row = lax.broadcasted_iota(jnp.int32, (tm, 2 * DH), 0)
        vm = (row & (blk - 1)) >= (blk - valid)
    xb = x_ref[...].astype(bf16)
    nstep = 3 * NH // 2
    cuts = [C_QA + WB] + [C_KA + (C_QKV - C_KA) * (i + 1) // (nstep - 1) // 128 * 128 for i in range(nstep - 1)]
    cuts[-1] = C_QKV
    for cb in range(nstep):
        cols = slice(cb * 2 * DH, (cb + 1) * 2 * DH)
        u = _dot(xb, w_ref[:, C_QKV + cb * 2 * DH:C_QKV + (cb + 1) * 2 * DH])
        lo = C_QA if cb == 0 else cuts[cb - 1]
        plain = _dot(xb, w_ref[:, lo:cuts[cb]])
        o_ref[:, lo:cuts[cb]] = plain * QSCALE if cb == 0 else plain
        if long_mode:
            h8 = carry_ref[:, cols]
            t8 = u[tm - HDR:tm]
            carry_ref[:, cols] = t8
            tail_ref[0, :, cols] = t8
        else:
            h8 = None
            u = jnp.where(vm, u, e_ref[:, cols])
            tail_ref[:, cols] = u
        c = _silu(_conv_rows(u, h8, cw_ref[:, cols]))
        for half in range(2):
            ch = c[:, half * DH:(half + 1) * DH]
            if cb < NH:
                ch = ch * lax.rsqrt(jnp.sum(ch * ch, axis=1, keepdims=True) + NORM_EPS)
            if cb < NH // 2:
                ch = ch * QSCALE
            c0 = C_QKV + (2 * cb + half) * DH
            o_ref[:, c0:c0 + DH] = ch


def _inproj(x, w_mix, layer, cw, hist=None, e=None, nb=1, blk=CH, valid=CH):
    m = x.shape[0]
    long_mode = hist is not None
    seq = m // nb
    tm = min(seq, PROJ_TM)
    nmt = seq // tm
    rows = lambda w: pl.BlockSpec((tm, w), lambda i: (i, 0))
    const = lambda shape: pl.BlockSpec(shape, lambda i: (0,) * len(shape))
    in_specs = [rows(D_MODEL), pl.BlockSpec((None, D_MODEL, N_MIX), lambda i: (layer, 0, 0)),
                const((CONV_B, 3 * WB))]
    if long_mode:
        in_specs += [const((1, HDR, 3 * WB))]
        tail_spec = pl.BlockSpec((1, HDR, 3 * WB), lambda i: (i, 0, 0))
        tail_shape = jax.ShapeDtypeStruct((m // tm, HDR, 3 * WB), f32)
        scratch = [pltpu.VMEM((HDR, 3 * WB), f32)]
        extra = hist
    else:
        in_specs += [rows(3 * WB)]
        tail_spec = rows(3 * WB)
        tail_shape = jax.ShapeDtypeStruct((m, 3 * WB), f32)
        scratch = []
        extra = e
    return pl.pallas_call(
        functools.partial(_inproj_kernel, long_mode, tm, nmt, blk, valid),
        grid=(m // tm,),
        in_specs=in_specs,
        out_specs=[rows(N_MIX), tail_spec],
        out_shape=[jax.ShapeDtypeStruct((m, N_MIX), f32), tail_shape],
        scratch_shapes=scratch,
        compiler_params=_cparams(("arbitrary",)),
        name="mixer_inproj",
    )(x, w_mix, cw, extra)


def _chunk_consts(blk, valid):
    lg = blk.bit_length() - 1
    r = lax.broadcasted_iota(jnp.int32, (CH, CH), 0)
    c = lax.broadcasted_iota(jnp.int32, (CH, CH), 1)
    same = (r >> lg) == (c >> lg)
    tri = jnp.logical_and(same, c <= r)
    stri = jnp.logical_and(same, c < r)
    sel = lambda cond: jnp.where(cond, 1.0, 0.0).astype(bf16)
    tri_m = sel(tri)
    blk_m = sel(same)
    re = lax.broadcasted_iota(jnp.int32, (CH, 8), 0)
    ce = lax.broadcasted_iota(jnp.int32, (CH, 8), 1)
    expand_m = sel(ce == (re >> lg))
    rs = lax.broadcasted_iota(jnp.int32, (8, CH), 0)
    cs = lax.broadcasted_iota(jnp.int32, (8, CH), 1)
    rowsel_m = sel(cs == (rs << lg) + (blk - 1))
    rv = lax.broadcasted_iota(jnp.int32, (CH, 128), 0)
    valid_m = (rv & (blk - 1)) >= (blk - valid)
    return dict(tri=tri, stri=stri, tri_m=tri_m, blk_m=blk_m,
                expand_m=expand_m, rowsel_m=rowsel_m, valid_m=valid_m)


def _block_max(x, blk):
    if blk == CH:
        return jnp.broadcast_to(jnp.max(x, axis=0, keepdims=True), x.shape)
    x3 = x.reshape(CH // blk, blk, 128)
    return jnp.broadcast_to(jnp.max(x3, axis=1, keepdims=True), x3.shape).reshape(CH, 128)


def _mix_scratch(nc, nseq):
    return [pltpu.VMEM((nc, 6, CH, 128), f32),
            pltpu.VMEM((nc, NH, CH, DH), f32),
            pltpu.VMEM((nc, nseq * NH, DH, DH), f32),
            pltpu.VMEM((nc, nseq, 8, DH), f32),
            pltpu.VMEM((nc, nseq * NH, DH, DH), bf16),
            pltpu.VMEM((nc, nseq * NH, DH, DH), f32),
            pltpu.VMEM((nc, NH, CH, DH), bf16),
            pltpu.VMEM((nc, NH, CH, DH), f32)]


def _phase_a(blk, valid, k, chunks, par, sc):
    nseq = CH // blk
    gbias, alog = par[0], par[1]
    t_ref, numl_ref, kv_ref, nv_ref, m1_ref, m2_ref, qeff_ref, o2_ref = sc
    tri, stri = k["tri"], k["stri"]
    seqs = [slice(i * blk, (i + 1) * blk) for i in range(nseq)]
    nch = len(chunks)
    units = [(c, h) for c in range(nch) for h in range(NH)]
    lane = lax.broadcasted_iota(jnp.int32, (CH, 128), 1)
    cum_lane = jnp.logical_or(jnp.logical_and(lane >= 4, lane < 8), lane >= 12)

    gt = []
    for ci, pr in chunks:
        graw = pr(C_G, 128) + gbias
        g = jnp.where(lane < 4, graw,
                      jnp.where(lane < 8, -_softplus(-graw),
                                jnp.where(lane < 12, _sigmoid(graw),
                                          jnp.where(lane < 16, -jnp.exp(alog) * _softplus(graw), 0.0))))
        if valid < blk:
            g = jnp.where(k["valid_m"], g, jnp.where(lane < 4, NEG, 0.0))
        gt.append(g)
    cs = [_sel_dot(k["tri_m"], g) for g in gt]
    if nseq == 1:
        tot = [jnp.broadcast_to(x[CH - 1:CH, :], (CH, 128)) for x in cs]
    else:
        tot = [_sel_dot(k["blk_m"], g) for g in gt]
    mix_t = [jnp.transpose(jnp.where(cum_lane, cs[c], gt[c])) for c in range(nch)]
    b_all = [pltpu.roll(x, 124, 1) for x in cs]
    btot_all = [pltpu.roll(x, 124, 1) for x in tot]
    wlog = [btot_all[c] - b_all[c] + gt[c] for c in range(nch)]
    mloc_all = [_block_max(x, blk) for x in wlog]
    wsrc_all = [jnp.exp(wlog[c] - mloc_all[c]) for c in range(nch)]
    expg_all = [jnp.exp(x) for x in cs]
    expdiff_all = [jnp.exp(tot[c] - cs[c]) for c in range(nch)]

    qa, ka, va, qg, kgf, vg = {}, {}, {}, {}, {}, {}
    for c, h in units:
        _, pr = chunks[c]
        qa[c, h] = pr(C_QA + h * DH, DH).astype(bf16)
        ka[c, h] = pr(C_KA + h * DH, DH)
        va[c, h] = pr(C_VA + h * DH, DH).astype(bf16)
        qg[c, h] = pr(C_QKV + h * DH, DH)
        kgf[c, h] = pr(C_QKV + WB + h * DH, DH)
        vg[c, h] = pr(C_QKV + 2 * WB + h * DH, DH)
    qgb = {u: qg[u].astype(bf16) for u in units}
    kg = {u: kgf[u].astype(bf16) for u in units}

    sq = {u: _dot_nt(qa[u], ka[u].astype(bf16)) for u in units}
    kk = {u: _dot_nt(kg[u], kg[u]) for u in units}
    qk = {u: _dot_nt(qgb[u], kg[u]) for u in units}

    dm = {(c, h): jnp.where(tri, cs[c][:, 4 + h:5 + h] - mix_t[c][4 + h:5 + h, :] + mix_t[c][h:h + 1, :], NEG)
          for c, h in units}
    dmax = {u: jnp.max(dm[u], axis=1, keepdims=True) for u in units}
    s = {u: sq[u] * jnp.exp(dm[u] - dmax[u]) for u in units}
    denl = {u: jnp.sum(s[u], axis=1, keepdims=True) for u in units}
    sloc = {u: s[u].astype(bf16) for u in units}
    kwl = {(c, h): ka[c, h] * wsrc_all[c][:, h:h + 1] for c, h in units}
    for c in range(nch):
        ci = chunks[c][0]
        dmax_all = jnp.zeros((CH, 128), f32)
        denl_all = jnp.zeros((CH, 128), f32)
        for h in range(NH):
            dmax_all = jnp.where(lane == h, dmax[c, h], dmax_all)
            denl_all = jnp.where(lane == h, denl[c, h], denl_all)
        t_ref[ci, 0] = gt[c]
        t_ref[ci, 1] = b_all[c]
        t_ref[ci, 2] = btot_all[c]
        t_ref[ci, 3] = dmax_all
        t_ref[ci, 4] = denl_all
        t_ref[ci, 5] = mloc_all[c]

    xs, pw, qkd, kd = {}, {}, {}, {}
    for c, h in units:
        beta = gt[c][:, 8 + h:9 + h]
        decay = jnp.exp(jnp.where(tri, cs[c][:, 12 + h:13 + h] - mix_t[c][12 + h:13 + h, :], NEG))
        pw[c, h] = -jnp.where(stri, beta * kk[c, h] * decay, 0.0)
        xs[c, h] = jnp.concatenate([beta * vg[c, h], (beta * expg_all[c][:, 12 + h:13 + h]) * kgf[c, h]], axis=1)
        qkd[c, h] = (qk[c, h] * decay).astype(bf16)
        kd[c, h] = (kgf[c, h] * expdiff_all[c][:, 12 + h:13 + h]).astype(bf16)

    for c, h in units:
        ci = chunks[c][0]
        numl_ref[ci, h] = _dot(sloc[c, h], va[c, h])
        kwb = kwl[c, h].astype(bf16)
        for i, sl in enumerate(seqs):
            kv_ref[ci, i * NH + h] = _dot_tn(kwb[sl], va[c, h][sl])
            nv_ref[ci, i, h:h + 1, :] = jnp.sum(kwl[c, h][sl], axis=0, keepdims=True)

    nsteps = blk.bit_length() - 1
    for it in range(nsteps):
        pwb = {u: pw[u].astype(bf16) for u in units}
        xs = {u: xs[u] + _dot(pwb[u], xs[u].astype(bf16)) for u in units}
        if it + 1 < nsteps:
            pw = {u: _dot(pwb[u], pwb[u]) for u in units}

    xb = {u: xs[u].astype(bf16) for u in units}
    qx = {u: _dot(qkd[u], xb[u]) for u in units}
    for c, h in units:
        ci = chunks[c][0]
        o2_ref[ci, h] = qx[c, h][:, :DH]
        qeff_ref[ci, h] = (expg_all[c][:, 12 + h:13 + h] * qg[c, h] - qx[c, h][:, DH:]).astype(bf16)
        for i, sl in enumerate(seqs):
            mm = _dot_tn(kd[c, h][sl], xb[c, h][sl])
            m2_ref[ci, i * NH + h] = mm[:, :DH]
            m1_ref[ci, i * NH + h] = mm[:, DH:].astype(bf16)


def _phase_b(blk, k, pr, par, st, sc, out, ci):
    nseq = CH // blk
    c_ref, n_ref, m_ref, s_ref = st
    t_ref, numl_ref, kv_ref, nv_ref, m1_ref, m2_ref, qeff_ref, o2_ref = sc
    ha_ref, hb_ref, rows = out
    seqs = [slice(i * blk, (i + 1) * blk) for i in range(nseq)]
    heads = range(NH)
    gt, b_all, btot_all = t_ref[ci, 0], t_ref[ci, 1], t_ref[ci, 2]
    dmax_all, denl_all, mloc_all = t_ref[ci, 3], t_ref[ci, 4], t_ref[ci, 5]

    qa = [pr(C_QA + h * DH, DH) for h in heads]
    qc, ms, oq = [], [], []
    for h in heads:
        qb = qa[h].astype(bf16)
        qc.append([_dot(qb[sl], c_ref[i, h].astype(bf16)) for i, sl in enumerate(seqs)])
    for h in heads:
        qe = qeff_ref[ci, h]
        ms_h, oq_h = [], []
        for i, sl in enumerate(seqs):
            sb = s_ref[i, h].astype(bf16)
            ms_h.append(_dot(m1_ref[ci, i * NH + h], sb))
            oq_h.append(_dot(qe[sl], sb))
        ms.append(ms_h)
        oq.append(oq_h)

    if nseq == 1:
        mprev_all = jnp.broadcast_to(m_ref[0:1, :], (CH, 128))
    else:
        mprev_all = _sel_dot(k["expand_m"], m_ref[...])
    inter = b_all + mprev_all
    mt = jnp.maximum(inter, dmax_all)
    scale_all = jnp.exp(dmax_all - mt)
    winter_all = jnp.exp(inter - mt)
    emt_all = jnp.exp(-mt)
    carry = btot_all + mprev_all
    mnew = jnp.maximum(carry, mloc_all)
    sc2_all = jnp.exp(mloc_all - mnew)
    wold_all = jnp.exp(carry - mnew)
    egt_all = jnp.exp(btot_all)
    if nseq == 1:
        m_ref[0:1, :] = mnew[CH - 1:CH, :]
    else:
        m_ref[...] = _sel_dot(k["rowsel_m"], mnew)

    cat = lambda parts: parts[0] if nseq == 1 else jnp.concatenate(parts, axis=0)
    qn = [cat([jnp.sum(qa[h][sl] * n_ref[i, h:h + 1, :], axis=1, keepdims=True) for i, sl in enumerate(seqs)])
          for h in heads]
    hraw = []
    for h in heads:
        scale, winter = scale_all[:, h:h + 1], winter_all[:, h:h + 1]
        num = scale * numl_ref[ci, h] + winter * cat(qc[h])
        den = scale * denl_all[:, h:h + 1] + winter * qn[h]
        hraw.append(num / jnp.maximum(jnp.abs(den), emt_all[:, h:h + 1]))
    o = [cat(oq[h]) + o2_ref[ci, h] for h in heads]

    for h in heads:
        for i in range(nseq):
            last = (i + 1) * blk - 1
            wold = wold_all[last:last + 1, h:h + 1]
            sc2 = sc2_all[last:last + 1, h:h + 1]
            c_ref[i, h] = wold * c_ref[i, h] + sc2 * kv_ref[ci, i * NH + h]
            n_ref[i, h:h + 1, :] = wold * n_ref[i, h:h + 1, :] + sc2 * nv_ref[ci, i, h:h + 1, :]
            s_ref[i, h] = (egt_all[last:last + 1, 8 + h:9 + h] * s_ref[i, h] - ms[h][i]) + m2_ref[ci, i * NH + h]

    for h in heads:
        ha_ref[rows, h * DH:(h + 1) * DH] = hraw[h]
        hb_ref[rows, h * DH:(h + 1) * DH] = o[h]


def _mixer_long_kernel(t, p_ref, c0_ref, n0_ref, m0_ref, s0_ref, gb_ref, al_ref,
                       ha_ref, hb_ref, c_ref, n_ref, m_ref, s_ref, *sc):
    g = pl.program_id(1)

    @pl.when(g == 0)
    def _():
        c_ref[...] = c0_ref[...]
        n_ref[...] = n0_ref[...]
        m_ref[...] = m0_ref[...]
        s_ref[...] = s0_ref[...]

    k = _chunk_consts(CH, CH)
    par = (gb_ref[...], al_ref[...])
    st = (c_ref, n_ref, _M0(m_ref), s_ref)

    def chunk_views(ci):
        rows = pl.ds(pl.multiple_of(ci * CH, CH), CH)
        return ci, (lambda c0, n: p_ref[rows, c0:c0 + n])

    def body_a(j, carry):
        _phase_a(CH, CH, k, [chunk_views(A_GROUP * j + c) for c in range(A_GROUP)], par, sc)
        return carry

    def body_b(ci, carry):
        rows = pl.ds(pl.multiple_of(ci * CH, CH), CH)
        _phase_b(CH, k, lambda c0, n: p_ref[rows, c0:c0 + n], par, st, sc, (ha_ref, hb_ref, rows), ci)
        return carry

    lax.fori_loop(0, t // (CH * A_GROUP), body_a, 0)
    lax.fori_loop(0, t // CH, body_b, 0)


class _M0:
    def __init__(self, ref):
        self.ref = ref

    def __getitem__(self, idx):
        return self.ref[0] if idx is Ellipsis else self.ref[(0,) + idx]

    def __setitem__(self, idx, val):
        if idx is Ellipsis:
            self.ref[0] = val
        else:
            self.ref[(0,) + idx] = val


def _mixer_long(p, c0, n0, m0, s0, gb, al, nb, seq, t):
    nt = seq // t
    row_spec = lambda w: pl.BlockSpec((t, w), lambda b, g: (b * nt + g, 0))
    const = lambda shape: pl.BlockSpec(shape, lambda b, g: (0,) * len(shape))
    perb = lambda shape: pl.BlockSpec((1,) + shape, lambda b, g: (b,) + (0,) * len(shape))
    return pl.pallas_call(
        functools.partial(_mixer_long_kernel, t),
        grid=(nb, nt),
        in_specs=[row_spec(N_MIX),
                  const((1, NH, DH, DH)), const((1, NH, DH)), const((1, 8, 128)), const((1, NH, DH, DH)),
                  const((1, 128)), const((1, 128))],
        out_specs=[row_spec(WB), row_spec(WB),
                   perb((NH, DH, DH)), perb((NH, DH)), perb((8, 128)), perb((NH, DH, DH))],
        out_shape=[jax.ShapeDtypeStruct((nb * seq, WB), f32), jax.ShapeDtypeStruct((nb * seq, WB), f32),
                   jax.ShapeDtypeStruct((nb, NH, DH, DH), f32), jax.ShapeDtypeStruct((nb, NH, DH), f32),
                   jax.ShapeDtypeStruct((nb, 8, 128), f32), jax.ShapeDtypeStruct((nb, NH, DH, DH), f32)],
        scratch_shapes=_mix_scratch(t // CH, 1),
        compiler_params=_cparams(("arbitrary", "arbitrary")),
        name="mixer_long",
    )(p, c0, n0, m0, s0, gb, al)


def _mixer_block_kernel(blk, valid, n_alias, p_ref, c0_ref, n0_ref, m0_ref, s0_ref, gb_ref, al_ref, *rest):
    ha_ref, hb_ref, c_ref, n_ref, m_ref, s_ref = rest[n_alias:n_alias + 6]
    sc = rest[n_alias + 6:]
    c_ref[...] = c0_ref[...]
    n_ref[...] = n0_ref[...]
    m_ref[...] = m0_ref[...]
    s_ref[...] = s0_ref[...]
    k = _chunk_consts(blk, valid)
    par = (gb_ref[...], al_ref[...])
    pr = lambda c0, n: p_ref[:, c0:c0 + n]
    _phase_a(blk, valid, k, [(0, pr)], par, sc)
    _phase_b(blk, k, pr, par, (c_ref, n_ref, _M0(m_ref), s_ref), sc, (ha_ref, hb_ref, slice(0, CH)), 0)


def _mixer_block(p, c0, n0, m0, s0, gb, al, blk, valid, layer=None, prev=None):
    ng = p.shape[0] // CH
    nseq = CH // blk
    row_spec = lambda w: pl.BlockSpec((CH, w), lambda g: (g, 0))
    const = lambda shape: pl.BlockSpec(shape, lambda g: (0,) * len(shape))
    perg = lambda shape: pl.BlockSpec(shape, lambda g: (g,) + (0,) * (len(shape) - 1))
    big_shape = (ng * nseq, NH, DH, DH)
    if layer is None:
        big = perg((nseq, NH, DH, DH))
    else:
        big = pl.BlockSpec((None, nseq, NH, DH, DH), lambda g: (layer, g, 0, 0, 0))
        big_shape = (c0.shape[0],) + big_shape
    aliased = () if prev is None else tuple(prev)
    n_in = 7
    return pl.pallas_call(
        functools.partial(_mixer_block_kernel, blk, valid, len(aliased)),
        grid=(ng,),
        in_specs=[row_spec(N_MIX), big, perg((nseq, NH, DH)), perg((1, 8, 128)), big,
                  const((1, 128)), const((1, 128))]
                 + [pl.BlockSpec(memory_space=pl.ANY)] * len(aliased),
        out_specs=[row_spec(WB), row_spec(WB), big, perg((nseq, NH, DH)), perg((1, 8, 128)), big],
        out_shape=[jax.ShapeDtypeStruct((ng * CH, WB), f32), jax.ShapeDtypeStruct((ng * CH, WB), f32),
                   jax.ShapeDtypeStruct(big_shape, f32), jax.ShapeDtypeStruct((ng * nseq, NH, DH), f32),
                   jax.ShapeDtypeStruct((ng, 8, 128), f32), jax.ShapeDtypeStruct(big_shape, f32)],
        input_output_aliases={n_in: 2, n_in + 1: 5} if aliased else {},
        scratch_shapes=_mix_scratch(1, nseq),
        compiler_params=_cparams(("arbitrary",)),
        name="mixer_block",
    )(p, c0, n0, m0, s0, gb, al, *aliased)


def _outproj_kernel(x_ref, ha_ref, hb_ref, po_ref, pz_ref, anw_ref, bnw_ref,
                    wm_ref, wpa_ref, wpb_ref, wo_ref, g_ref, b_ref, o_ref):
    x = x_ref[...]
    mg = _sigmoid(_dot(x.astype(bf16), wm_ref[...]))
    cols = [slice(h * DH, (h + 1) * DH) for h in range(NH)]
    hr = [ha_ref[:, c] for c in cols]
    ob = [hb_ref[:, c] for c in cols]
    mu = [jnp.mean(v, axis=1, keepdims=True) for v in hr]
    osq = [jnp.mean(v * v, axis=1, keepdims=True) for v in ob]
    hc = [hr[h] - mu[h] for h in range(NH)]
    var = [jnp.mean(v * v, axis=1, keepdims=True) for v in hc]
    ha = [_sigmoid(po_ref[:, cols[h]]) * (hc[h] * lax.rsqrt(var[h] + NORM_EPS) * anw_ref[:, cols[h]])
          for h in range(NH)]
    hb = [(ob[h] * lax.rsqrt(osq[h] + NORM_EPS) * bnw_ref[...]) * _silu(pz_ref[:, cols[h]]) for h in range(NH)]
    ya = _dot(jnp.concatenate(ha, axis=1).astype(bf16), wpa_ref[...])
    yb = _dot(jnp.concatenate(hb, axis=1).astype(bf16), wpb_ref[...])
    y = mg[:, :D_MODEL] * ya + mg[:, D_MODEL:] * yb
    mix = _dot(y.astype(bf16), wo_ref[...])
    o_ref[...] = _ln_rows(ALPHA * x + mix, g_ref[...], b_ref[...])


def _outproj(x, ha, hb, p, anw, bnw, w_merge, layer, w_pa, w_pb, w_out, g, b):
    m = x.shape[0]
    tm = min(m, PROJ_TM)
    row_spec = lambda w: pl.BlockSpec((tm, w), lambda i: (i, 0))
    pcol = lambda c0: pl.BlockSpec((tm, WB), lambda i: (i, c0 // WB))
    const = lambda shape: pl.BlockSpec(shape, lambda i: (0, 0))
    return pl.pallas_call(
        _outproj_kernel,
        grid=(m // tm,),
        in_specs=[row_spec(D_MODEL), row_spec(WB), row_spec(WB), pcol(C_OA), pcol(C_Z),
                  const((1, WB)), const((1, DH)),
                  pl.BlockSpec((None, D_MODEL, 2 * D_MODEL), lambda i: (layer, 0, 0)),
                  const((WB, D_MODEL)), const((WB, D_MODEL)),
                  const((D_MODEL, D_MODEL)), const((1, D_MODEL)), const((1, D_MODEL))],
        out_specs=row_spec(D_MODEL),
        out_shape=jax.ShapeDtypeStruct((m, D_MODEL), f32),
        compiler_params=_cparams(("arbitrary",)),
        name="merge_outproj_ln",
    )(x, ha, hb, p, p, anw, bnw, w_merge, w_pa, w_pb, w_out, g, b)


def _ffn2_kernel(long_mode, tm, nmt, blk, valid, *refs):
    if long_mode:
        (x_ref, wa_ref, wb_ref, cwa_ref, cwb_ref, wd_ref, g_ref, b_ref, ha_ref, hb_ref,
         o_ref, sa_ref, sb_ref, hh_ref, ca_ref, cb_ref) = refs

        @pl.when(pl.program_id(0) % nmt == 0)
        def _():
            ca_ref[...] = ha_ref[0]
            cb_ref[...] = hb_ref[0]
    else:
        (x_ref, wa_ref, wb_ref, cwa_ref, cwb_ref, wd_ref, g_ref, b_ref, ea_ref, eb_ref,
         o_ref, sa_ref, sb_ref, hh_ref) = refs
        row = lax.broadcasted_iota(jnp.int32, (tm, FT), 0)
        vm = (row & (blk - 1)) >= (blk - valid)
    xb = x_ref[...].astype(bf16)
    for cb in range(NF):
        cols = slice(cb * FT, (cb + 1) * FT)
        ua = _dot(xb, wa_ref[:, cols])
        ub = _dot(xb, wb_ref[:, cols])
        if long_mode:
            h8a, h8b = ca_ref[:, cols], cb_ref[:, cols]
            ta, tb = ua[tm - HDR:tm], ub[tm - HDR:tm]
            ca_ref[:, cols] = ta
            cb_ref[:, cols] = tb
            sa_ref[0, :, cols] = ta
            sb_ref[0, :, cols] = tb
        else:
            h8a = h8b = None
            ua = jnp.where(vm, ua, ea_ref[:, cols])
            ub = jnp.where(vm, ub, eb_ref[:, cols])
            sa_ref[:, cols] = ua
            sb_ref[:, cols] = ub
        hh = _silu(_conv_rows(ua, h8a, cwa_ref[:, cols])) * _conv_rows(ub, h8b, cwb_ref[:, cols])
        hh_ref[:, cols] = hh.astype(bf16)
    out = _dot(hh_ref[...], wd_ref[...])
    o_ref[...] = _ln_rows(ALPHA * x_ref[...] + out, g_ref[...], b_ref[...])


def _ffn2(x, w_up_a, w_up_b, cw_a, cw_b, w_down, g, b, hist_a=None, hist_b=None,
          e_ab=None, nb=1, blk=CH, valid=CH):
    m = x.shape[0]
    long_mode = hist_a is not None
    seq = m // nb
    tm = min(seq, FFN_TM if long_mode else FFN_TM_BLOCK)
    nmt = seq // tm
    rows = lambda w: pl.BlockSpec((tm, w), lambda i: (i, 0))
    const = lambda shape: pl.BlockSpec(shape, lambda i: (0,) * len(shape))
    in_specs = [rows(D_MODEL), const((D_MODEL, D_FF)), const((D_MODEL, D_FF)), const((CONV_F, D_FF)),
                const((CONV_F, D_FF)), const((D_FF, D_MODEL)), const((1, D_MODEL)), const((1, D_MODEL))]
    scratch = [pltpu.VMEM((tm, D_FF), bf16)]
    if long_mode:
        in_specs += [const((1, HDR, D_FF))] * 2
        st_spec = pl.BlockSpec((1, HDR, D_FF), lambda i: (i, 0, 0))
        st_shape = jax.ShapeDtypeStruct((m // tm, HDR, D_FF), f32)
        scratch += [pltpu.VMEM((HDR, D_FF), f32), pltpu.VMEM((HDR, D_FF), f32)]
        extra = (hist_a, hist_b)
    else:
        in_specs += [pl.BlockSpec((tm, D_FF), lambda i: (i, 0)), pl.BlockSpec((tm, D_FF), lambda i: (i, 1))]
        st_spec = rows(D_FF)
        st_shape = jax.ShapeDtypeStruct((m, D_FF), f32)
        extra = (e_ab, e_ab)
    return pl.pallas_call(
        functools.partial(_ffn2_kernel, long_mode, tm, nmt, blk, valid),
        grid=(m // tm,),
        in_specs=in_specs,
        out_specs=[rows(D_MODEL), st_spec, st_spec],
        out_shape=[jax.ShapeDtypeStruct((m, D_MODEL), f32), st_shape, st_shape],
        scratch_shapes=scratch,
        compiler_params=_cparams(("arbitrary",)),
        name="conv_ffn_ln",
    )(x, w_up_a, w_up_b, cw_a, cw_b, w_down, g, b, *extra)


def _mix_weights(w_in):
    w = w_in.astype(bf16)
    zc = jnp.zeros(w.shape[:2] + (128 - 4 * NH,), bf16)
    w_mix = jnp.concatenate([w[..., A_Q:A_I], w[..., B_Z:B_BETA], w[..., A_I:B_QKV], w[..., B_BETA:G_MERGE], zc,
                             w[..., B_QKV:B_Z]], axis=-1)
    return w_mix, w[..., G_MERGE:]


def _layer_weights(l, w_mix, w_merge, mlstm_gate_bias, mlstm_norm_w, gdn_conv_w, gdn_A_log, gdn_dt_bias, gdn_norm_w,
                   w_branch_a, w_branch_b, w_out, ln1_g, ln1_b, w_up, ffn_conv_w, w_down, ln2_g, ln2_b):
    z4 = jnp.zeros((NH,), f32)
    gb = jnp.concatenate([mlstm_gate_bias[l], z4, gdn_dt_bias[l], jnp.zeros((128 - 4 * NH,), f32)])[None]
    al = jnp.concatenate([z4, z4, z4, gdn_A_log[l], jnp.zeros((128 - 4 * NH,), f32)])[None]
    return dict(
        layer=l, w_mix=w_mix, w_merge=w_merge, gb=gb, al=al,
        anw=mlstm_norm_w[l][None], bnw=gdn_norm_w[l][None], cw=gdn_conv_w[l],
        w_pa=w_branch_a[l].astype(bf16), w_pb=w_branch_b[l].astype(bf16), w_out=w_out[l].astype(bf16),
        ln1_g=ln1_g[l][None], ln1_b=ln1_b[l][None],
        w_up_a=w_up[l][:, :D_FF].astype(bf16), w_up_b=w_up[l][:, D_FF:].astype(bf16),
        cw_a=ffn_conv_w[l][:, :D_FF], cw_b=ffn_conv_w[l][:, D_FF:],
        w_down=w_down[l].astype(bf16), ln2_g=ln2_g[l][None], ln2_b=ln2_b[l][None])


def _block_layer(x, lw, st, blk, valid, layer=None, prev=None):
    c0, n0, m0, s0, gbuf, fbuf = st
    nseq_tot = n0.shape[0]
    nseq = CH // blk
    ng = nseq_tot // nseq
    front = blk - valid
    m0p = jnp.pad(m0.reshape(ng, nseq, NH), ((0, 0), (0, 8 - nseq), (0, 128 - NH)))
    e = jnp.pad(gbuf, ((0, 0), (front - (CONV_B - 1), valid), (0, 0))).reshape(nseq_tot * blk, 3 * WB)
    ef = jnp.pad(fbuf, ((0, 0), (front - (CONV_F - 1), valid), (0, 0))).reshape(nseq_tot * blk, 2 * D_FF)
    p, ext = _inproj(x, lw["w_mix"], lw["layer"], lw["cw"], e=e, blk=blk, valid=valid)
    ha, hb, c, n, m, s = _mixer_block(p, c0, n0, m0p, s0, lw["gb"], lw["al"], blk, valid, layer, prev)
    x1 = _outproj(x, ha, hb, p, lw["anw"], lw["bnw"], lw["w_merge"], lw["layer"], lw["w_pa"], lw["w_pb"],
                  lw["w_out"],
                  lw["ln1_g"], lw["ln1_b"])
    x2, ua, ub = _ffn2(x1, lw["w_up_a"], lw["w_up_b"], lw["cw_a"], lw["cw_b"], lw["w_down"],
                       lw["ln2_g"], lw["ln2_b"], e_ab=ef, blk=blk, valid=valid)
    m_new = m[:, :nseq, :NH].reshape(nseq_tot, NH)
    tail = lambda a, w, k: a.reshape(nseq_tot, blk, w)[:, blk - k:]
    gconv = tail(ext, 3 * WB, CONV_B - 1)
    fconv = jnp.concatenate([tail(ua, D_FF, CONV_F - 1), tail(ub, D_FF, CONV_F - 1)], axis=2)
    return x2, (c, n, m_new, s, gconv, fconv)


def _long_layer(x, lw, st, nb, seq):
    c0, n0, m0, s0, gbuf, fbuf = st
    m0p = jnp.pad(m0.reshape(1, 1, NH), ((0, 0), (0, 7), (0, 128 - NH)))
    hist = jnp.pad(gbuf, ((0, 0), (HDR - (CONV_B - 1), 0), (0, 0)))
    hf = jnp.pad(fbuf, ((0, 0), (HDR - (CONV_F - 1), 0), (0, 0)))
    p, tails = _inproj(x, lw["w_mix"], lw["layer"], lw["cw"], hist=hist, nb=nb)
    ha, hb, c, n, m, s = _mixer_long(p, c0, n0, m0p, s0, lw["gb"], lw["al"], nb, seq, MIX_T)
    x1 = _outproj(x, ha, hb, p, lw["anw"], lw["bnw"], lw["w_merge"], lw["layer"], lw["w_pa"], lw["w_pb"],
                  lw["w_out"],
                  lw["ln1_g"], lw["ln1_b"])
    x2, sa, sb = _ffn2(x1, lw["w_up_a"], lw["w_up_b"], lw["cw_a"], lw["cw_b"], lw["w_down"],
                       lw["ln2_g"], lw["ln2_b"], hist_a=hf[:, :, :D_FF], hist_b=hf[:, :, D_FF:], nb=nb)
    last_tile = lambda a: a.reshape((nb, -1) + a.shape[1:])[:, -1]
    gconv = last_tile(tails)[:, HDR - (CONV_B - 1):]
    fconv = jnp.concatenate([last_tile(sa), last_tile(sb)], axis=2)[:, HDR - (CONV_F - 1):]
    return x2, (c, n, m[:, 0, :NH], s, gconv, fconv)


def kernel(x_prompt, x_sample, state_mlstm_C, state_mlstm_n, state_mlstm_m, state_gdn_S, state_gdn_conv, state_ffn_conv, meta_tokens, ln_emb_g, ln_emb_b, w_in, mlstm_gate_bias, mlstm_norm_w, gdn_conv_w, gdn_A_log, gdn_dt_bias, gdn_norm_w, w_branch_a, w_branch_b, w_out, ln1_g, ln1_b, w_up, ffn_conv_w, w_down, ln2_g, ln2_b):
    nb, seq, _ = x_prompt.shape
    ns, ls, _ = x_sample.shape
    sblk = 8
    assert seq % max(MIX_T, PROJ_TM, FFN_TM) == 0 and ls + CONV_B - 1 <= sblk and ns % (CH // sblk) == 0 and N_META + CONV_B - 1 <= CH
    w_mix, w_merge = _mix_weights(w_in)
    lws = [_layer_weights(l, w_mix, w_merge, mlstm_gate_bias, mlstm_norm_w, gdn_conv_w, gdn_A_log, gdn_dt_bias,
                          gdn_norm_w, w_branch_a, w_branch_b, w_out, ln1_g, ln1_b, w_up, ffn_conv_w,
                          w_down, ln2_g, ln2_b) for l in range(DEPTH)]
    eg, eb = ln_emb_g[None], ln_emb_b[None]

    xm = _layer_norm(jnp.pad(meta_tokens, ((CH - N_META, 0), (0, 0))), eg, eb)
    xs = _layer_norm(jnp.pad(x_sample, ((0, 0), (sblk - ls, 0), (0, 0))).reshape(ns * sblk, D_MODEL), eg, eb)
    xp = _layer_norm(x_prompt.reshape(nb * seq, D_MODEL), eg, eb)

    zero_st = (jnp.zeros((1, NH, DH, DH), f32), jnp.zeros((1, NH, DH), f32), jnp.zeros((1, NH), f32),
               jnp.zeros((1, NH, DH, DH), f32), jnp.zeros((1, CONV_B - 1, 3 * WB), f32),
               jnp.zeros((1, CONV_F - 1, 2 * D_FF), f32))
    p_states, s_states = [], []
    big = None
    for l in range(DEPTH):
        xm, st_m = _block_layer(xm, lws[l], zero_st, CH, N_META)
        xp, st_p = _long_layer(xp, lws[l], st_m, nb, seq)
        samp_st = (state_mlstm_C, state_mlstm_n[l], state_mlstm_m[l], state_gdn_S,
                   state_gdn_conv[l], state_ffn_conv[l])
        xs, st_s = _block_layer(xs, lws[l], samp_st, sblk, ls, layer=l, prev=big)
        big = (st_s[0], st_s[3])
        p_states.append(st_p)
        s_states.append(st_s)

    stack = lambda states, i: jnp.stack([s[i] for s in states], axis=0)
    y_prompt = xp.reshape(nb, seq, D_MODEL)
    y_sample = xs.reshape(ns, sblk, D_MODEL)[:, sblk - ls:]
    return (y_prompt, y_sample,
            stack(p_states, 0), stack(p_states, 1), stack(p_states, 2), stack(p_states, 3),
            stack(p_states, 4), stack(p_states, 5),
            big[0], stack(s_states, 1), stack(s_states, 2), big[1],
            stack(s_states, 4), stack(s_states, 5))
```

```python
import functools

import jax
import jax.numpy as jnp
from jax import lax
from jax.experimental import pallas as pl
from jax.experimental.pallas import tpu as pltpu

f32 = jnp.float32
bf16 = jnp.bfloat16

D_MODEL = 1024
N_META = 16
NH = 4
DH = 128
WB = NH * DH
CONV_B = 4
D_FF = 2816
CONV_F = 3
DEPTH = 2
ALPHA = (2 * DEPTH) ** 0.25
LN_EPS = 1e-5
NORM_EPS = 1e-6
QSCALE = DH ** -0.5

A_Q = 0
A_I = 4 * WB
B_QKV = A_I + 2 * NH
B_Z = B_QKV + 3 * WB
B_BETA = B_Z + WB
G_MERGE = B_BETA + 2 * NH

C_QA, C_KA, C_VA, C_OA = 0, WB, 2 * WB, 3 * WB
C_Z = 4 * WB
C_G = 5 * WB
C_QKV = C_G + 128
N_MIX = C_QKV + 3 * WB

CH = 64
SUB = 16
A_GROUP = 8
HDR = 8
NEG = -1e30
FT = 256
NF = D_FF // FT
FFN_TM_BLOCK = 256
FFN_TM = 512
MIX_T = 512
PROJ_TM = 512
VMEM_LIMIT = 56 * 1024 * 1024


def _cparams(sem):
    return pltpu.CompilerParams(dimension_semantics=sem, vmem_limit_bytes=VMEM_LIMIT)


def _dot(a, b):
    return jnp.dot(a, b, preferred_element_type=f32)


def _dot_nt(a, b):
    return lax.dot_general(a, b, (((1,), (1,)), ((), ())), preferred_element_type=f32)


def _dot_tn(a, b):
    return lax.dot_general(a, b, (((0,), (0,)), ((), ())), preferred_element_type=f32)


def _sel_dot(sel, x):
    hi = x.astype(bf16)
    r1 = x - hi.astype(f32)
    mid = r1.astype(bf16)
    lo = (r1 - mid.astype(f32)).astype(bf16)
    return (_dot(sel, hi) + _dot(sel, mid)) + _dot(sel, lo)


def _softplus(x):
    return jnp.maximum(x, 0.0) + jnp.log1p(jnp.exp(-jnp.abs(x)))


def _sigmoid(x):
    return 1.0 / (1.0 + jnp.exp(-x))


def _silu(x):
    return x * _sigmoid(x)


def _ln_rows(x, g, b):
    mu = jnp.mean(x, axis=-1, keepdims=True)
    xc = x - mu
    var = jnp.mean(xc * xc, axis=-1, keepdims=True)
    return xc * lax.rsqrt(var + LN_EPS) * g + b


def _ln_kernel(x_ref, g_ref, b_ref, o_ref):
    o_ref[...] = _ln_rows(x_ref[...], g_ref[...], b_ref[...])


def _layer_norm(x, g, b):
    m = x.shape[0]
    tm = min(m, 1024)
    return pl.pallas_call(
        _ln_kernel,
        grid=(m // tm,),
        in_specs=[pl.BlockSpec((tm, D_MODEL), lambda i: (i, 0)),
                  pl.BlockSpec((1, D_MODEL), lambda i: (0, 0)),
                  pl.BlockSpec((1, D_MODEL), lambda i: (0, 0))],
        out_specs=pl.BlockSpec((tm, D_MODEL), lambda i: (i, 0)),
        out_shape=jax.ShapeDtypeStruct((m, D_MODEL), f32),
        compiler_params=_cparams(("arbitrary",)),
        name="embed_ln",
    )(x, g, b)


def _conv_rows(u, h8, cw):
    width = cw.shape[0]
    row = lax.broadcasted_iota(jnp.int32, (HDR, u.shape[1]), 0) if h8 is not None else None
    acc = None
    for j in range(width):
        sh = width - 1 - j
        term = u if sh == 0 else pltpu.roll(u, sh, 0)
        if h8 is not None and sh > 0:
            head = term[0:HDR]
            for r in range(sh):
                head = jnp.where(row == r, h8[HDR - sh + r:HDR - sh + r + 1], head)
            term = jnp.concatenate([head, term[HDR:]], axis=0)
        term = term * cw[j:j + 1]
        acc = term if acc is None else acc + term
    return acc


def _inproj_kernel(long_mode, tm, nmt, blk, valid, *refs):
    if long_mode:
        x_ref, w_ref, cw_ref, hist_ref, o_ref, tail_ref, carry_ref = refs

        @pl.when(pl.program_id(0) % nmt == 0)
        def _():
            carry_ref[...] = hist_ref[0]
    else:
        x_ref, w_ref, cw_ref, e_ref, o_ref, tail_ref = refs
        row = lax.broadcasted_iota(jnp.int32, (tm, 2 * DH), 0)
        vm = (row & (blk - 1)) >= (blk - valid)
    xb = x_ref[...].astype(bf16)
    nstep = 3 * NH // 2
    cuts = [C_QA + WB] + [C_KA + (C_QKV - C_KA) * (i + 1) // (nstep - 1) // 128 * 128 for i in range(nstep - 1)]
    cuts[-1] = C_QKV
    for cb in range(nstep):
        cols = slice(cb * 2 * DH, (cb + 1) * 2 * DH)
        u = _dot(xb, w_ref[:, C_QKV + cb * 2 * DH:C_QKV + (cb + 1) * 2 * DH])
        lo = C_QA if cb == 0 else cuts[cb - 1]
        plain = _dot(xb, w_ref[:, lo:cuts[cb]])
        o_ref[:, lo:cuts[cb]] = plain * QSCALE if cb == 0 else plain
        if long_mode:
            h8 = carry_ref[:, cols]
            t8 = u[tm - HDR:tm]
            carry_ref[:, cols] = t8
            tail_ref[0, :, cols] = t8
        else:
            h8 = None
            u = jnp.where(vm, u, e_ref[:, cols])
            tail_ref[:, cols] = u
        c = _silu(_conv_rows(u, h8, cw_ref[:, cols]))
        for half in range(2):
            ch = c[:, half * DH:(half + 1) * DH]
            if cb < NH:
                ch = ch * lax.rsqrt(jnp.sum(ch * ch, axis=1, keepdims=True) + NORM_EPS)
            if cb < NH // 2:
                ch = ch * QSCALE
            c0 = C_QKV + (2 * cb + half) * DH
            o_ref[:, c0:c0 + DH] = ch


def _inproj(x, w_mix, layer, cw, hist=None, e=None, nb=1, blk=CH, valid=CH):
    m = x.shape[0]
    long_mode = hist is not None
    seq = m // nb
    tm = min(seq, PROJ_TM)
    nmt = seq // tm
    rows = lambda w: pl.BlockSpec((tm, w), lambda i: (i, 0))
    const = lambda shape: pl.BlockSpec(shape, lambda i: (0,) * len(shape))
    in_specs = [rows(D_MODEL), pl.BlockSpec((None, D_MODEL, N_MIX), lambda i: (layer, 0, 0)),
                const((CONV_B, 3 * WB))]
    if long_mode:
        in_specs += [const((1, HDR, 3 * WB))]
        tail_spec = pl.BlockSpec((1, HDR, 3 * WB), lambda i: (i, 0, 0))
        tail_shape = jax.ShapeDtypeStruct((m // tm, HDR, 3 * WB), f32)
        scratch = [pltpu.VMEM((HDR, 3 * WB), f32)]
        extra = hist
    else:
        in_specs += [rows(3 * WB)]
        tail_spec = rows(3 * WB)
        tail_shape = jax.ShapeDtypeStruct((m, 3 * WB), f32)
        scratch = []
        extra = e
    return pl.pallas_call(
        functools.partial(_inproj_kernel, long_mode, tm, nmt, blk, valid),
        grid=(m // tm,),
        in_specs=in_specs,
        out_specs=[rows(N_MIX), tail_spec],
        out_shape=[jax.ShapeDtypeStruct((m, N_MIX), f32), tail_shape],
        scratch_shapes=scratch,
        compiler_params=_cparams(("arbitrary",)),
        name="mixer_inproj",
    )(x, w_mix, cw, extra)


def _chunk_consts(blk, valid):
    lg = blk.bit_length() - 1
    r = lax.broadcasted_iota(jnp.int32, (CH, CH), 0)
    c = lax.broadcasted_iota(jnp.int32, (CH, CH), 1)
    same = (r >> lg) == (c >> lg)
    tri = jnp.logical_and(same, c <= r)
    stri = jnp.logical_and(same, c < r)
    sel = lambda cond: jnp.where(cond, 1.0, 0.0).astype(bf16)
    tri_m = sel(tri)
    blk_m = sel(same)
    re = lax.broadcasted_iota(jnp.int32, (CH, 8), 0)
    ce = lax.broadcasted_iota(jnp.int32, (CH, 8), 1)
    expand_m = sel(ce == (re >> lg))
    rs = lax.broadcasted_iota(jnp.int32, (8, CH), 0)
    cs = lax.broadcasted_iota(jnp.int32, (8, CH), 1)
    rowsel_m = sel(cs == (rs << lg) + (blk - 1))
    rv = lax.broadcasted_iota(jnp.int32, (CH, 128), 0)
    valid_m = (rv & (blk - 1)) >= (blk - valid)
    lgs = min(SUB, blk).bit_length() - 1
    same_sub = (r >> lgs) == (c >> lgs)
    eye = jnp.where(r == c, 1.0, 0.0)
    return dict(tri=tri, stri=stri, tri_m=tri_m, blk_m=blk_m, same_sub=same_sub, eye=eye,
                expand_m=expand_m, rowsel_m=rowsel_m, valid_m=valid_m)


def _block_max(x, blk):
    if blk == CH:
        return jnp.broadcast_to(jnp.max(x, axis=0, keepdims=True), x.shape)
    x3 = x.reshape(CH // blk, blk, 128)
    return jnp.broadcast_to(jnp.max(x3, axis=1, keepdims=True), x3.shape).reshape(CH, 128)


def _mix_scratch(nc, nseq):
    return [pltpu.VMEM((nc, 6, CH, 128), f32),
            pltpu.VMEM((nc, NH, CH, DH), f32),
            pltpu.VMEM((nc, nseq * NH, DH, DH), f32),
            pltpu.VMEM((nc, nseq, 8, DH), f32),
            pltpu.VMEM((nc, nseq * NH, DH, DH), bf16),
            pltpu.VMEM((nc, nseq * NH, DH, DH), f32),
            pltpu.VMEM((nc, NH, CH, DH), bf16),
            pltpu.VMEM((nc, NH, CH, DH), f32)]


def _phase_a(blk, valid, k, chunks, par, sc):
    nseq = CH // blk
    gbias, alog = par[0], par[1]
    t_ref, numl_ref, kv_ref, nv_ref, m1_ref, m2_ref, qeff_ref, o2_ref = sc
    tri, stri = k["tri"], k["stri"]
    seqs = [slice(i * blk, (i + 1) * blk) for i in range(nseq)]
    nch = len(chunks)
    units = [(c, h) for c in range(nch) for h in range(NH)]
    lane = lax.broadcasted_iota(jnp.int32, (CH, 128), 1)
    cum_lane = jnp.logical_or(jnp.logical_and(lane >= 4, lane < 8), lane >= 12)

    gt = []
    for ci, pr in chunks:
        graw = pr(C_G, 128) + gbias
        g = jnp.where(lane < 4, graw,
                      jnp.where(lane < 8, -_softplus(-graw),
                                jnp.where(lane < 12, _sigmoid(graw),
                                          jnp.where(lane < 16, -jnp.exp(alog) * _softplus(graw), 0.0))))
        if valid < blk:
            g = jnp.where(k["valid_m"], g, jnp.where(lane < 4, NEG, 0.0))
        gt.append(g)
    cs = [_sel_dot(k["tri_m"], g) for g in gt]
    if nseq == 1:
        tot = [jnp.broadcast_to(x[CH - 1:CH, :], (CH, 128)) for x in cs]
    else:
        tot = [_sel_dot(k["blk_m"], g) for g in gt]
    mix_t = [jnp.transpose(jnp.where(cum_lane, cs[c], gt[c])) for c in range(nch)]
    b_all = [pltpu.roll(x, 124, 1) for x in cs]
    btot_all = [pltpu.roll(x, 124, 1) for x in tot]
    wlog = [btot_all[c] - b_all[c] + gt[c] for c in range(nch)]
    mloc_all = [_block_max(x, blk) for x in wlog]
    wsrc_all = [jnp.exp(wlog[c] - mloc_all[c]) for c in range(nch)]
    expg_all = [jnp.exp(x) for x in cs]
    expdiff_all = [jnp.exp(tot[c] - cs[c]) for c in range(nch)]

    qa, ka, va, qg, kgf, vg = {}, {}, {}, {}, {}, {}
    for c, h in units:
        _, pr = chunks[c]
        qa[c, h] = pr(C_QA + h * DH, DH).astype(bf16)
        ka[c, h] = pr(C_KA + h * DH, DH)
        va[c, h] = pr(C_VA + h * DH, DH).astype(bf16)
        qg[c, h] = pr(C_QKV + h * DH, DH)
        kgf[c, h] = pr(C_QKV + WB + h * DH, DH)
        vg[c, h] = pr(C_QKV + 2 * WB + h * DH, DH)
    qgb = {u: qg[u].astype(bf16) for u in units}
    kg = {u: kgf[u].astype(bf16) for u in units}

    sq = {u: _dot_nt(qa[u], ka[u].astype(bf16)) for u in units}
    kk = {u: _dot_nt(kg[u], kg[u]) for u in units}
    qk = {u: _dot_nt(qgb[u], kg[u]) for u in units}

    dm = {(c, h): jnp.where(tri, cs[c][:, 4 + h:5 + h] - mix_t[c][4 + h:5 + h, :] + mix_t[c][h:h + 1, :], NEG)
          for c, h in units}
    dmax = {u: jnp.max(dm[u], axis=1, keepdims=True) for u in units}
    s = {u: sq[u] * jnp.exp(dm[u] - dmax[u]) for u in units}
    denl = {u: jnp.sum(s[u], axis=1, keepdims=True) for u in units}
    sloc = {u: s[u].astype(bf16) for u in units}
    kwl = {(c, h): ka[c, h] * wsrc_all[c][:, h:h + 1] for c, h in units}
    for c in range(nch):
        ci = chunks[c][0]
        dmax_all = jnp.zeros((CH, 128), f32)
        denl_all = jnp.zeros((CH, 128), f32)
        for h in range(NH):
            dmax_all = jnp.where(lane == h, dmax[c, h], dmax_all)
            denl_all = jnp.where(lane == h, denl[c, h], denl_all)
        t_ref[ci, 0] = gt[c]
        t_ref[ci, 1] = b_all[c]
        t_ref[ci, 2] = btot_all[c]
        t_ref[ci, 3] = dmax_all
        t_ref[ci, 4] = denl_all
        t_ref[ci, 5] = mloc_all[c]

    xs, pw, qkd, kd = {}, {}, {}, {}
    for c, h in units:
        beta = gt[c][:, 8 + h:9 + h]
        decay = jnp.exp(jnp.where(tri, cs[c][:, 12 + h:13 + h] - mix_t[c][12 + h:13 + h, :], NEG))
        pw[c, h] = jnp.where(stri, beta * kk[c, h] * decay, 0.0)
        xs[c, h] = jnp.concatenate([beta * vg[c, h], (beta * expg_all[c][:, 12 + h:13 + h]) * kgf[c, h]], axis=1)
        qkd[c, h] = (qk[c, h] * decay).astype(bf16)
        kd[c, h] = (kgf[c, h] * expdiff_all[c][:, 12 + h:13 + h]).astype(bf16)

    for c, h in units:
        ci = chunks[c][0]
        numl_ref[ci, h] = _dot(sloc[c, h], va[c, h])
        kwb = kwl[c, h].astype(bf16)
        for i, sl in enumerate(seqs):
            kv_ref[ci, i * NH + h] = _dot_tn(kwb[sl], va[c, h][sl])
            nv_ref[ci, i, h:h + 1, :] = jnp.sum(kwl[c, h][sl], axis=0, keepdims=True)

    sub = min(SUB, blk)
    a_d = {u: jnp.where(k["same_sub"], pw[u], 0.0) for u in units}
    tinv = {u: k["eye"] - a_d[u] for u in units}
    pcur = {u: -a_d[u] for u in units}
    for it in range(1, sub.bit_length() - 1):
        pb = {u: pcur[u].astype(bf16) for u in units}
        pcur = {u: _dot(pb[u], pb[u]) for u in units}
        tinv = {u: tinv[u] + _dot(pcur[u].astype(bf16), tinv[u].astype(bf16)) for u in units}
    tb = {u: tinv[u].astype(bf16) for u in units}
    if sub == blk:
        xs = {u: _dot(tb[u], xs[u].astype(bf16)) for u in units}
    else:
        a_off = {u: (pw[u] - a_d[u]).astype(bf16) for u in units}
        nsub = CH // sub
        done = {u: [] for u in units}
        for j in range(nsub):
            rows = slice(j * sub, (j + 1) * sub)
            pad = lambda parts, n: parts + ([jnp.zeros((n, 2 * DH), f32)] if n else [])
            v = {u: xs[u][rows] for u in units}
            if j > 0:
                xcat = {u: jnp.concatenate(pad(done[u], CH - j * sub), axis=0).astype(bf16) for u in units}
                v = {u: v[u] - _dot(a_off[u][rows], xcat[u]) for u in units}
            vcat = {u: jnp.concatenate(pad(pad([], j * sub) + [v[u]], CH - (j + 1) * sub), axis=0).astype(bf16)
                    for u in units}
            for u in units:
                done[u].append(_dot(tb[u][rows], vcat[u]))
        xs = {u: jnp.concatenate(done[u], axis=0) for u in units}

    xb = {u: xs[u].astype(bf16) for u in units}
    qx = {u: _dot(qkd[u], xb[u]) for u in units}
    for c, h in units:
        ci = chunks[c][0]
        o2_ref[ci, h] = qx[c, h][:, :DH]
        qeff_ref[ci, h] = (expg_all[c][:, 12 + h:13 + h] * qg[c, h] - qx[c, h][:, DH:]).astype(bf16)
        for i, sl in enumerate(seqs):
            mm = _dot_tn(kd[c, h][sl], xb[c, h][sl])
            m2_ref[ci, i * NH + h] = mm[:, :DH]
            m1_ref[ci, i * NH + h] = mm[:, DH:].astype(bf16)


def _phase_b(blk, k, pr, par, st, sc, out, ci):
    nseq = CH // blk
    c_ref, n_ref, m_ref, s_ref = st
    t_ref, numl_ref, kv_ref, nv_ref, m1_ref, m2_ref, qeff_ref, o2_ref = sc
    ha_ref, hb_ref, rows = out
    seqs = [slice(i * blk, (i + 1) * blk) for i in range(nseq)]
    heads = range(NH)
    gt, b_all, btot_all = t_ref[ci, 0], t_ref[ci, 1], t_ref[ci, 2]
    dmax_all, denl_all, mloc_all = t_ref[ci, 3], t_ref[ci, 4], t_ref[ci, 5]

    qa = [pr(C_QA + h * DH, DH) for h in heads]
    qc, ms, oq = [], [], []
    for h in heads:
        qb = qa[h].astype(bf16)
        qc.append([_dot(qb[sl], c_ref[i, h].astype(bf16)) for i, sl in enumerate(seqs)])
    for h in heads:
        qe = qeff_ref[ci, h]
        ms_h, oq_h = [], []
        for i, sl in enumerate(seqs):
            sb = s_ref[i, h].astype(bf16)
            ms_h.append(_dot(m1_ref[ci, i * NH + h], sb))
            oq_h.append(_dot(qe[sl], sb))
        ms.append(ms_h)
        oq.append(oq_h)

    if nseq == 1:
        mprev_all = jnp.broadcast_to(m_ref[0:1, :], (CH, 128))
    else:
        mprev_all = _sel_dot(k["expand_m"], m_ref[...])
    inter = b_all + mprev_all
    mt = jnp.maximum(inter, dmax_all)
    scale_all = jnp.exp(dmax_all - mt)
    winter_all = jnp.exp(inter - mt)
    emt_all = jnp.exp(-mt)
    carry = btot_all + mprev_all
    mnew = jnp.maximum(carry, mloc_all)
    sc2_all = jnp.exp(mloc_all - mnew)
    wold_all = jnp.exp(carry - mnew)
    egt_all = jnp.exp(btot_all)
    if nseq == 1:
        m_ref[0:1, :] = mnew[CH - 1:CH, :]
    else:
        m_ref[...] = _sel_dot(k["rowsel_m"], mnew)

    cat = lambda parts: parts[0] if nseq == 1 else jnp.concatenate(parts, axis=0)
    qn = [cat([jnp.sum(qa[h][sl] * n_ref[i, h:h + 1, :], axis=1, keepdims=True) for i, sl in enumerate(seqs)])
          for h in heads]
    hraw = []
    for h in heads:
        scale, winter = scale_all[:, h:h + 1], winter_all[:, h:h + 1]
        num = scale * numl_ref[ci, h] + winter * cat(qc[h])
        den = scale * denl_all[:, h:h + 1] + winter * qn[h]
        hraw.append(num / jnp.maximum(jnp.abs(den), emt_all[:, h:h + 1]))
    o = [cat(oq[h]) + o2_ref[ci, h] for h in heads]

    for h in heads:
        for i in range(nseq):
            last = (i + 1) * blk - 1
            wold = wold_all[last:last + 1, h:h + 1]
            sc2 = sc2_all[last:last + 1, h:h + 1]
            c_ref[i, h] = wold * c_ref[i, h] + sc2 * kv_ref[ci, i * NH + h]
            n_ref[i, h:h + 1, :] = wold * n_ref[i, h:h + 1, :] + sc2 * nv_ref[ci, i, h:h + 1, :]
            s_ref[i, h] = (egt_all[last:last + 1, 8 + h:9 + h] * s_ref[i, h] - ms[h][i]) + m2_ref[ci, i * NH + h]

    for h in heads:
        ha_ref[rows, h * DH:(h + 1) * DH] = hraw[h]
        hb_ref[rows, h * DH:(h + 1) * DH] = o[h]


def _mixer_long_kernel(t, p_ref, c0_ref, n0_ref, m0_ref, s0_ref, gb_ref, al_ref,
                       ha_ref, hb_ref, c_ref, n_ref, m_ref, s_ref, *sc):
    g = pl.program_id(1)

    @pl.when(g == 0)
    def _():
        c_ref[...] = c0_ref[...]
        n_ref[...] = n0_ref[...]
        m_ref[...] = m0_ref[...]
        s_ref[...] = s0_ref[...]

    k = _chunk_consts(CH, CH)
    par = (gb_ref[...], al_ref[...])
    st = (c_ref, n_ref, _M0(m_ref), s_ref)

    def chunk_views(ci):
        rows = pl.ds(pl.multiple_of(ci * CH, CH), CH)
        return ci, (lambda c0, n: p_ref[rows, c0:c0 + n])

    def body_a(j, carry):
        _phase_a(CH, CH, k, [chunk_views(A_GROUP * j + c) for c in range(A_GROUP)], par, sc)
        return carry

    def body_b(ci, carry):
        rows = pl.ds(pl.multiple_of(ci * CH, CH), CH)
        _phase_b(CH, k, lambda c0, n: p_ref[rows, c0:c0 + n], par, st, sc, (ha_ref, hb_ref, rows), ci)
        return carry

    lax.fori_loop(0, t // (CH * A_GROUP), body_a, 0)
    lax.fori_loop(0, t // CH, body_b, 0)


class _M0:
    def __init__(self, ref):
        self.ref = ref

    def __getitem__(self, idx):
        return self.ref[0] if idx is Ellipsis else self.ref[(0,) + idx]

    def __setitem__(self, idx, val):
        if idx is Ellipsis:
            self.ref[0] = val
        else:
            self.ref[(0,) + idx] = val


def _mixer_long(p, c0, n0, m0, s0, gb, al, nb, seq, t):
    nt = seq // t
    row_spec = lambda w: pl.BlockSpec((t, w), lambda b, g: (b * nt + g, 0))
    const = lambda shape: pl.BlockSpec(shape, lambda b, g: (0,) * len(shape))
    perb = lambda shape: pl.BlockSpec((1,) + shape, lambda b, g: (b,) + (0,) * len(shape))
    return pl.pallas_call(
        functools.partial(_mixer_long_kernel, t),
        grid=(nb, nt),
        in_specs=[row_spec(N_MIX),
                  const((1, NH, DH, DH)), const((1, NH, DH)), const((1, 8, 128)), const((1, NH, DH, DH)),
                  const((1, 128)), const((1, 128))],
        out_specs=[row_spec(WB), row_spec(WB),
                   perb((NH, DH, DH)), perb((NH, DH)), perb((8, 128)), perb((NH, DH, DH))],
        out_shape=[jax.ShapeDtypeStruct((nb * seq, WB), f32), jax.ShapeDtypeStruct((nb * seq, WB), f32),
                   jax.ShapeDtypeStruct((nb, NH, DH, DH), f32), jax.ShapeDtypeStruct((nb, NH, DH), f32),
                   jax.ShapeDtypeStruct((nb, 8, 128), f32), jax.ShapeDtypeStruct((nb, NH, DH, DH), f32)],
        scratch_shapes=_mix_scratch(t // CH, 1),
        compiler_params=_cparams(("arbitrary", "arbitrary")),
        name="mixer_long",
    )(p, c0, n0, m0, s0, gb, al)


def _mixer_block_kernel(blk, valid, n_alias, first_of, p_ref, c0_ref, n0_ref, m0_ref, s0_ref, gb_ref, al_ref,
                        *rest):
    ha_ref, hb_ref, c_ref, n_ref, m_ref, s_ref = rest[n_alias:n_alias + 6]
    sc = rest[n_alias + 6:]
    if first_of is not None:
        layer, depth = first_of
        for other in range(depth):
            if other != layer:
                c_ref[other] = jnp.zeros(c_ref.shape[1:], f32)
                s_ref[other] = jnp.zeros(s_ref.shape[1:], f32)
        c_ref, s_ref = c_ref.at[layer], s_ref.at[layer]
    c_ref[...] = c0_ref[...]
    n_ref[...] = n0_ref[...]
    m_ref[...] = m0_ref[...]
    s_ref[...] = s0_ref[...]
    k = _chunk_consts(blk, valid)
    par = (gb_ref[...], al_ref[...])
    pr = lambda c0, n: p_ref[:, c0:c0 + n]
    _phase_a(blk, valid, k, [(0, pr)], par, sc)
    _phase_b(blk, k, pr, par, (c_ref, n_ref, _M0(m_ref), s_ref), sc, (ha_ref, hb_ref, slice(0, CH)), 0)


def _mixer_block(p, c0, n0, m0, s0, gb, al, blk, valid, layer=None, prev=None):
    ng = p.shape[0] // CH
    nseq = CH // blk
    row_spec = lambda w: pl.BlockSpec((CH, w), lambda g: (g, 0))
    const = lambda shape: pl.BlockSpec(shape, lambda g: (0,) * len(shape))
    perg = lambda shape: pl.BlockSpec(shape, lambda g: (g,) + (0,) * (len(shape) - 1))
    big_shape = (ng * nseq, NH, DH, DH)
    first_of = None
    if layer is None:
        big = big_out = perg((nseq, NH, DH, DH))
    else:
        depth = c0.shape[0]
        big = big_out = pl.BlockSpec((None, nseq, NH, DH, DH), lambda g: (layer, g, 0, 0, 0))
        big_shape = (depth,) + big_shape
        if prev is None:
            first_of = (layer, depth)
            big_out = pl.BlockSpec((depth, nseq, NH, DH, DH), lambda g: (0, g, 0, 0, 0))
    aliased = () if prev is None else tuple(prev)
    n_in = 7
    return pl.pallas_call(
        functools.partial(_mixer_block_kernel, blk, valid, len(aliased), first_of),
        grid=(ng,),
        in_specs=[row_spec(N_MIX), big, perg((nseq, NH, DH)), perg((1, 8, 128)), big,
                  const((1, 128)), const((1, 128))]
                 + [pl.BlockSpec(memory_space=pl.ANY)] * len(aliased),
        out_specs=[row_spec(WB), row_spec(WB), big_out, perg((nseq, NH, DH)), perg((1, 8, 128)), big_out],
        out_shape=[jax.ShapeDtypeStruct((ng * CH, WB), f32), jax.ShapeDtypeStruct((ng * CH, WB), f32),
                   jax.ShapeDtypeStruct(big_shape, f32), jax.ShapeDtypeStruct((ng * nseq, NH, DH), f32),
                   jax.ShapeDtypeStruct((ng, 8, 128), f32), jax.ShapeDtypeStruct(big_shape, f32)],
        input_output_aliases={n_in: 2, n_in + 1: 5} if aliased else {},
        scratch_shapes=_mix_scratch(1, nseq),
        compiler_params=_cparams(("arbitrary",)),
        name="mixer_block",
    )(p, c0, n0, m0, s0, gb, al, *aliased)


def _outproj_kernel(x_ref, ha_ref, hb_ref, po_ref, pz_ref, anw_ref, bnw_ref,
                    wm_ref, wpa_ref, wpb_ref, wo_ref, g_ref, b_ref, o_ref):
    x = x_ref[...]
    mg = _sigmoid(_dot(x.astype(bf16), wm_ref[...]))
    cols = [slice(h * DH, (h + 1) * DH) for h in range(NH)]
    hr = [ha_ref[:, c] for c in cols]
    ob = [hb_ref[:, c] for c in cols]
    mu = [jnp.mean(v, axis=1, keepdims=True) for v in hr]
    osq = [jnp.mean(v * v, axis=1, keepdims=True) for v in ob]
    hc = [hr[h] - mu[h] for h in range(NH)]
    var = [jnp.mean(v * v, axis=1, keepdims=True) for v in hc]
    ha = [_sigmoid(po_ref[:, cols[h]]) * (hc[h] * lax.rsqrt(var[h] + NORM_EPS) * anw_ref[:, cols[h]])
          for h in range(NH)]
    hb = [(ob[h] * lax.rsqrt(osq[h] + NORM_EPS) * bnw_ref[...]) * _silu(pz_ref[:, cols[h]]) for h in range(NH)]
    ya = _dot(jnp.concatenate(ha, axis=1).astype(bf16), wpa_ref[...])
    yb = _dot(jnp.concatenate(hb, axis=1).astype(bf16), wpb_ref[...])
    y = mg[:, :D_MODEL] * ya + mg[:, D_MODEL:] * yb
    mix = _dot(y.astype(bf16), wo_ref[...])
    o_ref[...] = _ln_rows(ALPHA * x + mix, g_ref[...], b_ref[...])


def _outproj(x, ha, hb, p, anw, bnw, w_merge, layer, w_pa, w_pb, w_out, g, b):
    m = x.shape[0]
    tm = min(m, PROJ_TM)
    row_spec = lambda w: pl.BlockSpec((tm, w), lambda i: (i, 0))
    pcol = lambda c0: pl.BlockSpec((tm, WB), lambda i: (i, c0 // WB))
    const = lambda shape: pl.BlockSpec(shape, lambda i: (0, 0))
    return pl.pallas_call(
        _outproj_kernel,
        grid=(m // tm,),
        in_specs=[row_spec(D_MODEL), row_spec(WB), row_spec(WB), pcol(C_OA), pcol(C_Z),
                  const((1, WB)), const((1, DH)),
                  pl.BlockSpec((None, D_MODEL, 2 * D_MODEL), lambda i: (layer, 0, 0)),
                  const((WB, D_MODEL)), const((WB, D_MODEL)),
                  const((D_MODEL, D_MODEL)), const((1, D_MODEL)), const((1, D_MODEL))],
        out_specs=row_spec(D_MODEL),
        out_shape=jax.ShapeDtypeStruct((m, D_MODEL), f32),
        compiler_params=_cparams(("arbitrary",)),
        name="merge_outproj_ln",
    )(x, ha, hb, p, p, anw, bnw, w_merge, w_pa, w_pb, w_out, g, b)


def _ffn2_kernel(long_mode, tm, nmt, blk, valid, *refs):
    if long_mode:
        (x_ref, wa_ref, wb_ref, cwa_ref, cwb_ref, wd_ref, g_ref, b_ref, ha_ref, hb_ref,
         o_ref, sa_ref, sb_ref, hh_ref, ca_ref, cb_ref) = refs

        @pl.when(pl.program_id(0) % nmt == 0)
        def _():
            ca_ref[...] = ha_ref[0]
            cb_ref[...] = hb_ref[0]
    else:
        (x_ref, wa_ref, wb_ref, cwa_ref, cwb_ref, wd_ref, g_ref, b_ref, ea_ref, eb_ref,
         o_ref, sa_ref, sb_ref, hh_ref) = refs
        row = lax.broadcasted_iota(jnp.int32, (tm, FT), 0)
        vm = (row & (blk - 1)) >= (blk - valid)
    xb = x_ref[...].astype(bf16)
    for cb in range(NF):
        cols = slice(cb * FT, (cb + 1) * FT)
        ua = _dot(xb, wa_ref[:, cols])
        ub = _dot(xb, wb_ref[:, cols])
        if long_mode:
            h8a, h8b = ca_ref[:, cols], cb_ref[:, cols]
            ta, tb = ua[tm - HDR:tm], ub[tm - HDR:tm]
            ca_ref[:, cols] = ta
            cb_ref[:, cols] = tb
            sa_ref[0, :, cols] = ta
            sb_ref[0, :, cols] = tb
        else:
            h8a = h8b = None
            ua = jnp.where(vm, ua, ea_ref[:, cols])
            ub = jnp.where(vm, ub, eb_ref[:, cols])
            sa_ref[:, cols] = ua
            sb_ref[:, cols] = ub
        hh = _silu(_conv_rows(ua, h8a, cwa_ref[:, cols])) * _conv_rows(ub, h8b, cwb_ref[:, cols])
        hh_ref[:, cols] = hh.astype(bf16)
    out = _dot(hh_ref[...], wd_ref[...])
    o_ref[...] = _ln_rows(ALPHA * x_ref[...] + out, g_ref[...], b_ref[...])


def _ffn2(x, w_up_a, w_up_b, cw_a, cw_b, w_down, g, b, hist_a=None, hist_b=None,
          e_ab=None, nb=1, blk=CH, valid=CH):
    m = x.shape[0]
    long_mode = hist_a is not None
    seq = m // nb
    tm = min(seq, FFN_TM if long_mode else FFN_TM_BLOCK)
    nmt = seq // tm
    rows = lambda w: pl.BlockSpec((tm, w), lambda i: (i, 0))
    const = lambda shape: pl.BlockSpec(shape, lambda i: (0,) * len(shape))
    in_specs = [rows(D_MODEL), const((D_MODEL, D_FF)), const((D_MODEL, D_FF)), const((CONV_F, D_FF)),
                const((CONV_F, D_FF)), const((D_FF, D_MODEL)), const((1, D_MODEL)), const((1, D_MODEL))]
    scratch = [pltpu.VMEM((tm, D_FF), bf16)]
    if long_mode:
        in_specs += [const((1, HDR, D_FF))] * 2
        st_spec = pl.BlockSpec((1, HDR, D_FF), lambda i: (i, 0, 0))
        st_shape = jax.ShapeDtypeStruct((m // tm, HDR, D_FF), f32)
        scratch += [pltpu.VMEM((HDR, D_FF), f32), pltpu.VMEM((HDR, D_FF), f32)]
        extra = (hist_a, hist_b)
    else:
        in_specs += [pl.BlockSpec((tm, D_FF), lambda i: (i, 0)), pl.BlockSpec((tm, D_FF), lambda i: (i, 1))]
        st_spec = rows(D_FF)
        st_shape = jax.ShapeDtypeStruct((m, D_FF), f32)
        extra = (e_ab, e_ab)
    return pl.pallas_call(
        functools.partial(_ffn2_kernel, long_mode, tm, nmt, blk, valid),
        grid=(m // tm,),
        in_specs=in_specs,
        out_specs=[rows(D_MODEL), st_spec, st_spec],
        out_shape=[jax.ShapeDtypeStruct((m, D_MODEL), f32), st_shape, st_shape],
        scratch_shapes=scratch,
        compiler_params=_cparams(("arbitrary",)),
        name="conv_ffn_ln",
    )(x, w_up_a, w_up_b, cw_a, cw_b, w_down, g, b, *extra)


def _mix_weights(w_in):
    w = w_in.astype(bf16)
    zc = jnp.zeros(w.shape[:2] + (128 - 4 * NH,), bf16)
    w_mix = jnp.concatenate([w[..., A_Q:A_I], w[..., B_Z:B_BETA], w[..., A_I:B_QKV], w[..., B_BETA:G_MERGE], zc,
                             w[..., B_QKV:B_Z]], axis=-1)
    return w_mix, w[..., G_MERGE:]


def _layer_weights(l, w_mix, w_merge, mlstm_gate_bias, mlstm_norm_w, gdn_conv_w, gdn_A_log, gdn_dt_bias, gdn_norm_w,
                   w_branch_a, w_branch_b, w_out, ln1_g, ln1_b, w_up, ffn_conv_w, w_down, ln2_g, ln2_b):
    z4 = jnp.zeros((NH,), f32)
    gb = jnp.concatenate([mlstm_gate_bias[l], z4, gdn_dt_bias[l], jnp.zeros((128 - 4 * NH,), f32)])[None]
    al = jnp.concatenate([z4, z4, z4, gdn_A_log[l], jnp.zeros((128 - 4 * NH,), f32)])[None]
    return dict(
        layer=l, w_mix=w_mix, w_merge=w_merge, gb=gb, al=al,
        anw=mlstm_norm_w[l][None], bnw=gdn_norm_w[l][None], cw=gdn_conv_w[l],
        w_pa=w_branch_a[l].astype(bf16), w_pb=w_branch_b[l].astype(bf16), w_out=w_out[l].astype(bf16),
        ln1_g=ln1_g[l][None], ln1_b=ln1_b[l][None],
        w_up_a=w_up[l][:, :D_FF].astype(bf16), w_up_b=w_up[l][:, D_FF:].astype(bf16),
        cw_a=ffn_conv_w[l][:, :D_FF], cw_b=ffn_conv_w[l][:, D_FF:],
        w_down=w_down[l].astype(bf16), ln2_g=ln2_g[l][None], ln2_b=ln2_b[l][None])


def _block_layer(x, lw, st, blk, valid, layer=None, prev=None):
    c0, n0, m0, s0, gbuf, fbuf = st
    nseq_tot = n0.shape[0]
    nseq = CH // blk
    ng = nseq_tot // nseq
    front = blk - valid
    m0p = jnp.pad(m0.reshape(ng, nseq, NH), ((0, 0), (0, 8 - nseq), (0, 128 - NH)))
    e = jnp.pad(gbuf, ((0, 0), (front - (CONV_B - 1), valid), (0, 0))).reshape(nseq_tot * blk, 3 * WB)
    ef = jnp.pad(fbuf, ((0, 0), (front - (CONV_F - 1), valid), (0, 0))).reshape(nseq_tot * blk, 2 * D_FF)
    p, ext = _inproj(x, lw["w_mix"], lw["layer"], lw["cw"], e=e, blk=blk, valid=valid)
    ha, hb, c, n, m, s = _mixer_block(p, c0, n0, m0p, s0, lw["gb"], lw["al"], blk, valid, layer, prev)
    x1 = _outproj(x, ha, hb, p, lw["anw"], lw["bnw"], lw["w_merge"], lw["layer"], lw["w_pa"], lw["w_pb"],
                  lw["w_out"],
                  lw["ln1_g"], lw["ln1_b"])
    x2, ua, ub = _ffn2(x1, lw["w_up_a"], lw["w_up_b"], lw["cw_a"], lw["cw_b"], lw["w_down"],
                       lw["ln2_g"], lw["ln2_b"], e_ab=ef, blk=blk, valid=valid)
    m_new = m[:, :nseq, :NH].reshape(nseq_tot, NH)
    tail = lambda a, w, k: a.reshape(nseq_tot, blk, w)[:, blk - k:]
    gconv = tail(ext, 3 * WB, CONV_B - 1)
    fconv = jnp.concatenate([tail(ua, D_FF, CONV_F - 1), tail(ub, D_FF, CONV_F - 1)], axis=2)
    return x2, (c, n, m_new, s, gconv, fconv)


def _long_layer(x, lw, st, nb, seq):
    c0, n0, m0, s0, gbuf, fbuf = st
    m0p = jnp.pad(m0.reshape(1, 1, NH), ((0, 0), (0, 7), (0, 128 - NH)))
    hist = jnp.pad(gbuf, ((0, 0), (HDR - (CONV_B - 1), 0), (0, 0)))
    hf = jnp.pad(fbuf, ((0, 0), (HDR - (CONV_F - 1), 0), (0, 0)))
    p, tails = _inproj(x, lw["w_mix"], lw["layer"], lw["cw"], hist=hist, nb=nb)
    ha, hb, c, n, m, s = _mixer_long(p, c0, n0, m0p, s0, lw["gb"], lw["al"], nb, seq, MIX_T)
    x1 = _outproj(x, ha, hb, p, lw["anw"], lw["bnw"], lw["w_merge"], lw["layer"], lw["w_pa"], lw["w_pb"],
                  lw["w_out"],
                  lw["ln1_g"], lw["ln1_b"])
    x2, sa, sb = _ffn2(x1, lw["w_up_a"], lw["w_up_b"], lw["cw_a"], lw["cw_b"], lw["w_down"],
                       lw["ln2_g"], lw["ln2_b"], hist_a=hf[:, :, :D_FF], hist_b=hf[:, :, D_FF:], nb=nb)
    last_tile = lambda a: a.reshape((nb, -1) + a.shape[1:])[:, -1]
    gconv = last_tile(tails)[:, HDR - (CONV_B - 1):]
    fconv = jnp.concatenate([last_tile(sa), last_tile(sb)], axis=2)[:, HDR - (CONV_F - 1):]
    return x2, (c, n, m[:, 0, :NH], s, gconv, fconv)


def kernel(x_prompt, x_sample, state_mlstm_C, state_mlstm_n, state_mlstm_m, state_gdn_S, state_gdn_conv, state_ffn_conv, meta_tokens, ln_emb_g, ln_emb_b, w_in, mlstm_gate_bias, mlstm_norm_w, gdn_conv_w, gdn_A_log, gdn_dt_bias, gdn_norm_w, w_branch_a, w_branch_b, w_out, ln1_g, ln1_b, w_up, ffn_conv_w, w_down, ln2_g, ln2_b):
    nb, seq, _ = x_prompt.shape
    ns, ls, _ = x_sample.shape
    sblk = 8
    assert seq % max(MIX_T, PROJ_TM, FFN_TM) == 0 and ls + CONV_B - 1 <= sblk and ns % (CH // sblk) == 0 and N_META + CONV_B - 1 <= CH
    w_mix, w_merge = _mix_weights(w_in)
    lws = [_layer_weights(l, w_mix, w_merge, mlstm_gate_bias, mlstm_norm_w, gdn_conv_w, gdn_A_log, gdn_dt_bias,
                          gdn_norm_w, w_branch_a, w_branch_b, w_out, ln1_g, ln1_b, w_up, ffn_conv_w,
                          w_down, ln2_g, ln2_b) for l in range(DEPTH)]
    eg, eb = ln_emb_g[None], ln_emb_b[None]

    xm = _layer_norm(jnp.pad(meta_tokens, ((CH - N_META, 0), (0, 0))), eg, eb)
    xs = _layer_norm(jnp.pad(x_sample, ((0, 0), (sblk - ls, 0), (0, 0))).reshape(ns * sblk, D_MODEL), eg, eb)
    xp = _layer_norm(x_prompt.reshape(nb * seq, D_MODEL), eg, eb)

    zero_st = (jnp.zeros((1, NH, DH, DH), f32), jnp.zeros((1, NH, DH), f32), jnp.zeros((1, NH), f32),
               jnp.zeros((1, NH, DH, DH), f32), jnp.zeros((1, CONV_B - 1, 3 * WB), f32),
               jnp.zeros((1, CONV_F - 1, 2 * D_FF), f32))
    p_states, s_states = [], []
    big = None
    for l in range(DEPTH):
        xm, st_m = _block_layer(xm, lws[l], zero_st, CH, N_META)
        xp, st_p = _long_layer(xp, lws[l], st_m, nb, seq)
        samp_st = (state_mlstm_C, state_mlstm_n[l], state_mlstm_m[l], state_gdn_S,
                   state_gdn_conv[l], state_ffn_conv[l])
        xs, st_s = _block_layer(xs, lws[l], samp_st, sblk, ls, layer=l, prev=big)
        big = (st_s[0], st_s[3])
        p_states.append(st_p)
        s_states.append(st_s)

    stack = lambda states, i: jnp.stack([s[i] for s in states], axis=0)
    y_prompt = xp.reshape(nb, seq, D_MODEL)
    y_sample = xs.reshape(ns, sblk, D_MODEL)[:, sblk - ls:]
    return (y_prompt, y_sample,
            stack(p_states, 0), stack(p_states, 1), stack(p_states, 2), stack(p_states, 3),
            stack(p_states, 4), stack(p_states, 5),
            big[0], stack(s_states, 1), stack(s_states, 2), big[1],
            stack(s_states, 4), stack(s_states, 5))
```

```python
import functools

import jax
import jax.numpy as jnp
from jax import lax
from jax.experimental import pallas as pl
from jax.experimental.pallas import tpu as pltpu

f32 = jnp.float32
bf16 = jnp.bfloat16

D_MODEL = 1024
N_META = 16
NH = 4
DH = 128
WB = NH * DH
CONV_B = 4
D_FF = 2816
CONV_F = 3
DEPTH = 2
ALPHA = (2 * DEPTH) ** 0.25
LN_EPS = 1e-5
NORM_EPS = 1e-6
QSCALE = DH ** -0.5

A_Q = 0
A_I = 4 * WB
B_QKV = A_I + 2 * NH
B_Z = B_QKV + 3 * WB
B_BETA = B_Z + WB
G_MERGE = B_BETA + 2 * NH

C_QA, C_KA, C_VA, C_OA = 0, WB, 2 * WB, 3 * WB
C_Z = 4 * WB
C_G = 5 * WB
C_QKV = C_G + 128
N_MIX = C_QKV + 3 * WB

CH = 64
SUB = 16
A_GROUP = 8
HDR = 8
NEG = -1e30
FT = 256
NF = D_FF // FT
FFN_TM_BLOCK = 256
FFN_TM = 512
MIX_T = 512
PROJ_TM = 512
VMEM_LIMIT = 56 * 1024 * 1024


def _cparams(sem):
    return pltpu.CompilerParams(dimension_semantics=sem, vmem_limit_bytes=VMEM_LIMIT)


def _dot(a, b):
    return jnp.dot(a, b, preferred_element_type=f32)


def _dot_nt(a, b):
    return lax.dot_general(a, b, (((1,), (1,)), ((), ())), preferred_element_type=f32)


def _dot_tn(a, b):
    return lax.dot_general(a, b, (((0,), (0,)), ((), ())), preferred_element_type=f32)


def _sel_dot(sel, x):
    hi = x.astype(bf16)
    r1 = x - hi.astype(f32)
    mid = r1.astype(bf16)
    lo = (r1 - mid.astype(f32)).astype(bf16)
    return (_dot(sel, hi) + _dot(sel, mid)) + _dot(sel, lo)


def _softplus(x):
    return jnp.maximum(x, 0.0) + jnp.log1p(jnp.exp(-jnp.abs(x)))


def _sigmoid(x):
    return 1.0 / (1.0 + jnp.exp(-x))


def _silu(x):
    return x * _sigmoid(x)


def _ln_rows(x, g, b):
    mu = jnp.mean(x, axis=-1, keepdims=True)
    xc = x - mu
    var = jnp.mean(xc * xc, axis=-1, keepdims=True)
    return xc * lax.rsqrt(var + LN_EPS) * g + b


def _conv_rows(u, h8, cw):
    width = cw.shape[0]
    row = lax.broadcasted_iota(jnp.int32, (HDR, u.shape[1]), 0) if h8 is not None else None
    acc = None
    for j in range(width):
        sh = width - 1 - j
        term = u if sh == 0 else pltpu.roll(u, sh, 0)
        if h8 is not None and sh > 0:
            head = term[0:HDR]
            for r in range(sh):
                head = jnp.where(row == r, h8[HDR - sh + r:HDR - sh + r + 1], head)
            term = jnp.concatenate([head, term[HDR:]], axis=0)
        term = term * cw[j:j + 1]
        acc = term if acc is None else acc + term
    return acc


def _inproj_kernel(long_mode, with_ln, tm, nmt, blk, valid, *refs):
    if with_ln:
        g_ref, b_ref, xln_ref = refs[4], refs[5], refs[6]
        refs = refs[:4] + refs[7:]
    if long_mode:
        x_ref, w_ref, cw_ref, hist_ref, o_ref, tail_ref, carry_ref = refs

        @pl.when(pl.program_id(0) % nmt == 0)
        def _():
            carry_ref[...] = hist_ref[0]
    else:
        x_ref, w_ref, cw_ref, e_ref, o_ref, tail_ref = refs
        row = lax.broadcasted_iota(jnp.int32, (tm, 2 * DH), 0)
        vm = (row & (blk - 1)) >= (blk - valid)
    if with_ln:
        xln = _ln_rows(x_ref[...], g_ref[...], b_ref[...])
        xln_ref[...] = xln
        xb = xln.astype(bf16)
    else:
        xb = x_ref[...].astype(bf16)
    nstep = 3 * NH // 2
    cuts = [C_QA + WB] + [C_KA + (C_QKV - C_KA) * (i + 1) // (nstep - 1) // 128 * 128 for i in range(nstep - 1)]
    cuts[-1] = C_QKV
    for cb in range(nstep):
        cols = slice(cb * 2 * DH, (cb + 1) * 2 * DH)
        u = _dot(xb, w_ref[:, C_QKV + cb * 2 * DH:C_QKV + (cb + 1) * 2 * DH])
        lo = C_QA if cb == 0 else cuts[cb - 1]
        plain = _dot(xb, w_ref[:, lo:cuts[cb]])
        o_ref[:, lo:cuts[cb]] = plain * QSCALE if cb == 0 else plain
        if long_mode:
            h8 = carry_ref[:, cols]
            t8 = u[tm - HDR:tm]
            carry_ref[:, cols] = t8
            tail_ref[0, :, cols] = t8
        else:
            h8 = None
            u = jnp.where(vm, u, e_ref[:, cols])
            tail_ref[:, cols] = u
        c = _silu(_conv_rows(u, h8, cw_ref[:, cols]))
        for half in range(2):
            ch = c[:, half * DH:(half + 1) * DH]
            if cb < NH:
                ch = ch * lax.rsqrt(jnp.sum(ch * ch, axis=1, keepdims=True) + NORM_EPS)
            if cb < NH // 2:
                ch = ch * QSCALE
            c0 = C_QKV + (2 * cb + half) * DH
            o_ref[:, c0:c0 + DH] = ch


def _inproj(x, w_mix, layer, cw, hist=None, e=None, nb=1, blk=CH, valid=CH, ln=None):
    m = x.shape[0]
    long_mode = hist is not None
    seq = m // nb
    tm = min(seq, PROJ_TM)
    nmt = seq // tm
    rows = lambda w: pl.BlockSpec((tm, w), lambda i: (i, 0))
    const = lambda shape: pl.BlockSpec(shape, lambda i: (0,) * len(shape))
    in_specs = [rows(D_MODEL), pl.BlockSpec((None, D_MODEL, N_MIX), lambda i: (layer, 0, 0)),
                const((CONV_B, 3 * WB))]
    if long_mode:
        in_specs += [const((1, HDR, 3 * WB))]
        tail_spec = pl.BlockSpec((1, HDR, 3 * WB), lambda i: (i, 0, 0))
        tail_shape = jax.ShapeDtypeStruct((m // tm, HDR, 3 * WB), f32)
        scratch = [pltpu.VMEM((HDR, 3 * WB), f32)]
        extra = hist
    else:
        in_specs += [rows(3 * WB)]
        tail_spec = rows(3 * WB)
        tail_shape = jax.ShapeDtypeStruct((m, 3 * WB), f32)
        scratch = []
        extra = e
    args = [x, w_mix, cw, extra]
    out_specs = [rows(N_MIX), tail_spec]
    out_shape = [jax.ShapeDtypeStruct((m, N_MIX), f32), tail_shape]
    if ln is not None:
        in_specs += [const((1, D_MODEL))] * 2
        args += list(ln)
        out_specs = [rows(D_MODEL)] + out_specs
        out_shape = [jax.ShapeDtypeStruct((m, D_MODEL), f32)] + out_shape
    res = pl.pallas_call(
        functools.partial(_inproj_kernel, long_mode, ln is not None, tm, nmt, blk, valid),
        grid=(m // tm,),
        in_specs=in_specs,
        out_specs=out_specs,
        out_shape=out_shape,
        scratch_shapes=scratch,
        compiler_params=_cparams(("arbitrary",)),
        name="mixer_inproj",
    )(*args)
    return res if ln is None else (res[1], res[2], res[0])


def _chunk_consts(blk, valid):
    lg = blk.bit_length() - 1
    r = lax.broadcasted_iota(jnp.int32, (CH, CH), 0)
    c = lax.broadcasted_iota(jnp.int32, (CH, CH), 1)
    same = (r >> lg) == (c >> lg)
    tri = jnp.logical_and(same, c <= r)
    stri = jnp.logical_and(same, c < r)
    sel = lambda cond: jnp.where(cond, 1.0, 0.0).astype(bf16)
    tri_m = sel(tri)
    blk_m = sel(same)
    re = lax.broadcasted_iota(jnp.int32, (CH, 8), 0)
    ce = lax.broadcasted_iota(jnp.int32, (CH, 8), 1)
    expand_m = sel(ce == (re >> lg))
    rs = lax.broadcasted_iota(jnp.int32, (8, CH), 0)
    cs = lax.broadcasted_iota(jnp.int32, (8, CH), 1)
    rowsel_m = sel(cs == (rs << lg) + (blk - 1))
    rv = lax.broadcasted_iota(jnp.int32, (CH, 128), 0)
    valid_m = (rv & (blk - 1)) >= (blk - valid)
    lgs = min(SUB, blk).bit_length() - 1
    same_sub = (r >> lgs) == (c >> lgs)
    eye = jnp.where(r == c, 1.0, 0.0)
    return dict(tri=tri, stri=stri, tri_m=tri_m, blk_m=blk_m, same_sub=same_sub, eye=eye,
                expand_m=expand_m, rowsel_m=rowsel_m, valid_m=valid_m)


def _block_max(x, blk):
    if blk == CH:
        return jnp.broadcast_to(jnp.max(x, axis=0, keepdims=True), x.shape)
    x3 = x.reshape(CH // blk, blk, 128)
    return jnp.broadcast_to(jnp.max(x3, axis=1, keepdims=True), x3.shape).reshape(CH, 128)


def _mix_scratch(nc, nseq):
    return [pltpu.VMEM((nc, 6, CH, 128), f32),
            pltpu.VMEM((nc, NH, CH, DH), f32),
            pltpu.VMEM((nc, nseq * NH, DH, DH), f32),
            pltpu.VMEM((nc, nseq, 8, DH), f32),
            pltpu.VMEM((nc, nseq * NH, DH, DH), bf16),
            pltpu.VMEM((nc, nseq * NH, DH, DH), f32),
            pltpu.VMEM((nc, NH, CH, DH), bf16),
            pltpu.VMEM((nc, NH, CH, DH), f32)]


def _phase_a(blk, valid, k, chunks, par, sc):
    nseq = CH // blk
    gbias, alog = par[0], par[1]
    t_ref, numl_ref, kv_ref, nv_ref, m1_ref, m2_ref, qeff_ref, o2_ref = sc
    tri, stri = k["tri"], k["stri"]
    seqs = [slice(i * blk, (i + 1) * blk) for i in range(nseq)]
    nch = len(chunks)
    units = [(c, h) for c in range(nch) for h in range(NH)]
    lane = lax.broadcasted_iota(jnp.int32, (CH, 128), 1)
    cum_lane = jnp.logical_or(jnp.logical_and(lane >= 4, lane < 8), lane >= 12)

    gt = []
    for ci, pr in chunks:
        graw = pr(C_G, 128) + gbias
        g = jnp.where(lane < 4, graw,
                      jnp.where(lane < 8, -_softplus(-graw),
                                jnp.where(lane < 12, _sigmoid(graw),
                                          jnp.where(lane < 16, -jnp.exp(alog) * _softplus(graw), 0.0))))
        if valid < blk:
            g = jnp.where(k["valid_m"], g, jnp.where(lane < 4, NEG, 0.0))
        gt.append(g)
    cs = [_sel_dot(k["tri_m"], g) for g in gt]
    if nseq == 1:
        tot = [jnp.broadcast_to(x[CH - 1:CH, :], (CH, 128)) for x in cs]
    else:
        tot = [_sel_dot(k["blk_m"], g) for g in gt]
    mix_t = [jnp.transpose(jnp.where(cum_lane, cs[c], gt[c])) for c in range(nch)]
    b_all = [pltpu.roll(x, 124, 1) for x in cs]
    btot_all = [pltpu.roll(x, 124, 1) for x in tot]
    wlog = [btot_all[c] - b_all[c] + gt[c] for c in range(nch)]
    mloc_all = [_block_max(x, blk) for x in wlog]
    wsrc_all = [jnp.exp(wlog[c] - mloc_all[c]) for c in range(nch)]
    expg_all = [jnp.exp(x) for x in cs]
    expdiff_all = [jnp.exp(tot[c] - cs[c]) for c in range(nch)]

    qa, ka, va, qg, kgf, vg = {}, {}, {}, {}, {}, {}
    for c, h in units:
        _, pr = chunks[c]
        qa[c, h] = pr(C_QA + h * DH, DH).astype(bf16)
        ka[c, h] = pr(C_KA + h * DH, DH)
        va[c, h] = pr(C_VA + h * DH, DH).astype(bf16)
        qg[c, h] = pr(C_QKV + h * DH, DH)
        kgf[c, h] = pr(C_QKV + WB + h * DH, DH)
        vg[c, h] = pr(C_QKV + 2 * WB + h * DH, DH)
    qgb = {u: qg[u].astype(bf16) for u in units}
    kg = {u: kgf[u].astype(bf16) for u in units}

    sq = {u: _dot_nt(qa[u], ka[u].astype(bf16)) for u in units}
    kk = {u: _dot_nt(kg[u], kg[u]) for u in units}
    qk = {u: _dot_nt(qgb[u], kg[u]) for u in units}

    dm = {(c, h): jnp.where(tri, cs[c][:, 4 + h:5 + h] - mix_t[c][4 + h:5 + h, :] + mix_t[c][h:h + 1, :], NEG)
          for c, h in units}
    dmax = {u: jnp.max(dm[u], axis=1, keepdims=True) for u in units}
    s = {u: sq[u] * jnp.exp(dm[u] - dmax[u]) for u in units}
    denl = {u: jnp.sum(s[u], axis=1, keepdims=True) for u in units}
    sloc = {u: s[u].astype(bf16) for u in units}
    kwl = {(c, h): ka[c, h] * wsrc_all[c][:, h:h + 1] for c, h in units}
    for c in range(nch):
        ci = chunks[c][0]
        dmax_all = jnp.zeros((CH, 128), f32)
        denl_all = jnp.zeros((CH, 128), f32)
        for h in range(NH):
            dmax_all = jnp.where(lane == h, dmax[c, h], dmax_all)
            denl_all = jnp.where(lane == h, denl[c, h], denl_all)
        t_ref[ci, 0] = gt[c]
        t_ref[ci, 1] = b_all[c]
        t_ref[ci, 2] = btot_all[c]
        t_ref[ci, 3] = dmax_all
        t_ref[ci, 4] = denl_all
        t_ref[ci, 5] = mloc_all[c]

    xs, pw, qkd, kd = {}, {}, {}, {}
    for c, h in units:
        beta = gt[c][:, 8 + h:9 + h]
        decay = jnp.exp(jnp.where(tri, cs[c][:, 12 + h:13 + h] - mix_t[c][12 + h:13 + h, :], NEG))
        pw[c, h] = jnp.where(stri, beta * kk[c, h] * decay, 0.0)
        xs[c, h] = jnp.concatenate([beta * vg[c, h], (beta * expg_all[c][:, 12 + h:13 + h]) * kgf[c, h]], axis=1)
        qkd[c, h] = (qk[c, h] * decay).astype(bf16)
        kd[c, h] = (kgf[c, h] * expdiff_all[c][:, 12 + h:13 + h]).astype(bf16)

    for c, h in units:
        ci = chunks[c][0]
        numl_ref[ci, h] = _dot(sloc[c, h], va[c, h])
        kwb = kwl[c, h].astype(bf16)
        for i, sl in enumerate(seqs):
            kv_ref[ci, i * NH + h] = _dot_tn(kwb[sl], va[c, h][sl])
            nv_ref[ci, i, h:h + 1, :] = jnp.sum(kwl[c, h][sl], axis=0, keepdims=True)

    sub = min(SUB, blk)
    a_d = {u: jnp.where(k["same_sub"], pw[u], 0.0) for u in units}
    tinv = {u: k["eye"] - a_d[u] for u in units}
    pcur = {u: -a_d[u] for u in units}
    for it in range(1, sub.bit_length() - 1):
        pb = {u: pcur[u].astype(bf16) for u in units}
        pcur = {u: _dot(pb[u], pb[u]) for u in units}
        tinv = {u: tinv[u] + _dot(pcur[u].astype(bf16), tinv[u].astype(bf16)) for u in units}
    tb = {u: tinv[u].astype(bf16) for u in units}
    if sub == blk:
        xs = {u: _dot(tb[u], xs[u].astype(bf16)) for u in units}
    else:
        a_off = {u: (pw[u] - a_d[u]).astype(bf16) for u in units}
        nsub = CH // sub
        done = {u: [] for u in units}
        for j in range(nsub):
            rows = slice(j * sub, (j + 1) * sub)
            pad = lambda parts, n: parts + ([jnp.zeros((n, 2 * DH), f32)] if n else [])
            v = {u: xs[u][rows] for u in units}
            if j > 0:
                xcat = {u: jnp.concatenate(pad(done[u], CH - j * sub), axis=0).astype(bf16) for u in units}
                v = {u: v[u] - _dot(a_off[u][rows], xcat[u]) for u in units}
            vcat = {u: jnp.concatenate(pad(pad([], j * sub) + [v[u]], CH - (j + 1) * sub), axis=0).astype(bf16)
                    for u in units}
            for u in units:
                done[u].append(_dot(tb[u][rows], vcat[u]))
        xs = {u: jnp.concatenate(done[u], axis=0) for u in units}

    xb = {u: xs[u].astype(bf16) for u in units}
    qx = {u: _dot(qkd[u], xb[u]) for u in units}
    for c, h in units:
        ci = chunks[c][0]
        o2_ref[ci, h] = qx[c, h][:, :DH]
        qeff_ref[ci, h] = (expg_all[c][:, 12 + h:13 + h] * qg[c, h] - qx[c, h][:, DH:]).astype(bf16)
        for i, sl in enumerate(seqs):
            mm = _dot_tn(kd[c, h][sl], xb[c, h][sl])
            m2_ref[ci, i * NH + h] = mm[:, :DH]
            m1_ref[ci, i * NH + h] = mm[:, DH:].astype(bf16)


def _phase_b(blk, k, pr, par, st, sc, out, ci):
    nseq = CH // blk
    c_ref, n_ref, m_ref, s_ref = st
    t_ref, numl_ref, kv_ref, nv_ref, m1_ref, m2_ref, qeff_ref, o2_ref = sc
    ha_ref, hb_ref, rows = out
    seqs = [slice(i * blk, (i + 1) * blk) for i in range(nseq)]
    heads = range(NH)
    gt, b_all, btot_all = t_ref[ci, 0], t_ref[ci, 1], t_ref[ci, 2]
    dmax_all, denl_all, mloc_all = t_ref[ci, 3], t_ref[ci, 4], t_ref[ci, 5]

    qa = [pr(C_QA + h * DH, DH) for h in heads]
    qc, ms, oq = [], [], []
    for h in heads:
        qb = qa[h].astype(bf16)
        qc.append([_dot(qb[sl], c_ref[i, h].astype(bf16)) for i, sl in enumerate(seqs)])
    for h in heads:
        qe = qeff_ref[ci, h]
        ms_h, oq_h = [], []
        for i, sl in enumerate(seqs):
            sb = s_ref[i, h].astype(bf16)
            ms_h.append(_dot(m1_ref[ci, i * NH + h], sb))
            oq_h.append(_dot(qe[sl], sb))
        ms.append(ms_h)
        oq.append(oq_h)

    if nseq == 1:
        mprev_all = jnp.broadcast_to(m_ref[0:1, :], (CH, 128))
    else:
        mprev_all = _sel_dot(k["expand_m"], m_ref[...])
    inter = b_all + mprev_all
    mt = jnp.maximum(inter, dmax_all)
    scale_all = jnp.exp(dmax_all - mt)
    winter_all = jnp.exp(inter - mt)
    emt_all = jnp.exp(-mt)
    carry = btot_all + mprev_all
    mnew = jnp.maximum(carry, mloc_all)
    sc2_all = jnp.exp(mloc_all - mnew)
    wold_all = jnp.exp(carry - mnew)
    egt_all = jnp.exp(btot_all)
    if nseq == 1:
        m_ref[0:1, :] = mnew[CH - 1:CH, :]
    else:
        m_ref[...] = _sel_dot(k["rowsel_m"], mnew)

    cat = lambda parts: parts[0] if nseq == 1 else jnp.concatenate(parts, axis=0)
    qn = [cat([jnp.sum(qa[h][sl] * n_ref[i, h:h + 1, :], axis=1, keepdims=True) for i, sl in enumerate(seqs)])
          for h in heads]
    hraw = []
    for h in heads:
        scale, winter = scale_all[:, h:h + 1], winter_all[:, h:h + 1]
        num = scale * numl_ref[ci, h] + winter * cat(qc[h])
        den = scale * denl_all[:, h:h + 1] + winter * qn[h]
        hraw.append(num / jnp.maximum(jnp.abs(den), emt_all[:, h:h + 1]))
    o = [cat(oq[h]) + o2_ref[ci, h] for h in heads]

    for h in heads:
        for i in range(nseq):
            last = (i + 1) * blk - 1
            wold = wold_all[last:last + 1, h:h + 1]
            sc2 = sc2_all[last:last + 1, h:h + 1]
            c_ref[i, h] = wold * c_ref[i, h] + sc2 * kv_ref[ci, i * NH + h]
            n_ref[i, h:h + 1, :] = wold * n_ref[i, h:h + 1, :] + sc2 * nv_ref[ci, i, h:h + 1, :]
            s_ref[i, h] = (egt_all[last:last + 1, 8 + h:9 + h] * s_ref[i, h] - ms[h][i]) + m2_ref[ci, i * NH + h]

    for h in heads:
        ha_ref[rows, h * DH:(h + 1) * DH] = hraw[h]
        hb_ref[rows, h * DH:(h + 1) * DH] = o[h]


def _mixer_long_kernel(t, p_ref, c0_ref, n0_ref, m0_ref, s0_ref, gb_ref, al_ref,
                       ha_ref, hb_ref, c_ref, n_ref, m_ref, s_ref, *sc):
    g = pl.program_id(1)

    @pl.when(g == 0)
    def _():
        c_ref[...] = c0_ref[...]
        n_ref[...] = n0_ref[...]
        m_ref[...] = m0_ref[...]
        s_ref[...] = s0_ref[...]

    k = _chunk_consts(CH, CH)
    par = (gb_ref[...], al_ref[...])
    st = (c_ref, n_ref, _M0(m_ref), s_ref)

    def chunk_views(ci):
        rows = pl.ds(pl.multiple_of(ci * CH, CH), CH)
        return ci, (lambda c0, n: p_ref[rows, c0:c0 + n])

    def body_a(j, carry):
        _phase_a(CH, CH, k, [chunk_views(A_GROUP * j + c) for c in range(A_GROUP)], par, sc)
        return carry

    def body_b(ci, carry):
        rows = pl.ds(pl.multiple_of(ci * CH, CH), CH)
        _phase_b(CH, k, lambda c0, n: p_ref[rows, c0:c0 + n], par, st, sc, (ha_ref, hb_ref, rows), ci)
        return carry

    lax.fori_loop(0, t // (CH * A_GROUP), body_a, 0)
    lax.fori_loop(0, t // CH, body_b, 0)


class _M0:
    def __init__(self, ref):
        self.ref = ref

    def __getitem__(self, idx):
        return self.ref[0] if idx is Ellipsis else self.ref[(0,) + idx]

    def __setitem__(self, idx, val):
        if idx is Ellipsis:
            self.ref[0] = val
        else:
            self.ref[(0,) + idx] = val


def _mixer_long(p, c0, n0, m0, s0, gb, al, nb, seq, t):
    nt = seq // t
    row_spec = lambda w: pl.BlockSpec((t, w), lambda b, g: (b * nt + g, 0))
    const = lambda shape: pl.BlockSpec(shape, lambda b, g: (0,) * len(shape))
    perb = lambda shape: pl.BlockSpec((1,) + shape, lambda b, g: (b,) + (0,) * len(shape))
    return pl.pallas_call(
        functools.partial(_mixer_long_kernel, t),
        grid=(nb, nt),
        in_specs=[row_spec(N_MIX),
                  const((1, NH, DH, DH)), const((1, NH, DH)), const((1, 8, 128)), const((1, NH, DH, DH)),
                  const((1, 128)), const((1, 128))],
        out_specs=[row_spec(WB), row_spec(WB),
                   perb((NH, DH, DH)), perb((NH, DH)), perb((8, 128)), perb((NH, DH, DH))],
        out_shape=[jax.ShapeDtypeStruct((nb * seq, WB), f32), jax.ShapeDtypeStruct((nb * seq, WB), f32),
                   jax.ShapeDtypeStruct((nb, NH, DH, DH), f32), jax.ShapeDtypeStruct((nb, NH, DH), f32),
                   jax.ShapeDtypeStruct((nb, 8, 128), f32), jax.ShapeDtypeStruct((nb, NH, DH, DH), f32)],
        scratch_shapes=_mix_scratch(t // CH, 1),
        compiler_params=_cparams(("arbitrary", "arbitrary")),
        name="mixer_long",
    )(p, c0, n0, m0, s0, gb, al)


def _mixer_block_kernel(blk, valid, n_alias, first_of, p_ref, c0_ref, n0_ref, m0_ref, s0_ref, gb_ref, al_ref,
                        *rest):
    ha_ref, hb_ref, c_ref, n_ref, m_ref, s_ref = rest[n_alias:n_alias + 6]
    sc = rest[n_alias + 6:]
    if first_of is not None:
        layer, depth = first_of
        for other in range(depth):
            if other != layer:
                c_ref[other] = jnp.zeros(c_ref.shape[1:], f32)
                s_ref[other] = jnp.zeros(s_ref.shape[1:], f32)
        c_ref, s_ref = c_ref.at[layer], s_ref.at[layer]
    c_ref[...] = c0_ref[...]
    n_ref[...] = n0_ref[...]
    m_ref[...] = m0_ref[...]
    s_ref[...] = s0_ref[...]
    k = _chunk_consts(blk, valid)
    par = (gb_ref[...], al_ref[...])
    pr = lambda c0, n: p_ref[:, c0:c0 + n]
    _phase_a(blk, valid, k, [(0, pr)], par, sc)
    _phase_b(blk, k, pr, par, (c_ref, n_ref, _M0(m_ref), s_ref), sc, (ha_ref, hb_ref, slice(0, CH)), 0)


def _mixer_block(p, c0, n0, m0, s0, gb, al, blk, valid, layer=None, prev=None):
    ng = p.shape[0] // CH
    nseq = CH // blk
    row_spec = lambda w: pl.BlockSpec((CH, w), lambda g: (g, 0))
    const = lambda shape: pl.BlockSpec(shape, lambda g: (0,) * len(shape))
    perg = lambda shape: pl.BlockSpec(shape, lambda g: (g,) + (0,) * (len(shape) - 1))
    big_shape = (ng * nseq, NH, DH, DH)
    first_of = None
    if layer is None:
        big = big_out = perg((nseq, NH, DH, DH))
    else:
        depth = c0.shape[0]
        big = big_out = pl.BlockSpec((None, nseq, NH, DH, DH), lambda g: (layer, g, 0, 0, 0))
        big_shape = (depth,) + big_shape
        if prev is None:
            first_of = (layer, depth)
            big_out = pl.BlockSpec((depth, nseq, NH, DH, DH), lambda g: (0, g, 0, 0, 0))
    aliased = () if prev is None else tuple(prev)
    n_in = 7
    return pl.pallas_call(
        functools.partial(_mixer_block_kernel, blk, valid, len(aliased), first_of),
        grid=(ng,),
        in_specs=[row_spec(N_MIX), big, perg((nseq, NH, DH)), perg((1, 8, 128)), big,
                  const((1, 128)), const((1, 128))]
                 + [pl.BlockSpec(memory_space=pl.ANY)] * len(aliased),
        out_specs=[row_spec(WB), row_spec(WB), big_out, perg((nseq, NH, DH)), perg((1, 8, 128)), big_out],
        out_shape=[jax.ShapeDtypeStruct((ng * CH, WB), f32), jax.ShapeDtypeStruct((ng * CH, WB), f32),
                   jax.ShapeDtypeStruct(big_shape, f32), jax.ShapeDtypeStruct((ng * nseq, NH, DH), f32),
                   jax.ShapeDtypeStruct((ng, 8, 128), f32), jax.ShapeDtypeStruct(big_shape, f32)],
        input_output_aliases={n_in: 2, n_in + 1: 5} if aliased else {},
        scratch_shapes=_mix_scratch(1, nseq),
        compiler_params=_cparams(("arbitrary",)),
        name="mixer_block",
    )(p, c0, n0, m0, s0, gb, al, *aliased)


def _outproj_kernel(x_ref, ha_ref, hb_ref, po_ref, pz_ref, anw_ref, bnw_ref,
                    wm_ref, wpa_ref, wpb_ref, wo_ref, g_ref, b_ref, o_ref):
    x = x_ref[...]
    mg = _sigmoid(_dot(x.astype(bf16), wm_ref[...]))
    cols = [slice(h * DH, (h + 1) * DH) for h in range(NH)]
    hr = [ha_ref[:, c] for c in cols]
    ob = [hb_ref[:, c] for c in cols]
    mu = [jnp.mean(v, axis=1, keepdims=True) for v in hr]
    osq = [jnp.mean(v * v, axis=1, keepdims=True) for v in ob]
    hc = [hr[h] - mu[h] for h in range(NH)]
    var = [jnp.mean(v * v, axis=1, keepdims=True) for v in hc]
    ha = [_sigmoid(po_ref[:, cols[h]]) * (hc[h] * lax.rsqrt(var[h] + NORM_EPS) * anw_ref[:, cols[h]])
          for h in range(NH)]
    hb = [(ob[h] * lax.rsqrt(osq[h] + NORM_EPS) * bnw_ref[...]) * _silu(pz_ref[:, cols[h]]) for h in range(NH)]
    ya = _dot(jnp.concatenate(ha, axis=1).astype(bf16), wpa_ref[...])
    yb = _dot(jnp.concatenate(hb, axis=1).astype(bf16), wpb_ref[...])
    y = mg[:, :D_MODEL] * ya + mg[:, D_MODEL:] * yb
    mix = _dot(y.astype(bf16), wo_ref[...])
    o_ref[...] = _ln_rows(ALPHA * x + mix, g_ref[...], b_ref[...])


def _outproj(x, ha, hb, p, anw, bnw, w_merge, layer, w_pa, w_pb, w_out, g, b):
    m = x.shape[0]
    tm = min(m, PROJ_TM)
    row_spec = lambda w: pl.BlockSpec((tm, w), lambda i: (i, 0))
    pcol = lambda c0: pl.BlockSpec((tm, WB), lambda i: (i, c0 // WB))
    const = lambda shape: pl.BlockSpec(shape, lambda i: (0, 0))
    return pl.pallas_call(
        _outproj_kernel,
        grid=(m // tm,),
        in_specs=[row_spec(D_MODEL), row_spec(WB), row_spec(WB), pcol(C_OA), pcol(C_Z),
                  const((1, WB)), const((1, DH)),
                  pl.BlockSpec((None, D_MODEL, 2 * D_MODEL), lambda i: (layer, 0, 0)),
                  const((WB, D_MODEL)), const((WB, D_MODEL)),
                  const((D_MODEL, D_MODEL)), const((1, D_MODEL)), const((1, D_MODEL))],
        out_specs=row_spec(D_MODEL),
        out_shape=jax.ShapeDtypeStruct((m, D_MODEL), f32),
        compiler_params=_cparams(("arbitrary",)),
        name="merge_outproj_ln",
    )(x, ha, hb, p, p, anw, bnw, w_merge, w_pa, w_pb, w_out, g, b)


def _ffn2_kernel(long_mode, tm, nmt, blk, valid, *refs):
    if long_mode:
        (x_ref, wa_ref, wb_ref, cwa_ref, cwb_ref, wd_ref, g_ref, b_ref, ha_ref, hb_ref,
         o_ref, sa_ref, sb_ref, hh_ref, ca_ref, cb_ref) = refs

        @pl.when(pl.program_id(0) % nmt == 0)
        def _():
            ca_ref[...] = ha_ref[0]
            cb_ref[...] = hb_ref[0]
    else:
        (x_ref, wa_ref, wb_ref, cwa_ref, cwb_ref, wd_ref, g_ref, b_ref, ea_ref, eb_ref,
         o_ref, sa_ref, sb_ref, hh_ref) = refs
        row = lax.broadcasted_iota(jnp.int32, (tm, FT), 0)
        vm = (row & (blk - 1)) >= (blk - valid)
    xb = x_ref[...].astype(bf16)
    for cb in range(NF):
        cols = slice(cb * FT, (cb + 1) * FT)
        ua = _dot(xb, wa_ref[:, cols])
        ub = _dot(xb, wb_ref[:, cols])
        if long_mode:
            h8a, h8b = ca_ref[:, cols], cb_ref[:, cols]
            ta, tb = ua[tm - HDR:tm], ub[tm - HDR:tm]
            ca_ref[:, cols] = ta
            cb_ref[:, cols] = tb
            sa_ref[0, :, cols] = ta
            sb_ref[0, :, cols] = tb
        else:
            h8a = h8b = None
            ua = jnp.where(vm, ua, ea_ref[:, cols])
            ub = jnp.where(vm, ub, eb_ref[:, cols])
            sa_ref[:, cols] = ua
            sb_ref[:, cols] = ub
        hh = _silu(_conv_rows(ua, h8a, cwa_ref[:, cols])) * _conv_rows(ub, h8b, cwb_ref[:, cols])
        hh_ref[:, cols] = hh.astype(bf16)
    out = _dot(hh_ref[...], wd_ref[...])
    o_ref[...] = _ln_rows(ALPHA * x_ref[...] + out, g_ref[...], b_ref[...])


def _ffn2(x, w_up, cw, w_down, layer, g, b, hist_a=None, hist_b=None,
          e_ab=None, nb=1, blk=CH, valid=CH):
    m = x.shape[0]
    long_mode = hist_a is not None
    seq = m // nb
    tm = min(seq, FFN_TM if long_mode else FFN_TM_BLOCK)
    nmt = seq // tm
    rows = lambda w: pl.BlockSpec((tm, w), lambda i: (i, 0))
    const = lambda shape: pl.BlockSpec(shape, lambda i: (0,) * len(shape))
    lsp = lambda shape, j: pl.BlockSpec((None,) + shape, lambda i: (layer, 0, j))
    in_specs = [rows(D_MODEL), lsp((D_MODEL, D_FF), 0), lsp((D_MODEL, D_FF), 1), lsp((CONV_F, D_FF), 0),
                lsp((CONV_F, D_FF), 1), lsp((D_FF, D_MODEL), 0), const((1, D_MODEL)), const((1, D_MODEL))]
    scratch = [pltpu.VMEM((tm, D_FF), bf16)]
    if long_mode:
        in_specs += [const((1, HDR, D_FF))] * 2
        st_spec = pl.BlockSpec((1, HDR, D_FF), lambda i: (i, 0, 0))
        st_shape = jax.ShapeDtypeStruct((m // tm, HDR, D_FF), f32)
        scratch += [pltpu.VMEM((HDR, D_FF), f32), pltpu.VMEM((HDR, D_FF), f32)]
        extra = (hist_a, hist_b)
    else:
        in_specs += [pl.BlockSpec((tm, D_FF), lambda i: (i, 0)), pl.BlockSpec((tm, D_FF), lambda i: (i, 1))]
        st_spec = rows(D_FF)
        st_shape = jax.ShapeDtypeStruct((m, D_FF), f32)
        extra = (e_ab, e_ab)
    return pl.pallas_call(
        functools.partial(_ffn2_kernel, long_mode, tm, nmt, blk, valid),
        grid=(m // tm,),
        in_specs=in_specs,
        out_specs=[rows(D_MODEL), st_spec, st_spec],
        out_shape=[jax.ShapeDtypeStruct((m, D_MODEL), f32), st_shape, st_shape],
        scratch_shapes=scratch,
        compiler_params=_cparams(("arbitrary",)),
        name="conv_ffn_ln",
    )(x, w_up, w_up, cw, cw, w_down, g, b, *extra)


def _mix_weights(w_in):
    wt = jnp.swapaxes(w_in, 1, 2).astype(bf16)
    zr = jnp.zeros((wt.shape[0], 128 - 4 * NH, D_MODEL), bf16)
    wt_mix = jnp.concatenate([wt[:, A_Q:A_I], wt[:, B_Z:B_BETA], wt[:, A_I:B_QKV], wt[:, B_BETA:G_MERGE], zr,
                              wt[:, B_QKV:B_Z]], axis=1)
    wt_mix, wt_merge = lax.optimization_barrier((wt_mix, wt[:, G_MERGE:]))
    return jnp.swapaxes(wt_mix, 1, 2), jnp.swapaxes(wt_merge, 1, 2)


def _layer_weights(l, w_mix, w_merge, mlstm_gate_bias, mlstm_norm_w, gdn_conv_w, gdn_A_log, gdn_dt_bias, gdn_norm_w,
                   w_branch_a, w_branch_b, w_out, ln1_g, ln1_b, w_up, ffn_conv_w, w_down, ln2_g, ln2_b):
    z4 = jnp.zeros((NH,), f32)
    gb = jnp.concatenate([mlstm_gate_bias[l], z4, gdn_dt_bias[l], jnp.zeros((128 - 4 * NH,), f32)])[None]
    al = jnp.concatenate([z4, z4, z4, gdn_A_log[l], jnp.zeros((128 - 4 * NH,), f32)])[None]
    return dict(
        layer=l, w_mix=w_mix, w_merge=w_merge, gb=gb, al=al,
        anw=mlstm_norm_w[l][None], bnw=gdn_norm_w[l][None], cw=gdn_conv_w[l],
        w_pa=w_branch_a[l].astype(bf16), w_pb=w_branch_b[l].astype(bf16), w_out=w_out[l].astype(bf16),
        ln1_g=ln1_g[l][None], ln1_b=ln1_b[l][None],
        w_up=w_up, cw_f=ffn_conv_w, w_down=w_down, ln2_g=ln2_g[l][None], ln2_b=ln2_b[l][None])


def _block_layer(x, lw, st, blk, valid, layer=None, prev=None, ln=None):
    c0, n0, m0, s0, gbuf, fbuf = st
    nseq_tot = n0.shape[0]
    nseq = CH // blk
    ng = nseq_tot // nseq
    front = blk - valid
    m0p = jnp.pad(m0.reshape(ng, nseq, NH), ((0, 0), (0, 8 - nseq), (0, 128 - NH)))
    e = jnp.pad(gbuf, ((0, 0), (front - (CONV_B - 1), valid), (0, 0))).reshape(nseq_tot * blk, 3 * WB)
    ef = jnp.pad(fbuf, ((0, 0), (front - (CONV_F - 1), valid), (0, 0))).reshape(nseq_tot * blk, 2 * D_FF)
    if ln is None:
        p, ext = _inproj(x, lw["w_mix"], lw["layer"], lw["cw"], e=e, blk=blk, valid=valid)
    else:
        p, ext, x = _inproj(x, lw["w_mix"], lw["layer"], lw["cw"], e=e, blk=blk, valid=valid, ln=ln)
    ha, hb, c, n, m, s = _mixer_block(p, c0, n0, m0p, s0, lw["gb"], lw["al"], blk, valid, layer, prev)
    x1 = _outproj(x, ha, hb, p, lw["anw"], lw["bnw"], lw["w_merge"], lw["layer"], lw["w_pa"], lw["w_pb"],
                  lw["w_out"],
                  lw["ln1_g"], lw["ln1_b"])
    x2, ua, ub = _ffn2(x1, lw["w_up"], lw["cw_f"], lw["w_down"], lw["layer"],
                       lw["ln2_g"], lw["ln2_b"], e_ab=ef, blk=blk, valid=valid)
    m_new = m[:, :nseq, :NH].reshape(nseq_tot, NH)
    tail = lambda a, w, k: a.reshape(nseq_tot, blk, w)[:, blk - k:]
    gconv = tail(ext, 3 * WB, CONV_B - 1)
    fconv = jnp.concatenate([tail(ua, D_FF, CONV_F - 1), tail(ub, D_FF, CONV_F - 1)], axis=2)
    return x2, (c, n, m_new, s, gconv, fconv)


def _long_layer(x, lw, st, nb, seq, ln=None):
    c0, n0, m0, s0, gbuf, fbuf = st
    m0p = jnp.pad(m0.reshape(1, 1, NH), ((0, 0), (0, 7), (0, 128 - NH)))
    hist = jnp.pad(gbuf, ((0, 0), (HDR - (CONV_B - 1), 0), (0, 0)))
    hf = jnp.pad(fbuf, ((0, 0), (HDR - (CONV_F - 1), 0), (0, 0)))
    if ln is None:
        p, tails = _inproj(x, lw["w_mix"], lw["layer"], lw["cw"], hist=hist, nb=nb)
    else:
        p, tails, x = _inproj(x, lw["w_mix"], lw["layer"], lw["cw"], hist=hist, nb=nb, ln=ln)
    ha, hb, c, n, m, s = _mixer_long(p, c0, n0, m0p, s0, lw["gb"], lw["al"], nb, seq, MIX_T)
    x1 = _outproj(x, ha, hb, p, lw["anw"], lw["bnw"], lw["w_merge"], lw["layer"], lw["w_pa"], lw["w_pb"],
                  lw["w_out"],
                  lw["ln1_g"], lw["ln1_b"])
    x2, sa, sb = _ffn2(x1, lw["w_up"], lw["cw_f"], lw["w_down"], lw["layer"],
                       lw["ln2_g"], lw["ln2_b"], hist_a=hf[:, :, :D_FF], hist_b=hf[:, :, D_FF:], nb=nb)
    last_tile = lambda a: a.reshape((nb, -1) + a.shape[1:])[:, -1]
    gconv = last_tile(tails)[:, HDR - (CONV_B - 1):]
    fconv = jnp.concatenate([last_tile(sa), last_tile(sb)], axis=2)[:, HDR - (CONV_F - 1):]
    return x2, (c, n, m[:, 0, :NH], s, gconv, fconv)


def kernel(x_prompt, x_sample, state_mlstm_C, state_mlstm_n, state_mlstm_m, state_gdn_S, state_gdn_conv, state_ffn_conv, meta_tokens, ln_emb_g, ln_emb_b, w_in, mlstm_gate_bias, mlstm_norm_w, gdn_conv_w, gdn_A_log, gdn_dt_bias, gdn_norm_w, w_branch_a, w_branch_b, w_out, ln1_g, ln1_b, w_up, ffn_conv_w, w_down, ln2_g, ln2_b):
    nb, seq, _ = x_prompt.shape
    ns, ls, _ = x_sample.shape
    sblk = 8
    assert seq % max(MIX_T, PROJ_TM, FFN_TM) == 0 and ls + CONV_B - 1 <= sblk and ns % (CH // sblk) == 0 and N_META + CONV_B - 1 <= CH
    w_mix, w_merge = _mix_weights(w_in)
    w_up_b16, w_down_b16 = w_up.astype(bf16), w_down.astype(bf16)
    lws =[_layer_weights(l, w_mix, w_merge, mlstm_gate_bias, mlstm_norm_w, gdn_conv_w, gdn_A_log, gdn_dt_bias,
                          gdn_norm_w, w_branch_a, w_branch_b, w_out, ln1_g, ln1_b, w_up_b16, ffn_conv_w,
                          w_down_b16, ln2_g, ln2_b) for l in range(DEPTH)]
    emb_ln = (ln_emb_g[None], ln_emb_b[None])

    xm = jnp.pad(meta_tokens, ((CH - N_META, 0), (0, 0)))
    xs = jnp.pad(x_sample, ((0, 0), (sblk - ls, 0), (0, 0))).reshape(ns * sblk, D_MODEL)
    xp = x_prompt.reshape(nb * seq, D_MODEL)

    zero_st = (jnp.zeros((1, NH, DH, DH), f32), jnp.zeros((1, NH, DH), f32), jnp.zeros((1, NH), f32),
               jnp.zeros((1, NH, DH, DH), f32), jnp.zeros((1, CONV_B - 1, 3 * WB), f32),
               jnp.zeros((1, CONV_F - 1, 2 * D_FF), f32))
    p_states, s_states = [], []
    big = None
    for l in range(DEPTH):
        ln = emb_ln if l == 0 else None
        xm, st_m = _block_layer(xm, lws[l], zero_st, CH, N_META, ln=ln)
        xp, st_p = _long_layer(xp, lws[l], st_m, nb, seq, ln=ln)
        samp_st = (state_mlstm_C, state_mlstm_n[l], state_mlstm_m[l], state_gdn_S,
                   state_gdn_conv[l], state_ffn_conv[l])
        xs, st_s = _block_layer(xs, lws[l], samp_st, sblk, ls, layer=l, prev=big, ln=ln)
        big = (st_s[0], st_s[3])
        p_states.append(st_p)
        s_states.append(st_s)

    stack = lambda states, i: jnp.stack([s[i] for s in states], axis=0)
    y_prompt = xp.reshape(nb, seq, D_MODEL)
    y_sample = xs.reshape(ns, sblk, D_MODEL)[:, sblk - ls:]
    return (y_prompt, y_sample,
            stack(p_states, 0), stack(p_states, 1), stack(p_states, 2), stack(p_states, 3),
            stack(p_states, 4), stack(p_states, 5),
            big[0], stack(s_states, 1), stack(s_states, 2), big[1],
            stack(s_states, 4), stack(s_states, 5))
```

```python
import functools

import jax
import jax.numpy as jnp
from jax import lax
from jax.experimental import pallas as pl
from jax.experimental.pallas import tpu as pltpu

f32 = jnp.float32
bf16 = jnp.bfloat16

D_MODEL = 1024
N_META = 16
NH = 4
DH = 128
WB = NH * DH
CONV_B = 4
D_FF = 2816
CONV_F = 3
DEPTH = 2
ALPHA = (2 * DEPTH) ** 0.25
LN_EPS = 1e-5
NORM_EPS = 1e-6
QSCALE = DH ** -0.5

A_Q = 0
A_I = 4 * WB
B_QKV = A_I + 2 * NH
B_Z = B_QKV + 3 * WB
B_BETA = B_Z + WB
G_MERGE = B_BETA + 2 * NH

C_QA, C_KA, C_VA, C_OA = 0, WB, 2 * WB, 3 * WB
C_Z = 4 * WB
C_G = 5 * WB
C_QKV = C_G + 128
N_MIX = C_QKV + 3 * WB

CH = 64
SUB = 16
A_GROUP = 8
HDR = 8
NEG = -1e30
FT = 256
NF = D_FF // FT
FFN_TM_BLOCK = 256
FFN_TM = 512
MIX_T = 512
PROJ_TM = 512
VMEM_LIMIT = 56 * 1024 * 1024


def _cparams(sem):
    return pltpu.CompilerParams(dimension_semantics=sem, vmem_limit_bytes=VMEM_LIMIT)


def _dot(a, b):
    return jnp.dot(a, b, preferred_element_type=f32)


def _dot_nt(a, b):
    return lax.dot_general(a, b, (((1,), (1,)), ((), ())), preferred_element_type=f32)


def _dot_tn(a, b):
    return lax.dot_general(a, b, (((0,), (0,)), ((), ())), preferred_element_type=f32)


def _sel_dot(sel, x):
    hi = x.astype(bf16)
    r1 = x - hi.astype(f32)
    mid = r1.astype(bf16)
    lo = (r1 - mid.astype(f32)).astype(bf16)
    return (_dot(sel, hi) + _dot(sel, mid)) + _dot(sel, lo)


def _softplus(x):
    return jnp.maximum(x, 0.0) + jnp.log1p(jnp.exp(-jnp.abs(x)))


def _sigmoid(x):
    return 1.0 / (1.0 + jnp.exp(-x))


def _silu(x):
    return x * _sigmoid(x)


def _ln_rows(x, g, b):
    mu = jnp.mean(x, axis=-1, keepdims=True)
    xc = x - mu
    var = jnp.mean(xc * xc, axis=-1, keepdims=True)
    return xc * lax.rsqrt(var + LN_EPS) * g + b


def _conv_rows(u, h8, cw):
    width = cw.shape[0]
    row = lax.broadcasted_iota(jnp.int32, (HDR, u.shape[1]), 0) if h8 is not None else None
    acc = None
    for j in range(width):
        sh = width - 1 - j
        term = u if sh == 0 else pltpu.roll(u, sh, 0)
        if h8 is not None and sh > 0:
            head = term[0:HDR]
            for r in range(sh):
                head = jnp.where(row == r, h8[HDR - sh + r:HDR - sh + r + 1], head)
            term = jnp.concatenate([head, term[HDR:]], axis=0)
        term = term * cw[j:j + 1]
        acc = term if acc is None else acc + term
    return acc


def _inproj_kernel(long_mode, with_ln, tm, nmt, blk, valid, *refs):
    if with_ln:
        g_ref, b_ref, xln_ref = refs[4], refs[5], refs[6]
        refs = refs[:4] + refs[7:]
    if long_mode:
        x_ref, w_ref, cw_ref, hist_ref, o_ref, tail_ref, carry_ref = refs

        @pl.when(pl.program_id(0) % nmt == 0)
        def _():
            carry_ref[...] = hist_ref[0]
    else:
        x_ref, w_ref, cw_ref, e_ref, o_ref, tail_ref = refs
        row = lax.broadcasted_iota(jnp.int32, (tm, 2 * DH), 0)
        vm = (row & (blk - 1)) >= (blk - valid)
    if with_ln:
        xln = _ln_rows(x_ref[...], g_ref[...], b_ref[...])
        xln_ref[...] = xln
        xb = xln.astype(bf16)
    else:
        xb = x_ref[...].astype(bf16)
    nstep = 3 * NH // 2
    cuts = [C_QA + WB] + [C_KA + (C_QKV - C_KA) * (i + 1) // (nstep - 1) // 128 * 128 for i in range(nstep - 1)]
    cuts[-1] = C_QKV
    for cb in range(nstep):
        cols = slice(cb * 2 * DH, (cb + 1) * 2 * DH)
        u = _dot(xb, w_ref[:, C_QKV + cb * 2 * DH:C_QKV + (cb + 1) * 2 * DH])
        lo = C_QA if cb == 0 else cuts[cb - 1]
        plain = _dot(xb, w_ref[:, lo:cuts[cb]])
        o_ref[:, lo:cuts[cb]] = plain * QSCALE if cb == 0 else plain
        if long_mode:
            h8 = carry_ref[:, cols]
            t8 = u[tm - HDR:tm]
            carry_ref[:, cols] = t8
            tail_ref[0, :, cols] = t8
        else:
            h8 = None
            u = jnp.where(vm, u, e_ref[:, cols])
            tail_ref[:, cols] = u
        c = _silu(_conv_rows(u, h8, cw_ref[:, cols]))
        for half in range(2):
            ch = c[:, half * DH:(half + 1) * DH]
            if cb < NH:
                ch = ch * lax.rsqrt(jnp.sum(ch * ch, axis=1, keepdims=True) + NORM_EPS)
            if cb < NH // 2:
                ch = ch * QSCALE
            c0 = C_QKV + (2 * cb + half) * DH
            o_ref[:, c0:c0 + DH] = ch


def _inproj(x, w_mix, layer, cw, hist=None, e=None, nb=1, blk=CH, valid=CH, ln=None):
    m = x.shape[0]
    long_mode = hist is not None
    seq = m // nb
    tm = min(seq, PROJ_TM)
    nmt = seq // tm
    rows = lambda w: pl.BlockSpec((tm, w), lambda i: (i, 0))
    const = lambda shape: pl.BlockSpec(shape, lambda i: (0,) * len(shape))
    in_specs = [rows(D_MODEL), pl.BlockSpec((None, D_MODEL, N_MIX), lambda i: (layer, 0, 0)),
                const((CONV_B, 3 * WB))]
    if long_mode:
        in_specs += [const((1, HDR, 3 * WB))]
        tail_spec = pl.BlockSpec((1, HDR, 3 * WB), lambda i: (i, 0, 0))
        tail_shape = jax.ShapeDtypeStruct((m // tm, HDR, 3 * WB), f32)
        scratch = [pltpu.VMEM((HDR, 3 * WB), f32)]
        extra = hist
    else:
        in_specs += [rows(3 * WB)]
        tail_spec = rows(3 * WB)
        tail_shape = jax.ShapeDtypeStruct((m, 3 * WB), f32)
        scratch = []
        extra = e
    args = [x, w_mix, cw, extra]
    out_specs = [rows(N_MIX), tail_spec]
    out_shape = [jax.ShapeDtypeStruct((m, N_MIX), f32), tail_shape]
    if ln is not None:
        in_specs += [const((1, D_MODEL))] * 2
        args += list(ln)
        out_specs = [rows(D_MODEL)] + out_specs
        out_shape = [jax.ShapeDtypeStruct((m, D_MODEL), f32)] + out_shape
    res = pl.pallas_call(
        functools.partial(_inproj_kernel, long_mode, ln is not None, tm, nmt, blk, valid),
        grid=(m // tm,),
        in_specs=in_specs,
        out_specs=out_specs,
        out_shape=out_shape,
        scratch_shapes=scratch,
        compiler_params=_cparams(("arbitrary",)),
        name="mixer_inproj",
    )(*args)
    return res if ln is None else (res[1], res[2], res[0])


def _chunk_consts(blk, valid):
    lg = blk.bit_length() - 1
    r = lax.broadcasted_iota(jnp.int32, (CH, CH), 0)
    c = lax.broadcasted_iota(jnp.int32, (CH, CH), 1)
    same = (r >> lg) == (c >> lg)
    tri = jnp.logical_and(same, c <= r)
    stri = jnp.logical_and(same, c < r)
    sel = lambda cond: jnp.where(cond, 1.0, 0.0).astype(bf16)
    tri_m = sel(tri)
    blk_m = sel(same)
    re = lax.broadcasted_iota(jnp.int32, (CH, 8), 0)
    ce = lax.broadcasted_iota(jnp.int32, (CH, 8), 1)
    expand_m = sel(ce == (re >> lg))
    rs = lax.broadcasted_iota(jnp.int32, (8, CH), 0)
    cs = lax.broadcasted_iota(jnp.int32, (8, CH), 1)
    rowsel_m = sel(cs == (rs << lg) + (blk - 1))
    rv = lax.broadcasted_iota(jnp.int32, (CH, 128), 0)
    valid_m = (rv & (blk - 1)) >= (blk - valid)
    lgs = min(SUB, blk).bit_length() - 1
    same_sub = (r >> lgs) == (c >> lgs)
    eye = jnp.where(r == c, 1.0, 0.0)
    return dict(tri=tri, stri=stri, tri_m=tri_m, blk_m=blk_m, same_sub=same_sub, eye=eye,
                expand_m=expand_m, rowsel_m=rowsel_m, valid_m=valid_m)


def _block_max(x, blk):
    if blk == CH:
        return jnp.broadcast_to(jnp.max(x, axis=0, keepdims=True), x.shape)
    x3 = x.reshape(CH // blk, blk, 128)
    return jnp.broadcast_to(jnp.max(x3, axis=1, keepdims=True), x3.shape).reshape(CH, 128)


def _mix_scratch(nc, nseq):
    return [pltpu.VMEM((nc, 6, CH, 128), f32),
            pltpu.VMEM((nc, NH, CH, DH), f32),
            pltpu.VMEM((nc, nseq * NH, DH, DH), f32),
            pltpu.VMEM((nc, nseq, 8, DH), f32),
            pltpu.VMEM((nc, nseq * NH, DH, DH), bf16),
            pltpu.VMEM((nc, nseq * NH, DH, DH), f32),
            pltpu.VMEM((nc, NH, CH, DH), bf16),
            pltpu.VMEM((nc, NH, CH, DH), f32)]


def _phase_a(blk, valid, k, chunks, par, sc):
    nseq = CH // blk
    gbias, alog = par[0], par[1]
    t_ref, numl_ref, kv_ref, nv_ref, m1_ref, m2_ref, qeff_ref, o2_ref = sc
    tri, stri = k["tri"], k["stri"]
    seqs = [slice(i * blk, (i + 1) * blk) for i in range(nseq)]
    nch = len(chunks)
    units = [(c, h) for c in range(nch) for h in range(NH)]
    lane = lax.broadcasted_iota(jnp.int32, (CH, 128), 1)
    cum_lane = jnp.logical_or(jnp.logical_and(lane >= 4, lane < 8), lane >= 12)

    gt = []
    for ci, pr in chunks:
        graw = pr(C_G, 128) + gbias
        g = jnp.where(lane < 4, graw,
                      jnp.where(lane < 8, -_softplus(-graw),
                                jnp.where(lane < 12, _sigmoid(graw),
                                          jnp.where(lane < 16, -jnp.exp(alog) * _softplus(graw), 0.0))))
        if valid < blk:
            g = jnp.where(k["valid_m"], g, jnp.where(lane < 4, NEG, 0.0))
        gt.append(g)
    cs = [_sel_dot(k["tri_m"], g) for g in gt]
    if nseq == 1:
        tot = [jnp.broadcast_to(x[CH - 1:CH, :], (CH, 128)) for x in cs]
    else:
        tot = [_sel_dot(k["blk_m"], g) for g in gt]
    mix_t = [jnp.transpose(jnp.where(cum_lane, cs[c], gt[c])) for c in range(nch)]
    b_all = [pltpu.roll(x, 124, 1) for x in cs]
    btot_all = [pltpu.roll(x, 124, 1) for x in tot]
    wlog = [btot_all[c] - b_all[c] + gt[c] for c in range(nch)]
    mloc_all = [_block_max(x, blk) for x in wlog]
    wsrc_all = [jnp.exp(wlog[c] - mloc_all[c]) for c in range(nch)]
    expg_all = [jnp.exp(x) for x in cs]
    expdiff_all = [jnp.exp(tot[c] - cs[c]) for c in range(nch)]

    qa, ka, va, qg, kgf, vg = {}, {}, {}, {}, {}, {}
    for c, h in units:
        _, pr = chunks[c]
        qa[c, h] = pr(C_QA + h * DH, DH).astype(bf16)
        ka[c, h] = pr(C_KA + h * DH, DH)
        va[c, h] = pr(C_VA + h * DH, DH).astype(bf16)
        qg[c, h] = pr(C_QKV + h * DH, DH)
        kgf[c, h] = pr(C_QKV + WB + h * DH, DH)
        vg[c, h] = pr(C_QKV + 2 * WB + h * DH, DH)
    qgb = {u: qg[u].astype(bf16) for u in units}
    kg = {u: kgf[u].astype(bf16) for u in units}

    sq = {u: _dot_nt(qa[u], ka[u].astype(bf16)) for u in units}
    kk = {u: _dot_nt(kg[u], kg[u]) for u in units}
    qk = {u: _dot_nt(qgb[u], kg[u]) for u in units}

    dm = {(c, h): jnp.where(tri, cs[c][:, 4 + h:5 + h] - mix_t[c][4 + h:5 + h, :] + mix_t[c][h:h + 1, :], NEG)
          for c, h in units}
    dmax = {u: jnp.max(dm[u], axis=1, keepdims=True) for u in units}
    s = {u: sq[u] * jnp.exp(dm[u] - dmax[u]) for u in units}
    denl = {u: jnp.sum(s[u], axis=1, keepdims=True) for u in units}
    sloc = {u: s[u].astype(bf16) for u in units}
    kwl = {(c, h): ka[c, h] * wsrc_all[c][:, h:h + 1] for c, h in units}
    for c in range(nch):
        ci = chunks[c][0]
        dmax_all = jnp.zeros((CH, 128), f32)
        denl_all = jnp.zeros((CH, 128), f32)
        for h in range(NH):
            dmax_all = jnp.where(lane == h, dmax[c, h], dmax_all)
            denl_all = jnp.where(lane == h, denl[c, h], denl_all)
        t_ref[ci, 0] = gt[c]
        t_ref[ci, 1] = b_all[c]
        t_ref[ci, 2] = btot_all[c]
        t_ref[ci, 3] = dmax_all
        t_ref[ci, 4] = denl_all
        t_ref[ci, 5] = mloc_all[c]

    xs, pw, qkd, kd = {}, {}, {}, {}
    for c, h in units:
        beta = gt[c][:, 8 + h:9 + h]
        decay = jnp.exp(jnp.where(tri, cs[c][:, 12 + h:13 + h] - mix_t[c][12 + h:13 + h, :], NEG))
        pw[c, h] = jnp.where(stri, beta * kk[c, h] * decay, 0.0)
        xs[c, h] = jnp.concatenate([beta * vg[c, h], (beta * expg_all[c][:, 12 + h:13 + h]) * kgf[c, h]], axis=1)
        qkd[c, h] = (qk[c, h] * decay).astype(bf16)
        kd[c, h] = (kgf[c, h] * expdiff_all[c][:, 12 + h:13 + h]).astype(bf16)

    for c, h in units:
        ci = chunks[c][0]
        numl_ref[ci, h] = _dot(sloc[c, h], va[c, h])
        kwb = kwl[c, h].astype(bf16)
        for i, sl in enumerate(seqs):
            kv_ref[ci, i * NH + h] = _dot_tn(kwb[sl], va[c, h][sl])
            nv_ref[ci, i, h:h + 1, :] = jnp.sum(kwl[c, h][sl], axis=0, keepdims=True)

    sub = min(SUB, blk)
    a_d = {u: jnp.where(k["same_sub"], pw[u], 0.0) for u in units}
    tinv = {u: k["eye"] - a_d[u] for u in units}
    pcur = {u: -a_d[u] for u in units}
    for it in range(1, sub.bit_length() - 1):
        pb = {u: pcur[u].astype(bf16) for u in units}
        pcur = {u: _dot(pb[u], pb[u]) for u in units}
        tinv = {u: tinv[u] + _dot(pcur[u].astype(bf16), tinv[u].astype(bf16)) for u in units}
    tb = {u: tinv[u].astype(bf16) for u in units}
    if sub == blk:
        xs = {u: _dot(tb[u], xs[u].astype(bf16)) for u in units}
    else:
        a_off = {u: (pw[u] - a_d[u]).astype(bf16) for u in units}
        nsub = CH // sub
        done = {u: [] for u in units}
        for j in range(nsub):
            rows = slice(j * sub, (j + 1) * sub)
            pad = lambda parts, n: parts + ([jnp.zeros((n, 2 * DH), f32)] if n else [])
            v = {u: xs[u][rows] for u in units}
            if j > 0:
                xcat = {u: jnp.concatenate(pad(done[u], CH - j * sub), axis=0).astype(bf16) for u in units}
                v = {u: v[u] - _dot(a_off[u][rows], xcat[u]) for u in units}
            vcat = {u: jnp.concatenate(pad(pad([], j * sub) + [v[u]], CH - (j + 1) * sub), axis=0).astype(bf16)
                    for u in units}
            for u in units:
                done[u].append(_dot(tb[u][rows], vcat[u]))
        xs = {u: jnp.concatenate(done[u], axis=0) for u in units}

    xb = {u: xs[u].astype(bf16) for u in units}
    qx = {u: _dot(qkd[u], xb[u]) for u in units}
    for c, h in units:
        ci = chunks[c][0]
        o2_ref[ci, h] = qx[c, h][:, :DH]
        qeff_ref[ci, h] = (expg_all[c][:, 12 + h:13 + h] * qg[c, h] - qx[c, h][:, DH:]).astype(bf16)
        for i, sl in enumerate(seqs):
            mm = _dot_tn(kd[c, h][sl], xb[c, h][sl])
            m2_ref[ci, i * NH + h] = mm[:, :DH]
            m1_ref[ci, i * NH + h] = mm[:, DH:].astype(bf16)


def _phase_b(blk, k, pr, par, st, sc, out, ci):
    nseq = CH // blk
    c_ref, n_ref, m_ref, s_ref = st
    t_ref, numl_ref, kv_ref, nv_ref, m1_ref, m2_ref, qeff_ref, o2_ref = sc
    ha_ref, hb_ref, rows = out
    seqs = [slice(i * blk, (i + 1) * blk) for i in range(nseq)]
    heads = range(NH)
    gt, b_all, btot_all = t_ref[ci, 0], t_ref[ci, 1], t_ref[ci, 2]
    dmax_all, denl_all, mloc_all = t_ref[ci, 3], t_ref[ci, 4], t_ref[ci, 5]

    qa = [pr(C_QA + h * DH, DH) for h in heads]
    qc, ms, oq = [], [], []
    for h in heads:
        qb = qa[h].astype(bf16)
        qc.append([_dot(qb[sl], c_ref[i, h].astype(bf16)) for i, sl in enumerate(seqs)])
    for h in heads:
        qe = qeff_ref[ci, h]
        ms_h, oq_h = [], []
        for i, sl in enumerate(seqs):
            sb = s_ref[i, h].astype(bf16)
            ms_h.append(_dot(m1_ref[ci, i * NH + h], sb))
            oq_h.append(_dot(qe[sl], sb))
        ms.append(ms_h)
        oq.append(oq_h)

    if nseq == 1:
        mprev_all = jnp.broadcast_to(m_ref[0:1, :], (CH, 128))
    else:
        mprev_all = _sel_dot(k["expand_m"], m_ref[...])
    inter = b_all + mprev_all
    mt = jnp.maximum(inter, dmax_all)
    scale_all = jnp.exp(dmax_all - mt)
    winter_all = jnp.exp(inter - mt)
    emt_all = jnp.exp(-mt)
    carry = btot_all + mprev_all
    mnew = jnp.maximum(carry, mloc_all)
    sc2_all = jnp.exp(mloc_all - mnew)
    wold_all = jnp.exp(carry - mnew)
    egt_all = jnp.exp(btot_all)
    if nseq == 1:
        m_ref[0:1, :] = mnew[CH - 1:CH, :]
    else:
        m_ref[...] = _sel_dot(k["rowsel_m"], mnew)

    cat = lambda parts: parts[0] if nseq == 1 else jnp.concatenate(parts, axis=0)
    qn = [cat([jnp.sum(qa[h][sl] * n_ref[i, h:h + 1, :], axis=1, keepdims=True) for i, sl in enumerate(seqs)])
          for h in heads]
    hraw = []
    for h in heads:
        scale, winter = scale_all[:, h:h + 1], winter_all[:, h:h + 1]
        num = scale * numl_ref[ci, h] + winter * cat(qc[h])
        den = scale * denl_all[:, h:h + 1] + winter * qn[h]
        hraw.append(num / jnp.maximum(jnp.abs(den), emt_all[:, h:h + 1]))
    o = [cat(oq[h]) + o2_ref[ci, h] for h in heads]

    for h in heads:
        for i in range(nseq):
            last = (i + 1) * blk - 1
            wold = wold_all[last:last + 1, h:h + 1]
            sc2 = sc2_all[last:last + 1, h:h + 1]
            c_ref[i, h] = wold * c_ref[i, h] + sc2 * kv_ref[ci, i * NH + h]
            n_ref[i, h:h + 1, :] = wold * n_ref[i, h:h + 1, :] + sc2 * nv_ref[ci, i, h:h + 1, :]
            s_ref[i, h] = (egt_all[last:last + 1, 8 + h:9 + h] * s_ref[i, h] - ms[h][i]) + m2_ref[ci, i * NH + h]

    for h in heads:
        ha_ref[rows, h * DH:(h + 1) * DH] = hraw[h]
        hb_ref[rows, h * DH:(h + 1) * DH] = o[h]


def _mixer_long_kernel(t, p_ref, c0_ref, n0_ref, m0_ref, s0_ref, gb_ref, al_ref,
                       ha_ref, hb_ref, c_ref, n_ref, m_ref, s_ref, *sc):
    g = pl.program_id(1)

    @pl.when(g == 0)
    def _():
        c_ref[...] = c0_ref[...]
        n_ref[...] = n0_ref[...]
        m_ref[...] = m0_ref[...]
        s_ref[...] = s0_ref[...]

    k = _chunk_consts(CH, CH)
    par = (gb_ref[...], al_ref[...])
    st = (c_ref, n_ref, _M0(m_ref), s_ref)

    def chunk_views(ci):
        rows = pl.ds(pl.multiple_of(ci * CH, CH), CH)
        return ci, (lambda c0, n: p_ref[rows, c0:c0 + n])

    def body_a(j, carry):
        _phase_a(CH, CH, k, [chunk_views(A_GROUP * j + c) for c in range(A_GROUP)], par, sc)
        return carry

    def body_b(ci, carry):
        rows = pl.ds(pl.multiple_of(ci * CH, CH), CH)
        _phase_b(CH, k, lambda c0, n: p_ref[rows, c0:c0 + n], par, st, sc, (ha_ref, hb_ref, rows), ci)
        return carry

    lax.fori_loop(0, t // (CH * A_GROUP), body_a, 0)
    lax.fori_loop(0, t // CH, body_b, 0)


class _M0:
    def __init__(self, ref):
        self.ref = ref

    def __getitem__(self, idx):
        return self.ref[0] if idx is Ellipsis else self.ref[(0,) + idx]

    def __setitem__(self, idx, val):
        if idx is Ellipsis:
            self.ref[0] = val
        else:
            self.ref[(0,) + idx] = val


def _mixer_long(p, c0, n0, m0, s0, gb, al, nb, seq, t):
    nt = seq // t
    row_spec = lambda w: pl.BlockSpec((t, w), lambda b, g: (b * nt + g, 0))
    const = lambda shape: pl.BlockSpec(shape, lambda b, g: (0,) * len(shape))
    perb = lambda shape: pl.BlockSpec((1,) + shape, lambda b, g: (b,) + (0,) * len(shape))
    return pl.pallas_call(
        functools.partial(_mixer_long_kernel, t),
        grid=(nb, nt),
        in_specs=[row_spec(N_MIX),
                  const((1, NH, DH, DH)), const((1, NH, DH)), const((1, 8, 128)), const((1, NH, DH, DH)),
                  const((1, 128)), const((1, 128))],
        out_specs=[row_spec(WB), row_spec(WB),
                   perb((NH, DH, DH)), perb((NH, DH)), perb((8, 128)), perb((NH, DH, DH))],
        out_shape=[jax.ShapeDtypeStruct((nb * seq, WB), f32), jax.ShapeDtypeStruct((nb * seq, WB), f32),
                   jax.ShapeDtypeStruct((nb, NH, DH, DH), f32), jax.ShapeDtypeStruct((nb, NH, DH), f32),
                   jax.ShapeDtypeStruct((nb, 8, 128), f32), jax.ShapeDtypeStruct((nb, NH, DH, DH), f32)],
        scratch_shapes=_mix_scratch(t // CH, 1),
        compiler_params=_cparams(("arbitrary", "arbitrary")),
        name="mixer_long",
    )(p, c0, n0, m0, s0, gb, al)


def _mixer_block_kernel(blk, valid, n_alias, first_of, p_ref, c0_ref, n0_ref, m0_ref, s0_ref, gb_ref, al_ref,
                        *rest):
    ha_ref, hb_ref, c_ref, n_ref, m_ref, s_ref = rest[n_alias:n_alias + 6]
    sc = rest[n_alias + 6:]
    if first_of is not None:
        layer, depth = first_of
        for other in range(depth):
            if other != layer:
                c_ref[other] = jnp.zeros(c_ref.shape[1:], f32)
                s_ref[other] = jnp.zeros(s_ref.shape[1:], f32)
        c_ref, s_ref = c_ref.at[layer], s_ref.at[layer]
    c_ref[...] = c0_ref[...]
    n_ref[...] = n0_ref[...]
    m_ref[...] = m0_ref[...]
    s_ref[...] = s0_ref[...]
    k = _chunk_consts(blk, valid)
    par = (gb_ref[...], al_ref[...])
    pr = lambda c0, n: p_ref[:, c0:c0 + n]
    _phase_a(blk, valid, k, [(0, pr)], par, sc)
    _phase_b(blk, k, pr, par, (c_ref, n_ref, _M0(m_ref), s_ref), sc, (ha_ref, hb_ref, slice(0, CH)), 0)


def _mixer_block(p, c0, n0, m0, s0, gb, al, blk, valid, layer=None, prev=None):
    ng = p.shape[0] // CH
    nseq = CH // blk
    row_spec = lambda w: pl.BlockSpec((CH, w), lambda g: (g, 0))
    const = lambda shape: pl.BlockSpec(shape, lambda g: (0,) * len(shape))
    perg = lambda shape: pl.BlockSpec(shape, lambda g: (g,) + (0,) * (len(shape) - 1))
    big_shape = (ng * nseq, NH, DH, DH)
    first_of = None
    if layer is None:
        big = big_out = perg((nseq, NH, DH, DH))
    else:
        depth = c0.shape[0]
        big = big_out = pl.BlockSpec((None, nseq, NH, DH, DH), lambda g: (layer, g, 0, 0, 0))
        big_shape = (depth,) + big_shape
        if prev is None:
            first_of = (layer, depth)
            big_out = pl.BlockSpec((depth, nseq, NH, DH, DH), lambda g: (0, g, 0, 0, 0))
    aliased = () if prev is None else tuple(prev)
    n_in = 7
    return pl.pallas_call(
        functools.partial(_mixer_block_kernel, blk, valid, len(aliased), first_of),
        grid=(ng,),
        in_specs=[row_spec(N_MIX), big, perg((nseq, NH, DH)), perg((1, 8, 128)), big,
                  const((1, 128)), const((1, 128))]
                 + [pl.BlockSpec(memory_space=pl.ANY)] * len(aliased),
        out_specs=[row_spec(WB), row_spec(WB), big_out, perg((nseq, NH, DH)), perg((1, 8, 128)), big_out],
        out_shape=[jax.ShapeDtypeStruct((ng * CH, WB), f32), jax.ShapeDtypeStruct((ng * CH, WB), f32),
                   jax.ShapeDtypeStruct(big_shape, f32), jax.ShapeDtypeStruct((ng * nseq, NH, DH), f32),
                   jax.ShapeDtypeStruct((ng, 8, 128), f32), jax.ShapeDtypeStruct(big_shape, f32)],
        input_output_aliases={n_in: 2, n_in + 1: 5} if aliased else {},
        scratch_shapes=_mix_scratch(1, nseq),
        compiler_params=_cparams(("arbitrary",)),
        name="mixer_block",
    )(p, c0, n0, m0, s0, gb, al, *aliased)


def _outproj_kernel(x_ref, ha_ref, hb_ref, po_ref, pz_ref, anw_ref, bnw_ref,
                    wm_ref, wpa_ref, wpb_ref, wo_ref, g_ref, b_ref, o_ref):
    x = x_ref[...]
    mg = _sigmoid(_dot(x.astype(bf16), wm_ref[...]))
    cols = [slice(h * DH, (h + 1) * DH) for h in range(NH)]
    hr = [ha_ref[:, c] for c in cols]
    ob = [hb_ref[:, c] for c in cols]
    mu = [jnp.mean(v, axis=1, keepdims=True) for v in hr]
    osq = [jnp.mean(v * v, axis=1, keepdims=True) for v in ob]
    hc = [hr[h] - mu[h] for h in range(NH)]
    var = [jnp.mean(v * v, axis=1, keepdims=True) for v in hc]
    ha = [_sigmoid(po_ref[:, cols[h]]) * (hc[h] * lax.rsqrt(var[h] + NORM_EPS) * anw_ref[:, cols[h]])
          for h in range(NH)]
    hb = [(ob[h] * lax.rsqrt(osq[h] + NORM_EPS) * bnw_ref[...]) * _silu(pz_ref[:, cols[h]]) for h in range(NH)]
    ya = _dot(jnp.concatenate(ha, axis=1).astype(bf16), wpa_ref[...])
    yb = _dot(jnp.concatenate(hb, axis=1).astype(bf16), wpb_ref[...])
    y = mg[:, :D_MODEL] * ya + mg[:, D_MODEL:] * yb
    mix = _dot(y.astype(bf16), wo_ref[...])
    o_ref[...] = _ln_rows(ALPHA * x + mix, g_ref[...], b_ref[...])


def _outproj(x, ha, hb, p, anw, bnw, w_merge, layer, w_pa, w_pb, w_out, g, b):
    m = x.shape[0]
    tm = min(m, PROJ_TM)
    row_spec = lambda w: pl.BlockSpec((tm, w), lambda i: (i, 0))
    pcol = lambda c0: pl.BlockSpec((tm, WB), lambda i: (i, c0 // WB))
    const = lambda shape: pl.BlockSpec(shape, lambda i: (0, 0))
    return pl.pallas_call(
        _outproj_kernel,
        grid=(m // tm,),
        in_specs=[row_spec(D_MODEL), row_spec(WB), row_spec(WB), pcol(C_OA), pcol(C_Z),
                  const((1, WB)), const((1, DH)),
                  pl.BlockSpec((None, D_MODEL, 2 * D_MODEL), lambda i: (layer, 0, 0)),
                  const((WB, D_MODEL)), const((WB, D_MODEL)),
                  const((D_MODEL, D_MODEL)), const((1, D_MODEL)), const((1, D_MODEL))],
        out_specs=row_spec(D_MODEL),
        out_shape=jax.ShapeDtypeStruct((m, D_MODEL), f32),
        compiler_params=_cparams(("arbitrary",)),
        name="merge_outproj_ln",
    )(x, ha, hb, p, p, anw, bnw, w_merge, w_pa, w_pb, w_out, g, b)


def _ffn2_kernel(long_mode, tm, nmt, blk, valid, *refs):
    if long_mode:
        (x_ref, wa_ref, wb_ref, cwa_ref, cwb_ref, wd_ref, g_ref, b_ref, ha_ref, hb_ref,
         o_ref, sa_ref, sb_ref, hh_ref, ca_ref, cb_ref) = refs

        @pl.when(pl.program_id(0) % nmt == 0)
        def _():
            ca_ref[...] = ha_ref[0]
            cb_ref[...] = hb_ref[0]
    else:
        (x_ref, wa_ref, wb_ref, cwa_ref, cwb_ref, wd_ref, g_ref, b_ref, ea_ref, eb_ref,
         o_ref, sa_ref, sb_ref, hh_ref) = refs
        nsq = tm // blk
        r0 = blk - valid - (CONV_F - 1)
        row3 = lax.broadcasted_iota(jnp.int32, (nsq, blk, FT), 1)

        def merge_history(u, e_ref, s_ref, cols):
            u3 = u.reshape(nsq, blk, FT)
            e3 = e_ref[:, :, cols]
            for r in range(CONV_F - 1):
                u3 = jnp.where(row3 == r0 + r, e3[:, r:r + 1, :], u3)
            s_ref[:, :, cols] = u3[:, blk - (CONV_F - 1):, :]
            return u3.reshape(tm, FT)
    xb = x_ref[...].astype(bf16)
    for cb in range(NF):
        cols = slice(cb * FT, (cb + 1) * FT)
        ua = _dot(xb, wa_ref[:, cols])
        ub = _dot(xb, wb_ref[:, cols])
        if long_mode:
            h8a, h8b = ca_ref[:, cols], cb_ref[:, cols]
            ta, tb = ua[tm - HDR:tm], ub[tm - HDR:tm]
            ca_ref[:, cols] = ta
            cb_ref[:, cols] = tb
            sa_ref[0, :, cols] = ta
            sb_ref[0, :, cols] = tb
        else:
            h8a = h8b = None
            ua = merge_history(ua, ea_ref, sa_ref, cols)
            ub = merge_history(ub, eb_ref, sb_ref, cols)
        hh = _silu(_conv_rows(ua, h8a, cwa_ref[:, cols])) * _conv_rows(ub, h8b, cwb_ref[:, cols])
        hh_ref[:, cols] = hh.astype(bf16)
    out = _dot(hh_ref[...], wd_ref[...])
    o_ref[...] = _ln_rows(ALPHA * x_ref[...] + out, g_ref[...], b_ref[...])


def _ffn2(x, w_up, cw, w_down, layer, g, b, hist_a=None, hist_b=None,
          e_ab=None, nb=1, blk=CH, valid=CH):
    m = x.shape[0]
    long_mode = hist_a is not None
    seq = m // nb
    tm = min(seq, FFN_TM if long_mode else FFN_TM_BLOCK)
    nmt = seq // tm
    rows = lambda w: pl.BlockSpec((tm, w), lambda i: (i, 0))
    const = lambda shape: pl.BlockSpec(shape, lambda i: (0,) * len(shape))
    lsp = lambda shape, j: pl.BlockSpec((None,) + shape, lambda i: (layer, 0, j))
    in_specs = [rows(D_MODEL), lsp((D_MODEL, D_FF), 0), lsp((D_MODEL, D_FF), 1), lsp((CONV_F, D_FF), 0),
                lsp((CONV_F, D_FF), 1), lsp((D_FF, D_MODEL), 0), const((1, D_MODEL)), const((1, D_MODEL))]
    scratch = [pltpu.VMEM((tm, D_FF), bf16)]
    if long_mode:
        in_specs += [const((1, HDR, D_FF))] * 2
        st_spec = pl.BlockSpec((1, HDR, D_FF), lambda i: (i, 0, 0))
        st_shape = jax.ShapeDtypeStruct((m // tm, HDR, D_FF), f32)
        scratch += [pltpu.VMEM((HDR, D_FF), f32), pltpu.VMEM((HDR, D_FF), f32)]
        extra = (hist_a, hist_b)
    else:
        nsq = tm // blk
        if e_ab.ndim == 4:
            e_spec = lambda j: pl.BlockSpec((None, nsq, CONV_F - 1, D_FF), lambda i: (layer, i, 0, j))
        else:
            e_spec = lambda j: pl.BlockSpec((nsq, CONV_F - 1, D_FF), lambda i: (i, 0, j))
        in_specs += [e_spec(0), e_spec(1)]
        st_spec = pl.BlockSpec((nsq, CONV_F - 1, D_FF), lambda i: (i, 0, 0))
        st_shape = jax.ShapeDtypeStruct((m // blk, CONV_F - 1, D_FF), f32)
        extra = (e_ab, e_ab)
    return pl.pallas_call(
        functools.partial(_ffn2_kernel, long_mode, tm, nmt, blk, valid),
        grid=(m // tm,),
        in_specs=in_specs,
        out_specs=[rows(D_MODEL), st_spec, st_spec],
        out_shape=[jax.ShapeDtypeStruct((m, D_MODEL), f32), st_shape, st_shape],
        scratch_shapes=scratch,
        compiler_params=_cparams(("arbitrary",)),
        name="conv_ffn_ln",
    )(x, w_up, w_up, cw, cw, w_down, g, b, *extra)


def _mix_weights(w_in):
    w = w_in.astype(bf16)
    zc = jnp.zeros(w.shape[:2] + (128 - 4 * NH,), bf16)
    w_mix = jnp.concatenate([w[..., A_Q:A_I], w[..., B_Z:B_BETA], w[..., A_I:B_QKV], w[..., B_BETA:G_MERGE], zc,
                             w[..., B_QKV:B_Z]], axis=-1)
    return w_mix, w[..., G_MERGE:]


def _layer_weights(l, w_mix, w_merge, mlstm_gate_bias, mlstm_norm_w, gdn_conv_w, gdn_A_log, gdn_dt_bias, gdn_norm_w,
                   w_branch_a, w_branch_b, w_out, ln1_g, ln1_b, w_up, ffn_conv_w, w_down, ln2_g, ln2_b):
    z4 = jnp.zeros((NH,), f32)
    gb = jnp.concatenate([mlstm_gate_bias[l], z4, gdn_dt_bias[l], jnp.zeros((128 - 4 * NH,), f32)])[None]
    al = jnp.concatenate([z4, z4, z4, gdn_A_log[l], jnp.zeros((128 - 4 * NH,), f32)])[None]
    return dict(
        layer=l, w_mix=w_mix, w_merge=w_merge, gb=gb, al=al,
        anw=mlstm_norm_w[l][None], bnw=gdn_norm_w[l][None], cw=gdn_conv_w[l],
        w_pa=w_branch_a[l].astype(bf16), w_pb=w_branch_b[l].astype(bf16), w_out=w_out[l].astype(bf16),
        ln1_g=ln1_g[l][None], ln1_b=ln1_b[l][None],
        w_up=w_up, cw_f=ffn_conv_w, w_down=w_down, ln2_g=ln2_g[l][None], ln2_b=ln2_b[l][None])


def _block_layer(x, lw, st, blk, valid, layer=None, prev=None, ln=None):
    c0, n0, m0, s0, gbuf, fbuf = st
    nseq_tot = n0.shape[0]
    nseq = CH // blk
    ng = nseq_tot // nseq
    front = blk - valid
    m0p = jnp.pad(m0.reshape(ng, nseq, NH), ((0, 0), (0, 8 - nseq), (0, 128 - NH)))
    e = jnp.pad(gbuf, ((0, 0), (front - (CONV_B - 1), valid), (0, 0))).reshape(nseq_tot * blk, 3 * WB)
    if ln is None:
        p, ext = _inproj(x, lw["w_mix"], lw["layer"], lw["cw"], e=e, blk=blk, valid=valid)
    else:
        p, ext, x = _inproj(x, lw["w_mix"], lw["layer"], lw["cw"], e=e, blk=blk, valid=valid, ln=ln)
    ha, hb, c, n, m, s = _mixer_block(p, c0, n0, m0p, s0, lw["gb"], lw["al"], blk, valid, layer, prev)
    x1 = _outproj(x, ha, hb, p, lw["anw"], lw["bnw"], lw["w_merge"], lw["layer"], lw["w_pa"], lw["w_pb"],
                  lw["w_out"],
                  lw["ln1_g"], lw["ln1_b"])
    x2, ua, ub = _ffn2(x1, lw["w_up"], lw["cw_f"], lw["w_down"], lw["layer"],
                       lw["ln2_g"], lw["ln2_b"], e_ab=fbuf, blk=blk, valid=valid)
    m_new = m[:, :nseq, :NH].reshape(nseq_tot, NH)
    tail = lambda a, w, k: a.reshape(nseq_tot, blk, w)[:, blk - k:]
    gconv = tail(ext, 3 * WB, CONV_B - 1)
    fconv = jnp.concatenate([ua, ub], axis=2)
    return x2, (c, n, m_new, s, gconv, fconv)


def _long_layer(x, lw, st, nb, seq, ln=None):
    c0, n0, m0, s0, gbuf, fbuf = st
    m0p = jnp.pad(m0.reshape(1, 1, NH), ((0, 0), (0, 7), (0, 128 - NH)))
    hist = jnp.pad(gbuf, ((0, 0), (HDR - (CONV_B - 1), 0), (0, 0)))
    hf = jnp.pad(fbuf, ((0, 0), (HDR - (CONV_F - 1), 0), (0, 0)))
    if ln is None:
        p, tails = _inproj(x, lw["w_mix"], lw["layer"], lw["cw"], hist=hist, nb=nb)
    else:
        p, tails, x = _inproj(x, lw["w_mix"], lw["layer"], lw["cw"], hist=hist, nb=nb, ln=ln)
    ha, hb, c, n, m, s = _mixer_long(p, c0, n0, m0p, s0, lw["gb"], lw["al"], nb, seq, MIX_T)
    x1 = _outproj(x, ha, hb, p, lw["anw"], lw["bnw"], lw["w_merge"], lw["layer"], lw["w_pa"], lw["w_pb"],
                  lw["w_out"],
                  lw["ln1_g"], lw["ln1_b"])
    x2, sa, sb = _ffn2(x1, lw["w_up"], lw["cw_f"], lw["w_down"], lw["layer"],
                       lw["ln2_g"], lw["ln2_b"], hist_a=hf[:, :, :D_FF], hist_b=hf[:, :, D_FF:], nb=nb)
    last_tile = lambda a: a.reshape((nb, -1) + a.shape[1:])[:, -1]
    gconv = last_tile(tails)[:, HDR - (CONV_B - 1):]
    fconv = jnp.concatenate([last_tile(sa), last_tile(sb)], axis=2)[:, HDR - (CONV_F - 1):]
    return x2, (c, n, m[:, 0, :NH], s, gconv, fconv)


def kernel(x_prompt, x_sample, state_mlstm_C, state_mlstm_n, state_mlstm_m, state_gdn_S, state_gdn_conv, state_ffn_conv, meta_tokens, ln_emb_g, ln_emb_b, w_in, mlstm_gate_bias, mlstm_norm_w, gdn_conv_w, gdn_A_log, gdn_dt_bias, gdn_norm_w, w_branch_a, w_branch_b, w_out, ln1_g, ln1_b, w_up, ffn_conv_w, w_down, ln2_g, ln2_b):
    nb, seq, _ = x_prompt.shape
    ns, ls, _ = x_sample.shape
    sblk = 8
    assert seq % max(MIX_T, PROJ_TM, FFN_TM) == 0 and ls + CONV_B - 1 <= sblk and ns % (CH // sblk) == 0 and N_META + CONV_B - 1 <= CH
    w_mix, w_merge = _mix_weights(w_in)
    w_up_b16, w_down_b16 = w_up.astype(bf16), w_down.astype(bf16)
    lws =[_layer_weights(l, w_mix, w_merge, mlstm_gate_bias, mlstm_norm_w, gdn_conv_w, gdn_A_log, gdn_dt_bias,
                          gdn_norm_w, w_branch_a, w_branch_b, w_out, ln1_g, ln1_b, w_up_b16, ffn_conv_w,
                          w_down_b16, ln2_g, ln2_b) for l in range(DEPTH)]
    emb_ln = (ln_emb_g[None], ln_emb_b[None])

    xm = jnp.pad(meta_tokens, ((CH - N_META, 0), (0, 0)))
    xs = jnp.pad(x_sample, ((0, 0), (sblk - ls, 0), (0, 0))).reshape(ns * sblk, D_MODEL)
    xp = x_prompt.reshape(nb * seq, D_MODEL)

    zero_st = (jnp.zeros((1, NH, DH, DH), f32), jnp.zeros((1, NH, DH), f32), jnp.zeros((1, NH), f32),
               jnp.zeros((1, NH, DH, DH), f32), jnp.zeros((1, CONV_B - 1, 3 * WB), f32),
               jnp.zeros((1, CONV_F - 1, 2 * D_FF), f32))
    p_states, s_states = [], []
    big = None
    for l in range(DEPTH):
        ln = emb_ln if l == 0 else None
        xm, st_m = _block_layer(xm, lws[l], zero_st, CH, N_META, ln=ln)
        xp, st_p = _long_layer(xp, lws[l], st_m, nb, seq, ln=ln)
        samp_st = (state_mlstm_C, state_mlstm_n[l], state_mlstm_m[l], state_gdn_S,
                   state_gdn_conv[l], state_ffn_conv)
        xs, st_s = _block_layer(xs, lws[l], samp_st, sblk, ls, layer=l, prev=big, ln=ln)
        big = (st_s[0], st_s[3])
        p_states.append(st_p)
        s_states.append(st_s)

    stack = lambda states, i: jnp.stack([s[i] for s in states], axis=0)
    y_prompt = xp.reshape(nb, seq, D_MODEL)
    y_sample = xs.reshape(ns, sblk, D_MODEL)[:, sblk - ls:]
    return (y_prompt, y_sample,
            stack(p_states, 0), stack(p_states, 1), stack(p_states, 2), stack(p_states, 3),
            stack(p_states, 4), stack(p_states, 5),
            big[0], stack(s_states, 1), stack(s_states, 2), big[1],
            stack(s_states, 4), stack(s_states, 5))
```

```python
import functools

import jax
import jax.numpy as jnp
from jax import lax
from jax.experimental import pallas as pl
from jax.experimental.pallas import tpu as pltpu

f32 = jnp.float32
bf16 = jnp.bfloat16

D_MODEL = 1024
N_META = 16
NH = 4
DH = 128
WB = NH * DH
CONV_B = 4
D_FF = 2816
CONV_F = 3
DEPTH = 2
ALPHA = (2 * DEPTH) ** 0.25
LN_EPS = 1e-5
NORM_EPS = 1e-6
QSCALE = DH ** -0.5

A_Q = 0
A_I = 4 * WB
B_QKV = A_I + 2 * NH
B_Z = B_QKV + 3 * WB
B_BETA = B_Z + WB
G_MERGE = B_BETA + 2 * NH

C_QA, C_KA, C_VA, C_OA = 0, WB, 2 * WB, 3 * WB
C_Z = 4 * WB
C_G = 5 * WB
C_QKV = C_G + 128
N_MIX = C_QKV + 3 * WB

CH = 64
SUB = 16
B_UNROLL = 2
A_GROUP = 8
HDR = 8
NEG = -1e30
FT = 256
NF = D_FF // FT
FFN_TM_BLOCK = 256
FFN_TM = 512
MIX_T = 512
PROJ_TM = 512
VMEM_LIMIT = 56 * 1024 * 1024


def _cparams(sem):
    return pltpu.CompilerParams(dimension_semantics=sem, vmem_limit_bytes=VMEM_LIMIT)


def _dot(a, b):
    return jnp.dot(a, b, preferred_element_type=f32)


def _dot_nt(a, b):
    return lax.dot_general(a, b, (((1,), (1,)), ((), ())), preferred_element_type=f32)


def _dot_tn(a, b):
    return lax.dot_general(a, b, (((0,), (0,)), ((), ())), preferred_element_type=f32)


def _sel_dot(sel, x):
    hi = x.astype(bf16)
    r1 = x - hi.astype(f32)
    mid = r1.astype(bf16)
    lo = (r1 - mid.astype(f32)).astype(bf16)
    return (_dot(sel, hi) + _dot(sel, mid)) + _dot(sel, lo)


def _softplus(x):
    return jnp.maximum(x, 0.0) + jnp.log1p(jnp.exp(-jnp.abs(x)))


def _sigmoid(x):
    return 0.5 * jnp.tanh(0.5 * x) + 0.5


def _silu(x):
    return x * _sigmoid(x)


def _ln_rows(x, g, b):
    mu = jnp.mean(x, axis=-1, keepdims=True)
    xc = x - mu
    var = jnp.mean(xc * xc, axis=-1, keepdims=True)
    return xc * lax.rsqrt(var + LN_EPS) * g + b


def _conv_rows(u, h8, cw):
    width = cw.shape[0]
    row = lax.broadcasted_iota(jnp.int32, (HDR, u.shape[1]), 0) if h8 is not None else None
    acc = None
    for j in range(width):
        sh = width - 1 - j
        term = u if sh == 0 else pltpu.roll(u, sh, 0)
        if h8 is not None and sh > 0:
            head = term[0:HDR]
            for r in range(sh):
                head = jnp.where(row == r, h8[HDR - sh + r:HDR - sh + r + 1], head)
            term = jnp.concatenate([head, term[HDR:]], axis=0)
        term = term * cw[j:j + 1]
        acc = term if acc is None else acc + term
    return acc


def _inproj_kernel(long_mode, with_ln, tm, nmt, blk, valid, *refs):
    if with_ln:
        g_ref, b_ref, xln_ref = refs[4], refs[5], refs[6]
        refs = refs[:4] + refs[7:]
    if long_mode:
        x_ref, w_ref, cw_ref, hist_ref, o_ref, tail_ref, carry_ref = refs

        @pl.when(pl.program_id(0) % nmt == 0)
        def _():
            carry_ref[...] = hist_ref[0]
    else:
        x_ref, w_ref, cw_ref, e_ref, o_ref, tail_ref = refs
        row = lax.broadcasted_iota(jnp.int32, (tm, 2 * DH), 0)
        vm = (row & (blk - 1)) >= (blk - valid)
    if with_ln:
        xln = _ln_rows(x_ref[...], g_ref[...], b_ref[...])
        xln_ref[...] = xln
        xb = xln.astype(bf16)
    else:
        xb = x_ref[...].astype(bf16)
    nstep = 3 * NH // 2
    cuts = [C_QA + WB] + [C_KA + (C_QKV - C_KA) * (i + 1) // (nstep - 1) // 128 * 128 for i in range(nstep - 1)]
    cuts[-1] = C_QKV
    for cb in range(nstep):
        cols = slice(cb * 2 * DH, (cb + 1) * 2 * DH)
        u = _dot(xb, w_ref[:, C_QKV + cb * 2 * DH:C_QKV + (cb + 1) * 2 * DH])
        lo = C_QA if cb == 0 else cuts[cb - 1]
        plain = _dot(xb, w_ref[:, lo:cuts[cb]])
        o_ref[:, lo:cuts[cb]] = plain * QSCALE if cb == 0 else plain
        if long_mode:
            h8 = carry_ref[:, cols]
            t8 = u[tm - HDR:tm]
            carry_ref[:, cols] = t8
            tail_ref[0, :, cols] = t8
        else:
            h8 = None
            u = jnp.where(vm, u, e_ref[:, cols])
            tail_ref[:, cols] = u
        c = _silu(_conv_rows(u, h8, cw_ref[:, cols]))
        for half in range(2):
            ch = c[:, half * DH:(half + 1) * DH]
            if cb < NH:
                ch = ch * lax.rsqrt(jnp.sum(ch * ch, axis=1, keepdims=True) + NORM_EPS)
            if cb < NH // 2:
                ch = ch * QSCALE
            c0 = C_QKV + (2 * cb + half) * DH
            o_ref[:, c0:c0 + DH] = ch


def _inproj(x, w_mix, layer, cw, hist=None, e=None, nb=1, blk=CH, valid=CH, ln=None):
    m = x.shape[0]
    long_mode = hist is not None
    seq = m // nb
    tm = min(seq, PROJ_TM)
    nmt = seq // tm
    rows = lambda w: pl.BlockSpec((tm, w), lambda i: (i, 0))
    const = lambda shape: pl.BlockSpec(shape, lambda i: (0,) * len(shape))
    in_specs = [rows(D_MODEL), pl.BlockSpec((None, D_MODEL, N_MIX), lambda i: (layer, 0, 0)),
                const((CONV_B, 3 * WB))]
    if long_mode:
        in_specs += [const((1, HDR, 3 * WB))]
        tail_spec = pl.BlockSpec((1, HDR, 3 * WB), lambda i: (i, 0, 0))
        tail_shape = jax.ShapeDtypeStruct((m // tm, HDR, 3 * WB), f32)
        scratch = [pltpu.VMEM((HDR, 3 * WB), f32)]
        extra = hist
    else:
        in_specs += [rows(3 * WB)]
        tail_spec = rows(3 * WB)
        tail_shape = jax.ShapeDtypeStruct((m, 3 * WB), f32)
        scratch = []
        extra = e
    args = [x, w_mix, cw, extra]
    out_specs = [rows(N_MIX), tail_spec]
    out_shape = [jax.ShapeDtypeStruct((m, N_MIX), f32), tail_shape]
    if ln is not None:
        in_specs += [const((1, D_MODEL))] * 2
        args += list(ln)
        out_specs = [rows(D_MODEL)] + out_specs
        out_shape = [jax.ShapeDtypeStruct((m, D_MODEL), f32)] + out_shape
    res = pl.pallas_call(
        functools.partial(_inproj_kernel, long_mode, ln is not None, tm, nmt, blk, valid),
        grid=(m // tm,),
        in_specs=in_specs,
        out_specs=out_specs,
        out_shape=out_shape,
        scratch_shapes=scratch,
        compiler_params=_cparams(("arbitrary",)),
        name="mixer_inproj",
    )(*args)
    return res if ln is None else (res[1], res[2], res[0])


def _chunk_consts(blk, valid):
    lg = blk.bit_length() - 1
    r = lax.broadcasted_iota(jnp.int32, (CH, CH), 0)
    c = lax.broadcasted_iota(jnp.int32, (CH, CH), 1)
    same = (r >> lg) == (c >> lg)
    tri = jnp.logical_and(same, c <= r)
    stri = jnp.logical_and(same, c < r)
    sel = lambda cond: jnp.where(cond, 1.0, 0.0).astype(bf16)
    tri_m = sel(tri)
    blk_m = sel(same)
    re = lax.broadcasted_iota(jnp.int32, (CH, 8), 0)
    ce = lax.broadcasted_iota(jnp.int32, (CH, 8), 1)
    expand_m = sel(ce == (re >> lg))
    rs = lax.broadcasted_iota(jnp.int32, (8, CH), 0)
    cs = lax.broadcasted_iota(jnp.int32, (8, CH), 1)
    rowsel_m = sel(cs == (rs << lg) + (blk - 1))
    rv = lax.broadcasted_iota(jnp.int32, (CH, 128), 0)
    valid_m = (rv & (blk - 1)) >= (blk - valid)
    lgs = min(SUB, blk).bit_length() - 1
    same_sub = (r >> lgs) == (c >> lgs)
    eye = jnp.where(r == c, 1.0, 0.0)
    return dict(tri=tri, stri=stri, tri_m=tri_m, blk_m=blk_m, same_sub=same_sub, eye=eye,
                expand_m=expand_m, rowsel_m=rowsel_m, valid_m=valid_m)


def _block_max(x, blk):
    if blk == CH:
        return jnp.broadcast_to(jnp.max(x, axis=0, keepdims=True), x.shape)
    x3 = x.reshape(CH // blk, blk, 128)
    return jnp.broadcast_to(jnp.max(x3, axis=1, keepdims=True), x3.shape).reshape(CH, 128)


def _mix_scratch(nc, nseq):
    return [pltpu.VMEM((nc, 6, CH, 128), f32),
            pltpu.VMEM((nc, NH, CH, DH), f32),
            pltpu.VMEM((nc, nseq * NH, DH, DH), f32),
            pltpu.VMEM((nc, nseq, 8, DH), f32),
            pltpu.VMEM((nc, nseq * NH, DH, DH), bf16),
            pltpu.VMEM((nc, nseq * NH, DH, DH), f32),
            pltpu.VMEM((nc, NH, CH, DH), bf16),
            pltpu.VMEM((nc, NH, CH, DH), f32)]


def _phase_a(blk, valid, k, chunks, par, sc):
    nseq = CH // blk
    gbias, alog = par[0], par[1]
    t_ref, numl_ref, kv_ref, nv_ref, m1_ref, m2_ref, qeff_ref, o2_ref = sc
    tri, stri = k["tri"], k["stri"]
    seqs = [slice(i * blk, (i + 1) * blk) for i in range(nseq)]
    nch = len(chunks)
    units = [(c, h) for c in range(nch) for h in range(NH)]
    lane = lax.broadcasted_iota(jnp.int32, (CH, 128), 1)
    cum_lane = jnp.logical_or(jnp.logical_and(lane >= 4, lane < 8), lane >= 12)

    gt = []
    for ci, pr in chunks:
        graw = pr(C_G, 128) + gbias
        g = jnp.where(lane < 4, graw,
                      jnp.where(lane < 8, -_softplus(-graw),
                                jnp.where(lane < 12, _sigmoid(graw),
                                          jnp.where(lane < 16, -jnp.exp(alog) * _softplus(graw), 0.0))))
        if valid < blk:
            g = jnp.where(k["valid_m"], g, jnp.where(lane < 4, NEG, 0.0))
        gt.append(g)
    cs = [_sel_dot(k["tri_m"], g) for g in gt]
    if nseq == 1:
        tot = [jnp.broadcast_to(x[CH - 1:CH, :], (CH, 128)) for x in cs]
    else:
        tot = [_sel_dot(k["blk_m"], g) for g in gt]
    mix_t = [jnp.transpose(jnp.where(cum_lane, cs[c], gt[c])) for c in range(nch)]
    b_all = [pltpu.roll(x, 124, 1) for x in cs]
    btot_all = [pltpu.roll(x, 124, 1) for x in tot]
    wlog = [btot_all[c] - b_all[c] + gt[c] for c in range(nch)]
    mloc_all = [_block_max(x, blk) for x in wlog]
    wsrc_all = [jnp.exp(wlog[c] - mloc_all[c]) for c in range(nch)]
    expg_all = [jnp.exp(x) for x in cs]
    expdiff_all = [jnp.exp(tot[c] - cs[c]) for c in range(nch)]

    qa, ka, va, qg, kgf, vg = {}, {}, {}, {}, {}, {}
    for c, h in units:
        _, pr = chunks[c]
        qa[c, h] = pr(C_QA + h * DH, DH).astype(bf16)
        ka[c, h] = pr(C_KA + h * DH, DH)
        va[c, h] = pr(C_VA + h * DH, DH).astype(bf16)
        qg[c, h] = pr(C_QKV + h * DH, DH)
        kgf[c, h] = pr(C_QKV + WB + h * DH, DH)
        vg[c, h] = pr(C_QKV + 2 * WB + h * DH, DH)
    qgb = {u: qg[u].astype(bf16) for u in units}
    kg = {u: kgf[u].astype(bf16) for u in units}

    sq = {u: _dot_nt(qa[u], ka[u].astype(bf16)) for u in units}
    kk = {u: _dot_nt(kg[u], kg[u]) for u in units}
    qk = {u: _dot_nt(qgb[u], kg[u]) for u in units}

    dm = {(c, h): jnp.where(tri, cs[c][:, 4 + h:5 + h] - mix_t[c][4 + h:5 + h, :] + mix_t[c][h:h + 1, :], NEG)
          for c, h in units}
    dmax = {u: jnp.max(dm[u], axis=1, keepdims=True) for u in units}
    s = {u: sq[u] * jnp.exp(dm[u] - dmax[u]) for u in units}
    denl = {u: jnp.sum(s[u], axis=1, keepdims=True) for u in units}
    sloc = {u: s[u].astype(bf16) for u in units}
    kwl = {(c, h): ka[c, h] * wsrc_all[c][:, h:h + 1] for c, h in units}
    for c in range(nch):
        ci = chunks[c][0]
        dmax_all = jnp.zeros((CH, 128), f32)
        denl_all = jnp.zeros((CH, 128), f32)
        for h in range(NH):
            dmax_all = jnp.where(lane == h, dmax[c, h], dmax_all)
            denl_all = jnp.where(lane == h, denl[c, h], denl_all)
        t_ref[ci, 0] = gt[c]
        t_ref[ci, 1] = b_all[c]
        t_ref[ci, 2] = btot_all[c]
        t_ref[ci, 3] = dmax_all
        t_ref[ci, 4] = denl_all
        t_ref[ci, 5] = mloc_all[c]

    xs, pw, qkd, kd = {}, {}, {}, {}
    for c, h in units:
        beta = gt[c][:, 8 + h:9 + h]
        decay = jnp.exp(jnp.where(tri, cs[c][:, 12 + h:13 + h] - mix_t[c][12 + h:13 + h, :], NEG))
        pw[c, h] = jnp.where(stri, beta * kk[c, h] * decay, 0.0)
        xs[c, h] = jnp.concatenate([beta * vg[c, h], (beta * expg_all[c][:, 12 + h:13 + h]) * kgf[c, h]], axis=1)
        qkd[c, h] = (qk[c, h] * decay).astype(bf16)
        kd[c, h] = (kgf[c, h] * expdiff_all[c][:, 12 + h:13 + h]).astype(bf16)

    for c, h in units:
        ci = chunks[c][0]
        numl_ref[ci, h] = _dot(sloc[c, h], va[c, h])
        kwb = kwl[c, h].astype(bf16)
        for i, sl in enumerate(seqs):
            kv_ref[ci, i * NH + h] = _dot_tn(kwb[sl], va[c, h][sl])
            nv_ref[ci, i, h:h + 1, :] = jnp.sum(kwl[c, h][sl], axis=0, keepdims=True)

    sub = min(SUB, blk)
    a_d = {u: jnp.where(k["same_sub"], pw[u], 0.0) for u in units}
    tinv = {u: k["eye"] - a_d[u] for u in units}
    pcur = {u: -a_d[u] for u in units}
    for it in range(1, sub.bit_length() - 1):
        pb = {u: pcur[u].astype(bf16) for u in units}
        pcur = {u: _dot(pb[u], pb[u]) for u in units}
        tinv = {u: tinv[u] + _dot(pcur[u].astype(bf16), tinv[u].astype(bf16)) for u in units}
    tb = {u: tinv[u].astype(bf16) for u in units}
    if sub == blk:
        xs = {u: _dot(tb[u], xs[u].astype(bf16)) for u in units}
    else:
        a_off = {u: (pw[u] - a_d[u]).astype(bf16) for u in units}
        nsub = CH // sub
        done = {u: [] for u in units}
        for j in range(nsub):
            rows = slice(j * sub, (j + 1) * sub)
            pad = lambda parts, n: parts + ([jnp.zeros((n, 2 * DH), f32)] if n else [])
            v = {u: xs[u][rows] for u in units}
            if j > 0:
                xcat = {u: jnp.concatenate(pad(done[u], CH - j * sub), axis=0).astype(bf16) for u in units}
                v = {u: v[u] - _dot(a_off[u][rows], xcat[u]) for u in units}
            vcat = {u: jnp.concatenate(pad(pad([], j * sub) + [v[u]], CH - (j + 1) * sub), axis=0).astype(bf16)
                    for u in units}
            for u in units:
                done[u].append(_dot(tb[u][rows], vcat[u]))
        xs = {u: jnp.concatenate(done[u], axis=0) for u in units}

    xb = {u: xs[u].astype(bf16) for u in units}
    qx = {u: _dot(qkd[u], xb[u]) for u in units}
    for c, h in units:
        ci = chunks[c][0]
        o2_ref[ci, h] = qx[c, h][:, :DH]
        qeff_ref[ci, h] = (expg_all[c][:, 12 + h:13 + h] * qg[c, h] - qx[c, h][:, DH:]).astype(bf16)
        for i, sl in enumerate(seqs):
            mm = _dot_tn(kd[c, h][sl], xb[c, h][sl])
            m2_ref[ci, i * NH + h] = mm[:, :DH]
            m1_ref[ci, i * NH + h] = mm[:, DH:].astype(bf16)


def _phase_b(blk, k, pr, par, st, sc, out, ci):
    nseq = CH // blk
    c_ref, n_ref, m_ref, s_ref = st
    t_ref, numl_ref, kv_ref, nv_ref, m1_ref, m2_ref, qeff_ref, o2_ref = sc
    ha_ref, hb_ref, rows = out
    seqs = [slice(i * blk, (i + 1) * blk) for i in range(nseq)]
    heads = range(NH)
    gt, b_all, btot_all = t_ref[ci, 0], t_ref[ci, 1], t_ref[ci, 2]
    dmax_all, denl_all, mloc_all = t_ref[ci, 3], t_ref[ci, 4], t_ref[ci, 5]

    qa = [pr(C_QA + h * DH, DH) for h in heads]
    qc, ms, oq = [], [], []
    for h in heads:
        qb = qa[h].astype(bf16)
        qc.append([_dot(qb[sl], c_ref[i, h].astype(bf16)) for i, sl in enumerate(seqs)])
    for h in heads:
        qe = qeff_ref[ci, h]
        ms_h, oq_h = [], []
        for i, sl in enumerate(seqs):
            sb = s_ref[i, h].astype(bf16)
            ms_h.append(_dot(m1_ref[ci, i * NH + h], sb))
            oq_h.append(_dot(qe[sl], sb))
        ms.append(ms_h)
        oq.append(oq_h)

    if nseq == 1:
        mprev_all = jnp.broadcast_to(m_ref[0:1, :], (CH, 128))
    else:
        mprev_all = _sel_dot(k["expand_m"], m_ref[...])
    inter = b_all + mprev_all
    mt = jnp.maximum(inter, dmax_all)
    scale_all = jnp.exp(dmax_all - mt)
    winter_all = jnp.exp(inter - mt)
    emt_all = jnp.exp(-mt)
    carry = btot_all + mprev_all
    mnew = jnp.maximum(carry, mloc_all)
    sc2_all = jnp.exp(mloc_all - mnew)
    wold_all = jnp.exp(carry - mnew)
    egt_all = jnp.exp(btot_all)
    if nseq == 1:
        m_ref[0:1, :] = mnew[CH - 1:CH, :]
    else:
        m_ref[...] = _sel_dot(k["rowsel_m"], mnew)

    cat = lambda parts: parts[0] if nseq == 1 else jnp.concatenate(parts, axis=0)
    qn = [cat([jnp.sum(qa[h][sl] * n_ref[i, h:h + 1, :], axis=1, keepdims=True) for i, sl in enumerate(seqs)])
          for h in heads]
    hraw = []
    for h in heads:
        scale, winter = scale_all[:, h:h + 1], winter_all[:, h:h + 1]
        num = scale * numl_ref[ci, h] + winter * cat(qc[h])
        den = scale * denl_all[:, h:h + 1] + winter * qn[h]
        hraw.append(num / jnp.maximum(jnp.abs(den), emt_all[:, h:h + 1]))
    o = [cat(oq[h]) + o2_ref[ci, h] for h in heads]

    for h in heads:
        for i in range(nseq):
            last = (i + 1) * blk - 1
            wold = wold_all[last:last + 1, h:h + 1]
            sc2 = sc2_all[last:last + 1, h:h + 1]
            c_ref[i, h] = wold * c_ref[i, h] + sc2 * kv_ref[ci, i * NH + h]
            n_ref[i, h:h + 1, :] = wold * n_ref[i, h:h + 1, :] + sc2 * nv_ref[ci, i, h:h + 1, :]
            s_ref[i, h] = (egt_all[last:last + 1, 8 + h:9 + h] * s_ref[i, h] - ms[h][i]) + m2_ref[ci, i * NH + h]

    for h in heads:
        ha_ref[rows, h * DH:(h + 1) * DH] = hraw[h]
        hb_ref[rows, h * DH:(h + 1) * DH] = o[h]


def _mixer_long_kernel(t, p_ref, c0_ref, n0_ref, m0_ref, s0_ref, gb_ref, al_ref,
                       ha_ref, hb_ref, c_ref, n_ref, m_ref, s_ref, *sc):
    g = pl.program_id(1)

    @pl.when(g == 0)
    def _():
        c_ref[...] = c0_ref[...]
        n_ref[...] = n0_ref[...]
        m_ref[...] = m0_ref[...]
        s_ref[...] = s0_ref[...]

    k = _chunk_consts(CH, CH)
    par = (gb_ref[...], al_ref[...])
    st = (c_ref, n_ref, _M0(m_ref), s_ref)

    def chunk_views(ci):
        rows = pl.ds(pl.multiple_of(ci * CH, CH), CH)
        return ci, (lambda c0, n: p_ref[rows, c0:c0 + n])

    def body_a(j, carry):
        _phase_a(CH, CH, k, [chunk_views(A_GROUP * j + c) for c in range(A_GROUP)], par, sc)
        return carry

    def body_b(j, carry):
        for c in range(B_UNROLL):
            ci = B_UNROLL * j + c
            rows = pl.ds(pl.multiple_of(ci * CH, CH), CH)
            _phase_b(CH, k, lambda c0, n, rows=rows: p_ref[rows, c0:c0 + n], par, st, sc, (ha_ref, hb_ref, rows), ci)
        return carry

    lax.fori_loop(0, t // (CH * A_GROUP), body_a, 0)
    lax.fori_loop(0, t // (CH * B_UNROLL), body_b, 0)


class _M0:
    def __init__(self, ref):
        self.ref = ref

    def __getitem__(self, idx):
        return self.ref[0] if idx is Ellipsis else self.ref[(0,) + idx]

    def __setitem__(self, idx, val):
        if idx is Ellipsis:
            self.ref[0] = val
        else:
            self.ref[(0,) + idx] = val


def _mixer_long(p, c0, n0, m0, s0, gb, al, nb, seq, t):
    nt = seq // t
    row_spec = lambda w: pl.BlockSpec((t, w), lambda b, g: (b * nt + g, 0))
    const = lambda shape: pl.BlockSpec(shape, lambda b, g: (0,) * len(shape))
    perb = lambda shape: pl.BlockSpec((1,) + shape, lambda b, g: (b,) + (0,) * len(shape))
    return pl.pallas_call(
        functools.partial(_mixer_long_kernel, t),
        grid=(nb, nt),
        in_specs=[row_spec(N_MIX),
                  const((1, NH, DH, DH)), const((1, NH, DH)), const((1, 8, 128)), const((1, NH, DH, DH)),
                  const((1, 128)), const((1, 128))],
        out_specs=[row_spec(WB), row_spec(WB),
                   perb((NH, DH, DH)), perb((NH, DH)), perb((8, 128)), perb((NH, DH, DH))],
        out_shape=[jax.ShapeDtypeStruct((nb * seq, WB), f32), jax.ShapeDtypeStruct((nb * seq, WB), f32),
                   jax.ShapeDtypeStruct((nb, NH, DH, DH), f32), jax.ShapeDtypeStruct((nb, NH, DH), f32),
                   jax.ShapeDtypeStruct((nb, 8, 128), f32), jax.ShapeDtypeStruct((nb, NH, DH, DH), f32)],
        scratch_shapes=_mix_scratch(t // CH, 1),
        compiler_params=_cparams(("arbitrary", "arbitrary")),
        name="mixer_long",
    )(p, c0, n0, m0, s0, gb, al)


def _mixer_block_kernel(blk, valid, n_alias, first_of, p_ref, c0_ref, n0_ref, m0_ref, s0_ref, gb_ref, al_ref,
                        *rest):
    ha_ref, hb_ref, c_ref, n_ref, m_ref, s_ref = rest[n_alias:n_alias + 6]
    sc = rest[n_alias + 6:]
    if first_of is not None:
        layer, depth = first_of
        for other in range(depth):
            if other != layer:
                c_ref[other] = jnp.zeros(c_ref.shape[1:], f32)
                s_ref[other] = jnp.zeros(s_ref.shape[1:], f32)
        c_ref, s_ref = c_ref.at[layer], s_ref.at[layer]
    c_ref[...] = c0_ref[...]
    n_ref[...] = n0_ref[...]
    m_ref[...] = m0_ref[...]
    s_ref[...] = s0_ref[...]
    k = _chunk_consts(blk, valid)
    par = (gb_ref[...], al_ref[...])
    pr = lambda c0, n: p_ref[:, c0:c0 + n]
    _phase_a(blk, valid, k, [(0, pr)], par, sc)
    _phase_b(blk, k, pr, par, (c_ref, n_ref, _M0(m_ref), s_ref), sc, (ha_ref, hb_ref, slice(0, CH)), 0)


def _mixer_block(p, c0, n0, m0, s0, gb, al, blk, valid, layer=None, prev=None):
    ng = p.shape[0] // CH
    nseq = CH // blk
    row_spec = lambda w: pl.BlockSpec((CH, w), lambda g: (g, 0))
    const = lambda shape: pl.BlockSpec(shape, lambda g: (0,) * len(shape))
    perg = lambda shape: pl.BlockSpec(shape, lambda g: (g,) + (0,) * (len(shape) - 1))
    big_shape = (ng * nseq, NH, DH, DH)
    first_of = None
    if layer is None:
        big = big_out = perg((nseq, NH, DH, DH))
    else:
        depth = c0.shape[0]
        big = big_out = pl.BlockSpec((None, nseq, NH, DH, DH), lambda g: (layer, g, 0, 0, 0))
        big_shape = (depth,) + big_shape
        if prev is None:
            first_of = (layer, depth)
            big_out = pl.BlockSpec((depth, nseq, NH, DH, DH), lambda g: (0, g, 0, 0, 0))
    aliased = () if prev is None else tuple(prev)
    n_in = 7
    return pl.pallas_call(
        functools.partial(_mixer_block_kernel, blk, valid, len(aliased), first_of),
        grid=(ng,),
        in_specs=[row_spec(N_MIX), big, perg((nseq, NH, DH)), perg((1, 8, 128)), big,
                  const((1, 128)), const((1, 128))]
                 + [pl.BlockSpec(memory_space=pl.ANY)] * len(aliased),
        out_specs=[row_spec(WB), row_spec(WB), big_out, perg((nseq, NH, DH)), perg((1, 8, 128)), big_out],
        out_shape=[jax.ShapeDtypeStruct((ng * CH, WB), f32), jax.ShapeDtypeStruct((ng * CH, WB), f32),
                   jax.ShapeDtypeStruct(big_shape, f32), jax.ShapeDtypeStruct((ng * nseq, NH, DH), f32),
                   jax.ShapeDtypeStruct((ng, 8, 128), f32), jax.ShapeDtypeStruct(big_shape, f32)],
        input_output_aliases={n_in: 2, n_in + 1: 5} if aliased else {},
        scratch_shapes=_mix_scratch(1, nseq),
        compiler_params=_cparams(("arbitrary",)),
        name="mixer_block",
    )(p, c0, n0, m0, s0, gb, al, *aliased)


def _outproj_kernel(x_ref, ha_ref, hb_ref, po_ref, pz_ref, anw_ref, bnw_ref,
                    wm_ref, wpa_ref, wpb_ref, wo_ref, g_ref, b_ref, o_ref):
    x = x_ref[...]
    mg = _sigmoid(_dot(x.astype(bf16), wm_ref[...]))
    cols = [slice(h * DH, (h + 1) * DH) for h in range(NH)]
    hr = [ha_ref[:, c] for c in cols]
    ob = [hb_ref[:, c] for c in cols]
    mu = [jnp.mean(v, axis=1, keepdims=True) for v in hr]
    osq = [jnp.mean(v * v, axis=1, keepdims=True) for v in ob]
    hc = [hr[h] - mu[h] for h in range(NH)]
    var = [jnp.mean(v * v, axis=1, keepdims=True) for v in hc]
    ha = [_sigmoid(po_ref[:, cols[h]]) * (hc[h] * lax.rsqrt(var[h] + NORM_EPS) * anw_ref[:, cols[h]])
          for h in range(NH)]
    hb = [(ob[h] * lax.rsqrt(osq[h] + NORM_EPS) * bnw_ref[...]) * _silu(pz_ref[:, cols[h]]) for h in range(NH)]
    ya = _dot(jnp.concatenate(ha, axis=1).astype(bf16), wpa_ref[...])
    yb = _dot(jnp.concatenate(hb, axis=1).astype(bf16), wpb_ref[...])
    y = mg[:, :D_MODEL] * ya + mg[:, D_MODEL:] * yb
    mix = _dot(y.astype(bf16), wo_ref[...])
    o_ref[...] = _ln_rows(ALPHA * x + mix, g_ref[...], b_ref[...])


def _outproj(x, ha, hb, p, anw, bnw, w_merge, layer, w_pa, w_pb, w_out, g, b):
    m = x.shape[0]
    tm = min(m, PROJ_TM)
    row_spec = lambda w: pl.BlockSpec((tm, w), lambda i: (i, 0))
    pcol = lambda c0: pl.BlockSpec((tm, WB), lambda i: (i, c0 // WB))
    const = lambda shape: pl.BlockSpec(shape, lambda i: (0, 0))
    return pl.pallas_call(
        _outproj_kernel,
        grid=(m // tm,),
        in_specs=[row_spec(D_MODEL), row_spec(WB), row_spec(WB), pcol(C_OA), pcol(C_Z),
                  const((1, WB)), const((1, DH)),
                  pl.BlockSpec((None, D_MODEL, 2 * D_MODEL), lambda i: (layer, 0, 0)),
                  const((WB, D_MODEL)), const((WB, D_MODEL)),
                  const((D_MODEL, D_MODEL)), const((1, D_MODEL)), const((1, D_MODEL))],
        out_specs=row_spec(D_MODEL),
        out_shape=jax.ShapeDtypeStruct((m, D_MODEL), f32),
        compiler_params=_cparams(("arbitrary",)),
        name="merge_outproj_ln",
    )(x, ha, hb, p, p, anw, bnw, w_merge, w_pa, w_pb, w_out, g, b)


def _ffn2_kernel(long_mode, tm, nmt, blk, valid, *refs):
    if long_mode:
        (x_ref, wa_ref, wb_ref, cwa_ref, cwb_ref, wd_ref, g_ref, b_ref, ha_ref, hb_ref,
         o_ref, sa_ref, sb_ref, hh_ref, ca_ref, cb_ref) = refs

        @pl.when(pl.program_id(0) % nmt == 0)
        def _():
            ca_ref[...] = ha_ref[0]
            cb_ref[...] = hb_ref[0]
    else:
        (x_ref, wa_ref, wb_ref, cwa_ref, cwb_ref, wd_ref, g_ref, b_ref, ea_ref, eb_ref,
         o_ref, sa_ref, sb_ref, hh_ref) = refs
        nsq = tm // blk
        r0 = blk - valid - (CONV_F - 1)
        row3 = lax.broadcasted_iota(jnp.int32, (nsq, blk, FT), 1)

        def merge_history(u, e_ref, s_ref, cols):
            u3 = u.reshape(nsq, blk, FT)
            e3 = e_ref[:, :, cols]
            for r in range(CONV_F - 1):
                u3 = jnp.where(row3 == r0 + r, e3[:, r:r + 1, :], u3)
            s_ref[:, :, cols] = u3[:, blk - (CONV_F - 1):, :]
            return u3.reshape(tm, FT)
    xb = x_ref[...].astype(bf16)
    for cb in range(NF):
        cols = slice(cb * FT, (cb + 1) * FT)
        ua = _dot(xb, wa_ref[:, cols])
        ub = _dot(xb, wb_ref[:, cols])
        if long_mode:
            h8a, h8b = ca_ref[:, cols], cb_ref[:, cols]
            ta, tb = ua[tm - HDR:tm], ub[tm - HDR:tm]
            ca_ref[:, cols] = ta
            cb_ref[:, cols] = tb
            sa_ref[0, :, cols] = ta
            sb_ref[0, :, cols] = tb
        else:
            h8a = h8b = None
            ua = merge_history(ua, ea_ref, sa_ref, cols)
            ub = merge_history(ub, eb_ref, sb_ref, cols)
        hh = _silu(_conv_rows(ua, h8a, cwa_ref[:, cols])) * _conv_rows(ub, h8b, cwb_ref[:, cols])
        hh_ref[:, cols] = hh.astype(bf16)
    out = _dot(hh_ref[...], wd_ref[...])
    o_ref[...] = _ln_rows(ALPHA * x_ref[...] + out, g_ref[...], b_ref[...])


def _ffn2(x, w_up, cw, w_down, layer, g, b, hist_a=None, hist_b=None,
          e_ab=None, nb=1, blk=CH, valid=CH):
    m = x.shape[0]
    long_mode = hist_a is not None
    seq = m // nb
    tm = min(seq, FFN_TM if long_mode else FFN_TM_BLOCK)
    nmt = seq // tm
    rows = lambda w: pl.BlockSpec((tm, w), lambda i: (i, 0))
    const = lambda shape: pl.BlockSpec(shape, lambda i: (0,) * len(shape))
    lsp = lambda shape, j: pl.BlockSpec((None,) + shape, lambda i: (layer, 0, j))
    in_specs = [rows(D_MODEL), lsp((D_MODEL, D_FF), 0), lsp((D_MODEL, D_FF), 1), lsp((CONV_F, D_FF), 0),
                lsp((CONV_F, D_FF), 1), lsp((D_FF, D_MODEL), 0), const((1, D_MODEL)), const((1, D_MODEL))]
    scratch = [pltpu.VMEM((tm, D_FF), bf16)]
    if long_mode:
        in_specs += [const((1, HDR, D_FF))] * 2
        st_spec = pl.BlockSpec((1, HDR, D_FF), lambda i: (i, 0, 0))
        st_shape = jax.ShapeDtypeStruct((m // tm, HDR, D_FF), f32)
        scratch += [pltpu.VMEM((HDR, D_FF), f32), pltpu.VMEM((HDR, D_FF), f32)]
        extra = (hist_a, hist_b)
    else:
        nsq = tm // blk
        if e_ab.ndim == 4:
            e_spec = lambda j: pl.BlockSpec((None, nsq, CONV_F - 1, D_FF), lambda i: (layer, i, 0, j))
        else:
            e_spec = lambda j: pl.BlockSpec((nsq, CONV_F - 1, D_FF), lambda i: (i, 0, j))
        in_specs += [e_spec(0), e_spec(1)]
        st_spec = pl.BlockSpec((nsq, CONV_F - 1, D_FF), lambda i: (i, 0, 0))
        st_shape = jax.ShapeDtypeStruct((m // blk, CONV_F - 1, D_FF), f32)
        extra = (e_ab, e_ab)
    return pl.pallas_call(
        functools.partial(_ffn2_kernel, long_mode, tm, nmt, blk, valid),
        grid=(m // tm,),
        in_specs=in_specs,
        out_specs=[rows(D_MODEL), st_spec, st_spec],
        out_shape=[jax.ShapeDtypeStruct((m, D_MODEL), f32), st_shape, st_shape],
        scratch_shapes=scratch,
        compiler_params=_cparams(("arbitrary",)),
        name="conv_ffn_ln",
    )(x, w_up, w_up, cw, cw, w_down, g, b, *extra)


def _mix_weights(w_in):
    w = w_in.astype(bf16)
    zc = jnp.zeros(w.shape[:2] + (128 - 4 * NH,), bf16)
    w_mix = jnp.concatenate([w[..., A_Q:A_I], w[..., B_Z:B_BETA], w[..., A_I:B_QKV], w[..., B_BETA:G_MERGE], zc,
                             w[..., B_QKV:B_Z]], axis=-1)
    return w_mix, w[..., G_MERGE:]


def _layer_weights(l, w_mix, w_merge, mlstm_gate_bias, mlstm_norm_w, gdn_conv_w, gdn_A_log, gdn_dt_bias, gdn_norm_w,
                   w_branch_a, w_branch_b, w_out, ln1_g, ln1_b, w_up, ffn_conv_w, w_down, ln2_g, ln2_b):
    z4 = jnp.zeros((NH,), f32)
    gb = jnp.concatenate([mlstm_gate_bias[l], z4, gdn_dt_bias[l], jnp.zeros((128 - 4 * NH,), f32)])[None]
    al = jnp.concatenate([z4, z4, z4, gdn_A_log[l], jnp.zeros((128 - 4 * NH,), f32)])[None]
    return dict(
        layer=l, w_mix=w_mix, w_merge=w_merge, gb=gb, al=al,
        anw=mlstm_norm_w[l][None], bnw=gdn_norm_w[l][None], cw=gdn_conv_w[l],
        w_pa=w_branch_a[l].astype(bf16), w_pb=w_branch_b[l].astype(bf16), w_out=w_out[l].astype(bf16),
        ln1_g=ln1_g[l][None], ln1_b=ln1_b[l][None],
        w_up=w_up, cw_f=ffn_conv_w, w_down=w_down, ln2_g=ln2_g[l][None], ln2_b=ln2_b[l][None])


def _block_layer(x, lw, st, blk, valid, layer=None, prev=None, ln=None):
    c0, n0, m0, s0, gbuf, fbuf = st
    nseq_tot = n0.shape[0]
    nseq = CH // blk
    ng = nseq_tot // nseq
    front = blk - valid
    m0p = jnp.pad(m0.reshape(ng, nseq, NH), ((0, 0), (0, 8 - nseq), (0, 128 - NH)))
    e = jnp.pad(gbuf, ((0, 0), (front - (CONV_B - 1), valid), (0, 0))).reshape(nseq_tot * blk, 3 * WB)
    if ln is None:
        p, ext = _inproj(x, lw["w_mix"], lw["layer"], lw["cw"], e=e, blk=blk, valid=valid)
    else:
        p, ext, x = _inproj(x, lw["w_mix"], lw["layer"], lw["cw"], e=e, blk=blk, valid=valid, ln=ln)
    ha, hb, c, n, m, s = _mixer_block(p, c0, n0, m0p, s0, lw["gb"], lw["al"], blk, valid, layer, prev)
    x1 = _outproj(x, ha, hb, p, lw["anw"], lw["bnw"], lw["w_merge"], lw["layer"], lw["w_pa"], lw["w_pb"],
                  lw["w_out"],
                  lw["ln1_g"], lw["ln1_b"])
    x2, ua, ub = _ffn2(x1, lw["w_up"], lw["cw_f"], lw["w_down"], lw["layer"],
                       lw["ln2_g"], lw["ln2_b"], e_ab=fbuf, blk=blk, valid=valid)
    m_new = m[:, :nseq, :NH].reshape(nseq_tot, NH)
    tail = lambda a, w, k: a.reshape(nseq_tot, blk, w)[:, blk - k:]
    gconv = tail(ext, 3 * WB, CONV_B - 1)
    fconv = jnp.concatenate([ua, ub], axis=2)
    return x2, (c, n, m_new, s, gconv, fconv)


def _long_layer(x, lw, st, nb, seq, ln=None):
    c0, n0, m0, s0, gbuf, fbuf = st
    m0p = jnp.pad(m0.reshape(1, 1, NH), ((0, 0), (0, 7), (0, 128 - NH)))
    hist = jnp.pad(gbuf, ((0, 0), (HDR - (CONV_B - 1), 0), (0, 0)))
    hf = jnp.pad(fbuf, ((0, 0), (HDR - (CONV_F - 1), 0), (0, 0)))
    if ln is None:
        p, tails = _inproj(x, lw["w_mix"], lw["layer"], lw["cw"], hist=hist, nb=nb)
    else:
        p, tails, x = _inproj(x, lw["w_mix"], lw["layer"], lw["cw"], hist=hist, nb=nb, ln=ln)
    ha, hb, c, n, m, s = _mixer_long(p, c0, n0, m0p, s0, lw["gb"], lw["al"], nb, seq, MIX_T)
    x1 = _outproj(x, ha, hb, p, lw["anw"], lw["bnw"], lw["w_merge"], lw["layer"], lw["w_pa"], lw["w_pb"],
                  lw["w_out"],
                  lw["ln1_g"], lw["ln1_b"])
    x2, sa, sb = _ffn2(x1, lw["w_up"], lw["cw_f"], lw["w_down"], lw["layer"],
                       lw["ln2_g"], lw["ln2_b"], hist_a=hf[:, :, :D_FF], hist_b=hf[:, :, D_FF:], nb=nb)
    last_tile = lambda a: a.reshape((nb, -1) + a.shape[1:])[:, -1]
    gconv = last_tile(tails)[:, HDR - (CONV_B - 1):]
    fconv = jnp.concatenate([last_tile(sa), last_tile(sb)], axis=2)[:, HDR - (CONV_F - 1):]
    return x2, (c, n, m[:, 0, :NH], s, gconv, fconv)


def kernel(x_prompt, x_sample, state_mlstm_C, state_mlstm_n, state_mlstm_m, state_gdn_S, state_gdn_conv, state_ffn_conv, meta_tokens, ln_emb_g, ln_emb_b, w_in, mlstm_gate_bias, mlstm_norm_w, gdn_conv_w, gdn_A_log, gdn_dt_bias, gdn_norm_w, w_branch_a, w_branch_b, w_out, ln1_g, ln1_b, w_up, ffn_conv_w, w_down, ln2_g, ln2_b):
    nb, seq, _ = x_prompt.shape
    ns, ls, _ = x_sample.shape
    sblk = 8
    assert seq % max(MIX_T, PROJ_TM, FFN_TM) == 0 and ls + CONV_B - 1 <= sblk and ns % (CH // sblk) == 0 and N_META + CONV_B - 1 <= CH
    w_mix, w_merge = _mix_weights(w_in)
    w_up_b16, w_down_b16 = w_up.astype(bf16), w_down.astype(bf16)
    lws =[_layer_weights(l, w_mix, w_merge, mlstm_gate_bias, mlstm_norm_w, gdn_conv_w, gdn_A_log, gdn_dt_bias,
                          gdn_norm_w, w_branch_a, w_branch_b, w_out, ln1_g, ln1_b, w_up_b16, ffn_conv_w,
                          w_down_b16, ln2_g, ln2_b) for l in range(DEPTH)]
    emb_ln = (ln_emb_g[None], ln_emb_b[None])

    xm = jnp.pad(meta_tokens, ((CH - N_META, 0), (0, 0)))
    xs = jnp.pad(x_sample, ((0, 0), (sblk - ls, 0), (0, 0))).reshape(ns * sblk, D_MODEL)
    xp = x_prompt.reshape(nb * seq, D_MODEL)

    zero_st = (jnp.zeros((1, NH, DH, DH), f32), jnp.zeros((1, NH, DH), f32), jnp.zeros((1, NH), f32),
               jnp.zeros((1, NH, DH, DH), f32), jnp.zeros((1, CONV_B - 1, 3 * WB), f32),
               jnp.zeros((1, CONV_F - 1, 2 * D_FF), f32))
    p_states, s_states = [], []
    big = None
    for l in range(DEPTH):
        ln = emb_ln if l == 0 else None
        xm, st_m = _block_layer(xm, lws[l], zero_st, CH, N_META, ln=ln)
        xp, st_p = _long_layer(xp, lws[l], st_m, nb, seq, ln=ln)
        samp_st = (state_mlstm_C, state_mlstm_n[l], state_mlstm_m[l], state_gdn_S,
                   state_gdn_conv[l], state_ffn_conv)
        xs, st_s = _block_layer(xs, lws[l], samp_st, sblk, ls, layer=l, prev=big, ln=ln)
        big = (st_s[0], st_s[3])
        p_states.append(st_p)
        s_states.append(st_s)

    stack = lambda states, i: jnp.stack([s[i] for s in states], axis=0)
    y_prompt = xp.reshape(nb, seq, D_MODEL)
    y_sample = xs.reshape(ns, sblk, D_MODEL)[:, sblk - ls:]
    return (y_prompt, y_sample,
            stack(p_states, 0), stack(p_states, 1), stack(p_states, 2), stack(p_states, 3),
            stack(p_states, 4), stack(p_states, 5),
            big[0], stack(s_states, 1), stack(s_states, 2), big[1],
            stack(s_states, 4), stack(s_states, 5))
```

```python
import functools

import jax
import jax.numpy as jnp
from jax import lax
from jax.experimental import pallas as pl
from jax.experimental.pallas import tpu as pltpu

f32 = jnp.float32
bf16 = jnp.bfloat16

D_MODEL = 1024
N_META = 16
NH = 4
DH = 128
WB = NH * DH
CONV_B = 4
D_FF = 2816
CONV_F = 3
DEPTH = 2
ALPHA = (2 * DEPTH) ** 0.25
LN_EPS = 1e-5
NORM_EPS = 1e-6
QSCALE = DH ** -0.5

A_Q = 0
A_I = 4 * WB
B_QKV = A_I + 2 * NH
B_Z = B_QKV + 3 * WB
B_BETA = B_Z + WB
G_MERGE = B_BETA + 2 * NH

C_QA, C_KA, C_VA, C_OA = 0, WB, 2 * WB, 3 * WB
C_Z = 4 * WB
C_G = 5 * WB
C_QKV = C_G + 128
N_MIX = C_QKV + 3 * WB

CH = 64
SUB = 16
B_UNROLL = 2
A_GROUP = 8
HDR = 8
NEG = -1e30
FT = 256
NF = D_FF // FT
FFN_TM_BLOCK = 512
FFN_TM = 512
MIX_T = 512
PROJ_TM = 512
VMEM_LIMIT = 56 * 1024 * 1024


def _cparams(sem):
    return pltpu.CompilerParams(dimension_semantics=sem, vmem_limit_bytes=VMEM_LIMIT)


def _dot(a, b):
    return jnp.dot(a, b, preferred_element_type=f32)


def _dot_nt(a, b):
    return lax.dot_general(a, b, (((1,), (1,)), ((), ())), preferred_element_type=f32)


def _dot_tn(a, b):
    return lax.dot_general(a, b, (((0,), (0,)), ((), ())), preferred_element_type=f32)


def _sel_dot(sel, x):
    hi = x.astype(bf16)
    r1 = x - hi.astype(f32)
    mid = r1.astype(bf16)
    lo = (r1 - mid.astype(f32)).astype(bf16)
    return (_dot(sel, hi) + _dot(sel, mid)) + _dot(sel, lo)


def _softplus(x):
    return jnp.maximum(x, 0.0) + jnp.log1p(jnp.exp(-jnp.abs(x)))


def _sigmoid(x):
    return 0.5 * jnp.tanh(0.5 * x) + 0.5


def _silu(x):
    t = 0.5 * x
    return t * (jnp.tanh(t) + 1.0)


def _ln_rows(x, g, b):
    mu = jnp.mean(x, axis=-1, keepdims=True)
    xc = x - mu
    var = jnp.mean(xc * xc, axis=-1, keepdims=True)
    return xc * lax.rsqrt(var + LN_EPS) * g + b


def _conv_rows(u, h8, cw):
    width = cw.shape[0]
    row = lax.broadcasted_iota(jnp.int32, (HDR, u.shape[1]), 0) if h8 is not None else None
    acc = None
    for j in range(width):
        sh = width - 1 - j
        term = u if sh == 0 else pltpu.roll(u, sh, 0)
        if h8 is not None and sh > 0:
            head = term[0:HDR]
            for r in range(sh):
                head = jnp.where(row == r, h8[HDR - sh + r:HDR - sh + r + 1], head)
            term = jnp.concatenate([head, term[HDR:]], axis=0)
        term = term * cw[j:j + 1]
        acc = term if acc is None else acc + term
    return acc


def _inproj_kernel(long_mode, with_ln, tm, nmt, blk, valid, *refs):
    if with_ln:
        g_ref, b_ref, xln_ref = refs[4], refs[5], refs[6]
        refs = refs[:4] + refs[7:]
    if long_mode:
        x_ref, w_ref, cw_ref, hist_ref, o_ref, tail_ref, carry_ref = refs

        @pl.when(pl.program_id(0) % nmt == 0)
        def _():
            carry_ref[...] = hist_ref[0]
    else:
        x_ref, w_ref, cw_ref, e_ref, o_ref, tail_ref = refs
        nsq = tm // blk
        r0 = blk - valid - (CONV_B - 1)
        row3 = lax.broadcasted_iota(jnp.int32, (nsq, blk, 2 * DH), 1)
    if with_ln:
        xln = _ln_rows(x_ref[...], g_ref[...], b_ref[...])
        xln_ref[...] = xln
        xb = xln.astype(bf16)
    else:
        xb = x_ref[...].astype(bf16)
    nstep = 3 * NH // 2
    cuts = [C_QA + WB] + [C_KA + (C_QKV - C_KA) * (i + 1) // (nstep - 1) // 128 * 128 for i in range(nstep - 1)]
    cuts[-1] = C_QKV
    for cb in range(nstep):
        cols = slice(cb * 2 * DH, (cb + 1) * 2 * DH)
        u = _dot(xb, w_ref[:, C_QKV + cb * 2 * DH:C_QKV + (cb + 1) * 2 * DH])
        lo = C_QA if cb == 0 else cuts[cb - 1]
        plain = _dot(xb, w_ref[:, lo:cuts[cb]])
        o_ref[:, lo:cuts[cb]] = plain * QSCALE if cb == 0 else plain
        if long_mode:
            h8 = carry_ref[:, cols]
            t8 = u[tm - HDR:tm]
            carry_ref[:, cols] = t8
            tail_ref[0, :, cols] = t8
        else:
            h8 = None
            u3 = u.reshape(nsq, blk, 2 * DH)
            e3 = e_ref[:, :, cols]
            for r in range(CONV_B - 1):
                u3 = jnp.where(row3 == r0 + r, e3[:, r:r + 1, :], u3)
            tail_ref[:, :, cols] = u3[:, blk - (CONV_B - 1):, :]
            u = u3.reshape(tm, 2 * DH)
        c = _silu(_conv_rows(u, h8, cw_ref[:, cols]))
        for half in range(2):
            ch = c[:, half * DH:(half + 1) * DH]
            if cb < NH:
                ch = ch * lax.rsqrt(jnp.sum(ch * ch, axis=1, keepdims=True) + NORM_EPS)
            if cb < NH // 2:
                ch = ch * QSCALE
            c0 = C_QKV + (2 * cb + half) * DH
            o_ref[:, c0:c0 + DH] = ch


def _inproj(x, w_mix, layer, cw, hist=None, e=None, nb=1, blk=CH, valid=CH, ln=None):
    m = x.shape[0]
    long_mode = hist is not None
    seq = m // nb
    tm = min(seq, PROJ_TM)
    nmt = seq // tm
    rows = lambda w: pl.BlockSpec((tm, w), lambda i: (i, 0))
    const = lambda shape: pl.BlockSpec(shape, lambda i: (0,) * len(shape))
    in_specs = [rows(D_MODEL), pl.BlockSpec((None, D_MODEL, N_MIX), lambda i: (layer, 0, 0)),
                const((CONV_B, 3 * WB))]
    if long_mode:
        in_specs += [const((1, HDR, 3 * WB))]
        tail_spec = pl.BlockSpec((1, HDR, 3 * WB), lambda i: (i, 0, 0))
        tail_shape = jax.ShapeDtypeStruct((m // tm, HDR, 3 * WB), f32)
        scratch = [pltpu.VMEM((HDR, 3 * WB), f32)]
        extra = hist
    else:
        nsq = tm // blk
        in_specs += [pl.BlockSpec((nsq, CONV_B - 1, 3 * WB), lambda i: (i, 0, 0))]
        tail_spec = pl.BlockSpec((nsq, CONV_B - 1, 3 * WB), lambda i: (i, 0, 0))
        tail_shape = jax.ShapeDtypeStruct((m // blk, CONV_B - 1, 3 * WB), f32)
        scratch = []
        extra = e
    args = [x, w_mix, cw, extra]
    out_specs = [rows(N_MIX), tail_spec]
    out_shape = [jax.ShapeDtypeStruct((m, N_MIX), f32), tail_shape]
    if ln is not None:
        in_specs += [const((1, D_MODEL))] * 2
        args += list(ln)
        out_specs = [rows(D_MODEL)] + out_specs
        out_shape = [jax.ShapeDtypeStruct((m, D_MODEL), f32)] + out_shape
    res = pl.pallas_call(
        functools.partial(_inproj_kernel, long_mode, ln is not None, tm, nmt, blk, valid),
        grid=(m // tm,),
        in_specs=in_specs,
        out_specs=out_specs,
        out_shape=out_shape,
        scratch_shapes=scratch,
        compiler_params=_cparams(("arbitrary",)),
        name="mixer_inproj",
    )(*args)
    return res if ln is None else (res[1], res[2], res[0])


def _chunk_consts(blk, valid):
    lg = blk.bit_length() - 1
    r = lax.broadcasted_iota(jnp.int32, (CH, CH), 0)
    c = lax.broadcasted_iota(jnp.int32, (CH, CH), 1)
    same = (r >> lg) == (c >> lg)
    tri = jnp.logical_and(same, c <= r)
    stri = jnp.logical_and(same, c < r)
    sel = lambda cond: jnp.where(cond, 1.0, 0.0).astype(bf16)
    tri_m = sel(tri)
    blk_m = sel(same)
    re = lax.broadcasted_iota(jnp.int32, (CH, 8), 0)
    ce = lax.broadcasted_iota(jnp.int32, (CH, 8), 1)
    expand_m = sel(ce == (re >> lg))
    rs = lax.broadcasted_iota(jnp.int32, (8, CH), 0)
    cs = lax.broadcasted_iota(jnp.int32, (8, CH), 1)
    rowsel_m = sel(cs == (rs << lg) + (blk - 1))
    rv = lax.broadcasted_iota(jnp.int32, (CH, 128), 0)
    valid_m = (rv & (blk - 1)) >= (blk - valid)
    lgs = min(SUB, blk).bit_length() - 1
    same_sub = (r >> lgs) == (c >> lgs)
    eye = jnp.where(r == c, 1.0, 0.0)
    return dict(tri=tri, stri=stri, tri_m=tri_m, blk_m=blk_m, same_sub=same_sub, eye=eye,
                expand_m=expand_m, rowsel_m=rowsel_m, valid_m=valid_m)


def _block_max(x, blk):
    if blk == CH:
        return jnp.broadcast_to(jnp.max(x, axis=0, keepdims=True), x.shape)
    x3 = x.reshape(CH // blk, blk, 128)
    return jnp.broadcast_to(jnp.max(x3, axis=1, keepdims=True), x3.shape).reshape(CH, 128)


def _mix_scratch(nc, nseq):
    return [pltpu.VMEM((nc, 6, CH, 128), f32),
            pltpu.VMEM((nc, NH, CH, DH), f32),
            pltpu.VMEM((nc, nseq * NH, DH, DH), f32),
            pltpu.VMEM((nc, nseq, 8, DH), f32),
            pltpu.VMEM((nc, nseq * NH, DH, DH), bf16),
            pltpu.VMEM((nc, nseq * NH, DH, DH), f32),
            pltpu.VMEM((nc, NH, CH, DH), bf16),
            pltpu.VMEM((nc, NH, CH, DH), f32)]


def _phase_a(blk, valid, k, chunks, par, sc):
    nseq = CH // blk
    gbias, alog = par[0], par[1]
    t_ref, numl_ref, kv_ref, nv_ref, m1_ref, m2_ref, qeff_ref, o2_ref = sc
    tri, stri = k["tri"], k["stri"]
    seqs = [slice(i * blk, (i + 1) * blk) for i in range(nseq)]
    nch = len(chunks)
    units = [(c, h) for c in range(nch) for h in range(NH)]
    lane = lax.broadcasted_iota(jnp.int32, (CH, 128), 1)
    cum_lane = jnp.logical_or(jnp.logical_and(lane >= 4, lane < 8), lane >= 12)

    gt = []
    for ci, pr in chunks:
        graw = pr(C_G, 128) + gbias
        g = jnp.where(lane < 4, graw,
                      jnp.where(lane < 8, -_softplus(-graw),
                                jnp.where(lane < 12, _sigmoid(graw),
                                          jnp.where(lane < 16, -jnp.exp(alog) * _softplus(graw), 0.0))))
        if valid < blk:
            g = jnp.where(k["valid_m"], g, jnp.where(lane < 4, NEG, 0.0))
        gt.append(g)
    cs = [_sel_dot(k["tri_m"], g) for g in gt]
    if nseq == 1:
        tot = [jnp.broadcast_to(x[CH - 1:CH, :], (CH, 128)) for x in cs]
    else:
        tot = [_sel_dot(k["blk_m"], g) for g in gt]
    mix_t = [jnp.transpose(jnp.where(cum_lane, cs[c], gt[c])) for c in range(nch)]
    b_all = [pltpu.roll(x, 124, 1) for x in cs]
    btot_all = [pltpu.roll(x, 124, 1) for x in tot]
    wlog = [btot_all[c] - b_all[c] + gt[c] for c in range(nch)]
    mloc_all = [_block_max(x, blk) for x in wlog]
    wsrc_all = [jnp.exp(wlog[c] - mloc_all[c]) for c in range(nch)]
    expg_all = [jnp.exp(x) for x in cs]
    expdiff_all = [jnp.exp(tot[c] - cs[c]) for c in range(nch)]

    qa, ka, va, qg, kgf, vg = {}, {}, {}, {}, {}, {}
    for c, h in units:
        _, pr = chunks[c]
        qa[c, h] = pr(C_QA + h * DH, DH).astype(bf16)
        ka[c, h] = pr(C_KA + h * DH, DH)
        va[c, h] = pr(C_VA + h * DH, DH).astype(bf16)
        qg[c, h] = pr(C_QKV + h * DH, DH)
        kgf[c, h] = pr(C_QKV + WB + h * DH, DH)
        vg[c, h] = pr(C_QKV + 2 * WB + h * DH, DH)
    qgb = {u: qg[u].astype(bf16) for u in units}
    kg = {u: kgf[u].astype(bf16) for u in units}

    sq = {u: _dot_nt(qa[u], ka[u].astype(bf16)) for u in units}
    kk = {u: _dot_nt(kg[u], kg[u]) for u in units}
    qk = {u: _dot_nt(qgb[u], kg[u]) for u in units}

    dm = {(c, h): jnp.where(tri, cs[c][:, 4 + h:5 + h] - mix_t[c][4 + h:5 + h, :] + mix_t[c][h:h + 1, :], NEG)
          for c, h in units}
    dmax = {u: jnp.max(dm[u], axis=1, keepdims=True) for u in units}
    s = {u: sq[u] * jnp.exp(dm[u] - dmax[u]) for u in units}
    denl = {u: jnp.sum(s[u], axis=1, keepdims=True) for u in units}
    sloc = {u: s[u].astype(bf16) for u in units}
    kwl = {(c, h): ka[c, h] * wsrc_all[c][:, h:h + 1] for c, h in units}
    for c in range(nch):
        ci = chunks[c][0]
        dmax_all = jnp.zeros((CH, 128), f32)
        denl_all = jnp.zeros((CH, 128), f32)
        for h in range(NH):
            dmax_all = jnp.where(lane == h, dmax[c, h], dmax_all)
            denl_all = jnp.where(lane == h, denl[c, h], denl_all)
        t_ref[ci, 0] = gt[c]
        t_ref[ci, 1] = b_all[c]
        t_ref[ci, 2] = btot_all[c]
        t_ref[ci, 3] = dmax_all
        t_ref[ci, 4] = denl_all
        t_ref[ci, 5] = mloc_all[c]

    xs, pw, qkd, kd = {}, {}, {}, {}
    for c, h in units:
        beta = gt[c][:, 8 + h:9 + h]
        decay = jnp.exp(jnp.where(tri, cs[c][:, 12 + h:13 + h] - mix_t[c][12 + h:13 + h, :], NEG))
        pw[c, h] = jnp.where(stri, beta * kk[c, h] * decay, 0.0)
        xs[c, h] = jnp.concatenate([beta * vg[c, h], (beta * expg_all[c][:, 12 + h:13 + h]) * kgf[c, h]], axis=1)
        qkd[c, h] = (qk[c, h] * decay).astype(bf16)
        kd[c, h] = (kgf[c, h] * expdiff_all[c][:, 12 + h:13 + h]).astype(bf16)

    for c, h in units:
        ci = chunks[c][0]
        numl_ref[ci, h] = _dot(sloc[c, h], va[c, h])
        kwb = kwl[c, h].astype(bf16)
        for i, sl in enumerate(seqs):
            kv_ref[ci, i * NH + h] = _dot_tn(kwb[sl], va[c, h][sl])
            nv_ref[ci, i, h:h + 1, :] = jnp.sum(kwl[c, h][sl], axis=0, keepdims=True)

    sub = min(SUB, blk)
    a_d = {u: jnp.where(k["same_sub"], pw[u], 0.0) for u in units}
    tinv = {u: k["eye"] - a_d[u] for u in units}
    pcur = {u: -a_d[u] for u in units}
    for it in range(1, sub.bit_length() - 1):
        pb = {u: pcur[u].astype(bf16) for u in units}
        pcur = {u: _dot(pb[u], pb[u]) for u in units}
        tinv = {u: tinv[u] + _dot(pcur[u].astype(bf16), tinv[u].astype(bf16)) for u in units}
    tb = {u: tinv[u].astype(bf16) for u in units}
    if sub == blk:
        xs = {u: _dot(tb[u], xs[u].astype(bf16)) for u in units}
    else:
        a_off = {u: (pw[u] - a_d[u]).astype(bf16) for u in units}
        nsub = CH // sub
        done = {u: [] for u in units}
        for j in range(nsub):
            rows = slice(j * sub, (j + 1) * sub)
            pad = lambda parts, n: parts + ([jnp.zeros((n, 2 * DH), f32)] if n else [])
            v = {u: xs[u][rows] for u in units}
            if j > 0:
                xcat = {u: jnp.concatenate(pad(done[u], CH - j * sub), axis=0).astype(bf16) for u in units}
                v = {u: v[u] - _dot(a_off[u][rows], xcat[u]) for u in units}
            vcat = {u: jnp.concatenate(pad(pad([], j * sub) + [v[u]], CH - (j + 1) * sub), axis=0).astype(bf16)
                    for u in units}
            for u in units:
                done[u].append(_dot(tb[u][rows], vcat[u]))
        xs = {u: jnp.concatenate(done[u], axis=0) for u in units}

    xb = {u: xs[u].astype(bf16) for u in units}
    qx = {u: _dot(qkd[u], xb[u]) for u in units}
    for c, h in units:
        ci = chunks[c][0]
        o2_ref[ci, h] = qx[c, h][:, :DH]
        qeff_ref[ci, h] = (expg_all[c][:, 12 + h:13 + h] * qg[c, h] - qx[c, h][:, DH:]).astype(bf16)
        for i, sl in enumerate(seqs):
            mm = _dot_tn(kd[c, h][sl], xb[c, h][sl])
            m2_ref[ci, i * NH + h] = mm[:, :DH]
            m1_ref[ci, i * NH + h] = mm[:, DH:].astype(bf16)


def _phase_b(blk, k, pr, par, st, sc, out, ci):
    nseq = CH // blk
    c_ref, n_ref, m_ref, s_ref = st
    t_ref, numl_ref, kv_ref, nv_ref, m1_ref, m2_ref, qeff_ref, o2_ref = sc
    ha_ref, hb_ref, rows = out
    seqs = [slice(i * blk, (i + 1) * blk) for i in range(nseq)]
    heads = range(NH)
    gt, b_all, btot_all = t_ref[ci, 0], t_ref[ci, 1], t_ref[ci, 2]
    dmax_all, denl_all, mloc_all = t_ref[ci, 3], t_ref[ci, 4], t_ref[ci, 5]

    qa = [pr(C_QA + h * DH, DH) for h in heads]
    qc, ms, oq = [], [], []
    for h in heads:
        qb = qa[h].astype(bf16)
        qc.append([_dot(qb[sl], c_ref[i, h].astype(bf16)) for i, sl in enumerate(seqs)])
    for h in heads:
        qe = qeff_ref[ci, h]
        ms_h, oq_h = [], []
        for i, sl in enumerate(seqs):
            sb = s_ref[i, h].astype(bf16)
            ms_h.append(_dot(m1_ref[ci, i * NH + h], sb))
            oq_h.append(_dot(qe[sl], sb))
        ms.append(ms_h)
        oq.append(oq_h)

    if nseq == 1:
        mprev_all = jnp.broadcast_to(m_ref[0:1, :], (CH, 128))
    else:
        mprev_all = _sel_dot(k["expand_m"], m_ref[...])
    inter = b_all + mprev_all
    mt = jnp.maximum(inter, dmax_all)
    scale_all = jnp.exp(dmax_all - mt)
    winter_all = jnp.exp(inter - mt)
    emt_all = jnp.exp(-mt)
    carry = btot_all + mprev_all
    mnew = jnp.maximum(carry, mloc_all)
    sc2_all = jnp.exp(mloc_all - mnew)
    wold_all = jnp.exp(carry - mnew)
    egt_all = jnp.exp(btot_all)
    if nseq == 1:
        m_ref[0:1, :] = mnew[CH - 1:CH, :]
    else:
        m_ref[...] = _sel_dot(k["rowsel_m"], mnew)

    cat = lambda parts: parts[0] if nseq == 1 else jnp.concatenate(parts, axis=0)
    qn = [cat([jnp.sum(qa[h][sl] * n_ref[i, h:h + 1, :], axis=1, keepdims=True) for i, sl in enumerate(seqs)])
          for h in heads]
    hraw = []
    for h in heads:
        scale, winter = scale_all[:, h:h + 1], winter_all[:, h:h + 1]
        den = scale * denl_all[:, h:h + 1] + winter * qn[h]
        inv = 1.0 / jnp.maximum(jnp.abs(den), emt_all[:, h:h + 1])
        hraw.append((scale * inv) * numl_ref[ci, h] + (winter * inv) * cat(qc[h]))
    o = [cat(oq[h]) + o2_ref[ci, h] for h in heads]

    for h in heads:
        for i in range(nseq):
            last = (i + 1) * blk - 1
            wold = wold_all[last:last + 1, h:h + 1]
            sc2 = sc2_all[last:last + 1, h:h + 1]
            c_ref[i, h] = wold * c_ref[i, h] + sc2 * kv_ref[ci, i * NH + h]
            n_ref[i, h:h + 1, :] = wold * n_ref[i, h:h + 1, :] + sc2 * nv_ref[ci, i, h:h + 1, :]
            s_ref[i, h] = (egt_all[last:last + 1, 8 + h:9 + h] * s_ref[i, h] - ms[h][i]) + m2_ref[ci, i * NH + h]

    for h in heads:
        ha_ref[rows, h * DH:(h + 1) * DH] = hraw[h]
        hb_ref[rows, h * DH:(h + 1) * DH] = o[h]


def _mixer_long_kernel(t, p_ref, c0_ref, n0_ref, m0_ref, s0_ref, gb_ref, al_ref,
                       ha_ref, hb_ref, c_ref, n_ref, m_ref, s_ref, *sc):
    g = pl.program_id(1)

    @pl.when(g == 0)
    def _():
        c_ref[...] = c0_ref[...]
        n_ref[...] = n0_ref[...]
        m_ref[...] = m0_ref[...]
        s_ref[...] = s0_ref[...]

    k = _chunk_consts(CH, CH)
    par = (gb_ref[...], al_ref[...])
    st = (c_ref, n_ref, _M0(m_ref), s_ref)

    def chunk_views(ci):
        rows = pl.ds(pl.multiple_of(ci * CH, CH), CH)
        return ci, (lambda c0, n: p_ref[rows, c0:c0 + n])

    def body_a(j, carry):
        _phase_a(CH, CH, k, [chunk_views(A_GROUP * j + c) for c in range(A_GROUP)], par, sc)
        return carry

    def body_b(j, carry):
        for c in range(B_UNROLL):
            ci = B_UNROLL * j + c
            rows = pl.ds(pl.multiple_of(ci * CH, CH), CH)
            _phase_b(CH, k, lambda c0, n, rows=rows: p_ref[rows, c0:c0 + n], par, st, sc, (ha_ref, hb_ref, rows), ci)
        return carry

    lax.fori_loop(0, t // (CH * A_GROUP), body_a, 0)
    lax.fori_loop(0, t // (CH * B_UNROLL), body_b, 0)


class _M0:
    def __init__(self, ref):
        self.ref = ref

    def __getitem__(self, idx):
        return self.ref[0] if idx is Ellipsis else self.ref[(0,) + idx]

    def __setitem__(self, idx, val):
        if idx is Ellipsis:
            self.ref[0] = val
        else:
            self.ref[(0,) + idx] = val


def _mixer_long(p, c0, n0, m0, s0, gb, al, nb, seq, t):
    nt = seq // t
    row_spec = lambda w: pl.BlockSpec((t, w), lambda b, g: (b * nt + g, 0))
    const = lambda shape: pl.BlockSpec(shape, lambda b, g: (0,) * len(shape))
    perb = lambda shape: pl.BlockSpec((1,) + shape, lambda b, g: (b,) + (0,) * len(shape))
    return pl.pallas_call(
        functools.partial(_mixer_long_kernel, t),
        grid=(nb, nt),
        in_specs=[row_spec(N_MIX),
                  const((1, NH, DH, DH)), const((1, NH, DH)), const((1, 8, 128)), const((1, NH, DH, DH)),
                  const((1, 128)), const((1, 128))],
        out_specs=[row_spec(WB), row_spec(WB),
                   perb((NH, DH, DH)), perb((NH, DH)), perb((8, 128)), perb((NH, DH, DH))],
        out_shape=[jax.ShapeDtypeStruct((nb * seq, WB), f32), jax.ShapeDtypeStruct((nb * seq, WB), f32),
                   jax.ShapeDtypeStruct((nb, NH, DH, DH), f32), jax.ShapeDtypeStruct((nb, NH, DH), f32),
                   jax.ShapeDtypeStruct((nb, 8, 128), f32), jax.ShapeDtypeStruct((nb, NH, DH, DH), f32)],
        scratch_shapes=_mix_scratch(t // CH, 1),
        compiler_params=_cparams(("arbitrary", "arbitrary")),
        name="mixer_long",
    )(p, c0, n0, m0, s0, gb, al)


def _mixer_block_kernel(blk, valid, n_alias, first_of, p_ref, c0_ref, n0_ref, m0_ref, s0_ref, gb_ref, al_ref,
                        *rest):
    ha_ref, hb_ref, c_ref, n_ref, m_ref, s_ref = rest[n_alias:n_alias + 6]
    sc = rest[n_alias + 6:]
    if first_of is not None:
        layer, depth = first_of
        for other in range(depth):
            if other != layer:
                c_ref[other] = jnp.zeros(c_ref.shape[1:], f32)
                s_ref[other] = jnp.zeros(s_ref.shape[1:], f32)
        c_ref, s_ref = c_ref.at[layer], s_ref.at[layer]
    c_ref[...] = c0_ref[...]
    n_ref[...] = n0_ref[...]
    m_ref[...] = m0_ref[...]
    s_ref[...] = s0_ref[...]
    k = _chunk_consts(blk, valid)
    par = (gb_ref[...], al_ref[...])
    pr = lambda c0, n: p_ref[:, c0:c0 + n]
    _phase_a(blk, valid, k, [(0, pr)], par, sc)
    _phase_b(blk, k, pr, par, (c_ref, n_ref, _M0(m_ref), s_ref), sc, (ha_ref, hb_ref, slice(0, CH)), 0)


def _mixer_block(p, c0, n0, m0, s0, gb, al, blk, valid, layer=None, prev=None):
    ng = p.shape[0] // CH
    nseq = CH // blk
    row_spec = lambda w: pl.BlockSpec((CH, w), lambda g: (g, 0))
    const = lambda shape: pl.BlockSpec(shape, lambda g: (0,) * len(shape))
    perg = lambda shape: pl.BlockSpec(shape, lambda g: (g,) + (0,) * (len(shape) - 1))
    big_shape = (ng * nseq, NH, DH, DH)
    first_of = None
    if layer is None:
        big = big_out = perg((nseq, NH, DH, DH))
    else:
        depth = c0.shape[0]
        big = big_out = pl.BlockSpec((None, nseq, NH, DH, DH), lambda g: (layer, g, 0, 0, 0))
        big_shape = (depth,) + big_shape
        if prev is None:
            first_of = (layer, depth)
            big_out = pl.BlockSpec((depth, nseq, NH, DH, DH), lambda g: (0, g, 0, 0, 0))
    aliased = () if prev is None else tuple(prev)
    n_in = 7
    return pl.pallas_call(
        functools.partial(_mixer_block_kernel, blk, valid, len(aliased), first_of),
        grid=(ng,),
        in_specs=[row_spec(N_MIX), big, perg((nseq, NH, DH)), perg((1, 8, 128)), big,
                  const((1, 128)), const((1, 128))]
                 + [pl.BlockSpec(memory_space=pl.ANY)] * len(aliased),
        out_specs=[row_spec(WB), row_spec(WB), big_out, perg((nseq, NH, DH)), perg((1, 8, 128)), big_out],
        out_shape=[jax.ShapeDtypeStruct((ng * CH, WB), f32), jax.ShapeDtypeStruct((ng * CH, WB), f32),
                   jax.ShapeDtypeStruct(big_shape, f32), jax.ShapeDtypeStruct((ng * nseq, NH, DH), f32),
                   jax.ShapeDtypeStruct((ng, 8, 128), f32), jax.ShapeDtypeStruct(big_shape, f32)],
        input_output_aliases={n_in: 2, n_in + 1: 5} if aliased else {},
        scratch_shapes=_mix_scratch(1, nseq),
        compiler_params=_cparams(("arbitrary",)),
        name="mixer_block",
    )(p, c0, n0, m0, s0, gb, al, *aliased)


def _outproj_kernel(x_ref, ha_ref, hb_ref, po_ref, pz_ref, anw_ref, bnw_ref,
                    wm_ref, wpa_ref, wpb_ref, wo_ref, g_ref, b_ref, o_ref):
    x = x_ref[...]
    mg = _sigmoid(_dot(x.astype(bf16), wm_ref[...]))
    cols = [slice(h * DH, (h + 1) * DH) for h in range(NH)]
    hr = [ha_ref[:, c] for c in cols]
    ob = [hb_ref[:, c] for c in cols]
    mu = [jnp.mean(v, axis=1, keepdims=True) for v in hr]
    osq = [jnp.mean(v * v, axis=1, keepdims=True) for v in ob]
    hc = [hr[h] - mu[h] for h in range(NH)]
    var = [jnp.mean(v * v, axis=1, keepdims=True) for v in hc]
    ha = [_sigmoid(po_ref[:, cols[h]]) * (hc[h] * lax.rsqrt(var[h] + NORM_EPS) * anw_ref[:, cols[h]])
          for h in range(NH)]
    hb = [(ob[h] * lax.rsqrt(osq[h] + NORM_EPS) * bnw_ref[...]) * _silu(pz_ref[:, cols[h]]) for h in range(NH)]
    ya = _dot(jnp.concatenate(ha, axis=1).astype(bf16), wpa_ref[...])
    yb = _dot(jnp.concatenate(hb, axis=1).astype(bf16), wpb_ref[...])
    y = mg[:, :D_MODEL] * ya + mg[:, D_MODEL:] * yb
    mix = _dot(y.astype(bf16), wo_ref[...])
    o_ref[...] = _ln_rows(ALPHA * x + mix, g_ref[...], b_ref[...])


def _outproj(x, ha, hb, p, anw, bnw, w_merge, layer, w_pa, w_pb, w_out, g, b):
    m = x.shape[0]
    tm = min(m, PROJ_TM)
    row_spec = lambda w: pl.BlockSpec((tm, w), lambda i: (i, 0))
    pcol = lambda c0: pl.BlockSpec((tm, WB), lambda i: (i, c0 // WB))
    const = lambda shape: pl.BlockSpec(shape, lambda i: (0, 0))
    return pl.pallas_call(
        _outproj_kernel,
        grid=(m // tm,),
        in_specs=[row_spec(D_MODEL), row_spec(WB), row_spec(WB), pcol(C_OA), pcol(C_Z),
                  const((1, WB)), const((1, DH)),
                  pl.BlockSpec((None, D_MODEL, 2 * D_MODEL), lambda i: (layer, 0, 0)),
                  const((WB, D_MODEL)), const((WB, D_MODEL)),
                  const((D_MODEL, D_MODEL)), const((1, D_MODEL)), const((1, D_MODEL))],
        out_specs=row_spec(D_MODEL),
        out_shape=jax.ShapeDtypeStruct((m, D_MODEL), f32),
        compiler_params=_cparams(("arbitrary",)),
        name="merge_outproj_ln",
    )(x, ha, hb, p, p, anw, bnw, w_merge, w_pa, w_pb, w_out, g, b)


def _ffn2_kernel(long_mode, tm, nmt, blk, valid, *refs):
    if long_mode:
        (x_ref, wa_ref, wb_ref, cwa_ref, cwb_ref, wd_ref, g_ref, b_ref, ha_ref, hb_ref,
         o_ref, sa_ref, sb_ref, hh_ref, ca_ref, cb_ref) = refs

        @pl.when(pl.program_id(0) % nmt == 0)
        def _():
            ca_ref[...] = ha_ref[0]
            cb_ref[...] = hb_ref[0]
    else:
        (x_ref, wa_ref, wb_ref, cwa_ref, cwb_ref, wd_ref, g_ref, b_ref, ea_ref, eb_ref,
         o_ref, sa_ref, sb_ref, hh_ref) = refs
        nsq = tm // blk
        r0 = blk - valid - (CONV_F - 1)
        row3 = lax.broadcasted_iota(jnp.int32, (nsq, blk, FT), 1)

        def merge_history(u, e_ref, s_ref, cols):
            u3 = u.reshape(nsq, blk, FT)
            e3 = e_ref[:, :, cols]
            for r in range(CONV_F - 1):
                u3 = jnp.where(row3 == r0 + r, e3[:, r:r + 1, :], u3)
            s_ref[:, :, cols] = u3[:, blk - (CONV_F - 1):, :]
            return u3.reshape(tm, FT)
    xb = x_ref[...].astype(bf16)
    for cb in range(NF):
        cols = slice(cb * FT, (cb + 1) * FT)
        ua = _dot(xb, wa_ref[:, cols])
        ub = _dot(xb, wb_ref[:, cols])
        if long_mode:
            h8a, h8b = ca_ref[:, cols], cb_ref[:, cols]
            ta, tb = ua[tm - HDR:tm], ub[tm - HDR:tm]
            ca_ref[:, cols] = ta
            cb_ref[:, cols] = tb
            sa_ref[0, :, cols] = ta
            sb_ref[0, :, cols] = tb
        else:
            h8a = h8b = None
            ua = merge_history(ua, ea_ref, sa_ref, cols)
            ub = merge_history(ub, eb_ref, sb_ref, cols)
        hh = _silu(_conv_rows(ua, h8a, cwa_ref[:, cols])) * _conv_rows(ub, h8b, cwb_ref[:, cols])
        hh_ref[:, cols] = hh.astype(bf16)
    out = _dot(hh_ref[...], wd_ref[...])
    o_ref[...] = _ln_rows(ALPHA * x_ref[...] + out, g_ref[...], b_ref[...])


def _ffn2(x, w_up, cw, w_down, layer, g, b, hist_a=None, hist_b=None,
          e_ab=None, nb=1, blk=CH, valid=CH):
    m = x.shape[0]
    long_mode = hist_a is not None
    seq = m // nb
    tm = min(seq, FFN_TM if long_mode else FFN_TM_BLOCK)
    nmt = seq // tm
    rows = lambda w: pl.BlockSpec((tm, w), lambda i: (i, 0))
    const = lambda shape: pl.BlockSpec(shape, lambda i: (0,) * len(shape))
    lsp = lambda shape, j: pl.BlockSpec((None,) + shape, lambda i: (layer, 0, j))
    in_specs = [rows(D_MODEL), lsp((D_MODEL, D_FF), 0), lsp((D_MODEL, D_FF), 1), lsp((CONV_F, D_FF), 0),
                lsp((CONV_F, D_FF), 1), lsp((D_FF, D_MODEL), 0), const((1, D_MODEL)), const((1, D_MODEL))]
    scratch = [pltpu.VMEM((tm, D_FF), bf16)]
    if long_mode:
        in_specs += [const((1, HDR, D_FF))] * 2
        st_spec = pl.BlockSpec((1, HDR, D_FF), lambda i: (i, 0, 0))
        st_shape = jax.ShapeDtypeStruct((m // tm, HDR, D_FF), f32)
        scratch += [pltpu.VMEM((HDR, D_FF), f32), pltpu.VMEM((HDR, D_FF), f32)]
        extra = (hist_a, hist_b)
    else:
        nsq = tm // blk
        if e_ab.ndim == 4:
            e_spec = lambda j: pl.BlockSpec((None, nsq, CONV_F - 1, D_FF), lambda i: (layer, i, 0, j))
        else:
            e_spec = lambda j: pl.BlockSpec((nsq, CONV_F - 1, D_FF), lambda i: (i, 0, j))
        in_specs += [e_spec(0), e_spec(1)]
        st_spec = pl.BlockSpec((nsq, CONV_F - 1, D_FF), lambda i: (i, 0, 0))
        st_shape = jax.ShapeDtypeStruct((m // blk, CONV_F - 1, D_FF), f32)
        extra = (e_ab, e_ab)
    return pl.pallas_call(
        functools.partial(_ffn2_kernel, long_mode, tm, nmt, blk, valid),
        grid=(m // tm,),
        in_specs=in_specs,
        out_specs=[rows(D_MODEL), st_spec, st_spec],
        out_shape=[jax.ShapeDtypeStruct((m, D_MODEL), f32), st_shape, st_shape],
        scratch_shapes=scratch,
        compiler_params=_cparams(("arbitrary",)),
        name="conv_ffn_ln",
    )(x, w_up, w_up, cw, cw, w_down, g, b, *extra)


def _mix_weights(w_in):
    w = w_in.astype(bf16)
    zc = jnp.zeros(w.shape[:2] + (128 - 4 * NH,), bf16)
    w_mix = jnp.concatenate([w[..., A_Q:A_I], w[..., B_Z:B_BETA], w[..., A_I:B_QKV], w[..., B_BETA:G_MERGE], zc,
                             w[..., B_QKV:B_Z]], axis=-1)
    return w_mix, w[..., G_MERGE:]


def _layer_weights(l, w_mix, w_merge, mlstm_gate_bias, mlstm_norm_w, gdn_conv_w, gdn_A_log, gdn_dt_bias, gdn_norm_w,
                   w_branch_a, w_branch_b, w_out, ln1_g, ln1_b, w_up, ffn_conv_w, w_down, ln2_g, ln2_b):
    z4 = jnp.zeros((NH,), f32)
    gb = jnp.concatenate([mlstm_gate_bias[l], z4, gdn_dt_bias[l], jnp.zeros((128 - 4 * NH,), f32)])[None]
    al = jnp.concatenate([z4, z4, z4, gdn_A_log[l], jnp.zeros((128 - 4 * NH,), f32)])[None]
    return dict(
        layer=l, w_mix=w_mix, w_merge=w_merge, gb=gb, al=al,
        anw=mlstm_norm_w[l][None], bnw=gdn_norm_w[l][None], cw=gdn_conv_w[l],
        w_pa=w_branch_a[l].astype(bf16), w_pb=w_branch_b[l].astype(bf16), w_out=w_out[l].astype(bf16),
        ln1_g=ln1_g[l][None], ln1_b=ln1_b[l][None],
        w_up=w_up, cw_f=ffn_conv_w, w_down=w_down, ln2_g=ln2_g[l][None], ln2_b=ln2_b[l][None])


def _block_layer(x, lw, st, blk, valid, layer=None, prev=None, ln=None):
    c0, n0, m0, s0, gbuf, fbuf = st
    nseq_tot = n0.shape[0]
    nseq = CH // blk
    ng = nseq_tot // nseq
    m0p = jnp.pad(m0.reshape(ng, nseq, NH), ((0, 0), (0, 8 - nseq), (0, 128 - NH)))
    if ln is None:
        p, ext = _inproj(x, lw["w_mix"], lw["layer"], lw["cw"], e=gbuf, blk=blk, valid=valid)
    else:
        p, ext, x = _inproj(x, lw["w_mix"], lw["layer"], lw["cw"], e=gbuf, blk=blk, valid=valid, ln=ln)
    ha, hb, c, n, m, s = _mixer_block(p, c0, n0, m0p, s0, lw["gb"], lw["al"], blk, valid, layer, prev)
    x1 = _outproj(x, ha, hb, p, lw["anw"], lw["bnw"], lw["w_merge"], lw["layer"], lw["w_pa"], lw["w_pb"],
                  lw["w_out"],
                  lw["ln1_g"], lw["ln1_b"])
    x2, ua, ub = _ffn2(x1, lw["w_up"], lw["cw_f"], lw["w_down"], lw["layer"],
                       lw["ln2_g"], lw["ln2_b"], e_ab=fbuf, blk=blk, valid=valid)
    m_new = m[:, :nseq, :NH].reshape(nseq_tot, NH)
    gconv = ext
    fconv = jnp.concatenate([ua, ub], axis=2)
    return x2, (c, n, m_new, s, gconv, fconv)


def _long_layer(x, lw, st, nb, seq, ln=None):
    c0, n0, m0, s0, gbuf, fbuf = st
    m0p = jnp.pad(m0.reshape(1, 1, NH), ((0, 0), (0, 7), (0, 128 - NH)))
    hist = jnp.pad(gbuf, ((0, 0), (HDR - (CONV_B - 1), 0), (0, 0)))
    hf = jnp.pad(fbuf, ((0, 0), (HDR - (CONV_F - 1), 0), (0, 0)))
    if ln is None:
        p, tails = _inproj(x, lw["w_mix"], lw["layer"], lw["cw"], hist=hist, nb=nb)
    else:
        p, tails, x = _inproj(x, lw["w_mix"], lw["layer"], lw["cw"], hist=hist, nb=nb, ln=ln)
    ha, hb, c, n, m, s = _mixer_long(p, c0, n0, m0p, s0, lw["gb"], lw["al"], nb, seq, MIX_T)
    x1 = _outproj(x, ha, hb, p, lw["anw"], lw["bnw"], lw["w_merge"], lw["layer"], lw["w_pa"], lw["w_pb"],
                  lw["w_out"],
                  lw["ln1_g"], lw["ln1_b"])
    x2, sa, sb = _ffn2(x1, lw["w_up"], lw["cw_f"], lw["w_down"], lw["layer"],
                       lw["ln2_g"], lw["ln2_b"], hist_a=hf[:, :, :D_FF], hist_b=hf[:, :, D_FF:], nb=nb)
    last_tile = lambda a: a.reshape((nb, -1) + a.shape[1:])[:, -1]
    gconv = last_tile(tails)[:, HDR - (CONV_B - 1):]
    fconv = jnp.concatenate([last_tile(sa), last_tile(sb)], axis=2)[:, HDR - (CONV_F - 1):]
    return x2, (c, n, m[:, 0, :NH], s, gconv, fconv)


def kernel(x_prompt, x_sample, state_mlstm_C, state_mlstm_n, state_mlstm_m, state_gdn_S, state_gdn_conv, state_ffn_conv, meta_tokens, ln_emb_g, ln_emb_b, w_in, mlstm_gate_bias, mlstm_norm_w, gdn_conv_w, gdn_A_log, gdn_dt_bias, gdn_norm_w, w_branch_a, w_branch_b, w_out, ln1_g, ln1_b, w_up, ffn_conv_w, w_down, ln2_g, ln2_b):
    nb, seq, _ = x_prompt.shape
    ns, ls, _ = x_sample.shape
    sblk = 8
    assert seq % max(MIX_T, PROJ_TM, FFN_TM) == 0 and ls + CONV_B - 1 <= sblk and ns % (CH // sblk) == 0 and N_META + CONV_B - 1 <= CH
    w_mix, w_merge = _mix_weights(w_in)
    w_up_b16, w_down_b16 = w_up.astype(bf16), w_down.astype(bf16)
    lws =[_layer_weights(l, w_mix, w_merge, mlstm_gate_bias, mlstm_norm_w, gdn_conv_w, gdn_A_log, gdn_dt_bias,
                          gdn_norm_w, w_branch_a, w_branch_b, w_out, ln1_g, ln1_b, w_up_b16, ffn_conv_w,
                          w_down_b16, ln2_g, ln2_b) for l in range(DEPTH)]
    emb_ln = (ln_emb_g[None], ln_emb_b[None])

    xm = jnp.pad(meta_tokens, ((CH - N_META, 0), (0, 0)))
    xs = jnp.pad(x_sample, ((0, 0), (sblk - ls, 0), (0, 0))).reshape(ns * sblk, D_MODEL)
    xp = x_prompt.reshape(nb * seq, D_MODEL)

    zero_st = (jnp.zeros((1, NH, DH, DH), f32), jnp.zeros((1, NH, DH), f32), jnp.zeros((1, NH), f32),
               jnp.zeros((1, NH, DH, DH), f32), jnp.zeros((1, CONV_B - 1, 3 * WB), f32),
               jnp.zeros((1, CONV_F - 1, 2 * D_FF), f32))
    p_states, s_states = [], []
    big = None
    for l in range(DEPTH):
        ln = emb_ln if l == 0 else None
        xm, st_m = _block_layer(xm, lws[l], zero_st, CH, N_META, ln=ln)
        xp, st_p = _long_layer(xp, lws[l], st_m, nb, seq, ln=ln)
        samp_st = (state_mlstm_C, state_mlstm_n[l], state_mlstm_m[l], state_gdn_S,
                   state_gdn_conv[l], state_ffn_conv)
        xs, st_s = _block_layer(xs, lws[l], samp_st, sblk, ls, layer=l, prev=big, ln=ln)
        big = (st_s[0], st_s[3])
        p_states.append(st_p)
        s_states.append(st_s)

    stack = lambda states, i: jnp.stack([s[i] for s in states], axis=0)
    y_prompt = xp.reshape(nb, seq, D_MODEL)
    y_sample = xs.reshape(ns, sblk, D_MODEL)[:, sblk - ls:]
    return (y_prompt, y_sample,
            stack(p_states, 0), stack(p_states, 1), stack(p_states, 2), stack(p_states, 3),
            stack(p_states, 4), stack(p_states, 5),
            big[0], stack(s_states, 1), stack(s_states, 2), big[1],
            stack(s_states, 4), stack(s_states, 5))
```

```python
import functools

import jax
import jax.numpy as jnp
from jax import lax
from jax.experimental import pallas as pl
from jax.experimental.pallas import tpu as pltpu

f32 = jnp.float32
bf16 = jnp.bfloat16

D_MODEL = 1024
N_META = 16
NH = 4
DH = 128
WB = NH * DH
CONV_B = 4
D_FF = 2816
CONV_F = 3
DEPTH = 2
ALPHA = (2 * DEPTH) ** 0.25
LN_EPS = 1e-5
NORM_EPS = 1e-6
QSCALE = DH ** -0.5

A_Q = 0
A_I = 4 * WB
B_QKV = A_I + 2 * NH
B_Z = B_QKV + 3 * WB
B_BETA = B_Z + WB
G_MERGE = B_BETA + 2 * NH

C_QA, C_KA, C_VA, C_OA = 0, WB, 2 * WB, 3 * WB
C_Z = 4 * WB
C_G = 5 * WB
C_QKV = C_G + 128
N_MIX = C_QKV + 3 * WB

CH = 64
SUB = 16
B_UNROLL = 2
A_GROUP = 8
HDR = 8
NEG = -1e30
FT = 256
NF = D_FF // FT
FFN_TM_BLOCK = 512
FFN_TM = 512
MIX_T = 512
PROJ_TM = 512
VMEM_LIMIT = 56 * 1024 * 1024


def _cparams(sem):
    return pltpu.CompilerParams(dimension_semantics=sem, vmem_limit_bytes=VMEM_LIMIT)


def _dot(a, b):
    return jnp.dot(a, b, preferred_element_type=f32)


def _dot_nt(a, b):
    return lax.dot_general(a, b, (((1,), (1,)), ((), ())), preferred_element_type=f32)


def _dot_tn(a, b):
    return lax.dot_general(a, b, (((0,), (0,)), ((), ())), preferred_element_type=f32)


def _sel_dot(sel, x):
    hi = x.astype(bf16)
    r1 = x - hi.astype(f32)
    mid = r1.astype(bf16)
    lo = (r1 - mid.astype(f32)).astype(bf16)
    return (_dot(sel, hi) + _dot(sel, mid)) + _dot(sel, lo)


def _softplus(x):
    return jnp.maximum(x, 0.0) + jnp.log1p(jnp.exp(-jnp.abs(x)))


def _sigmoid(x):
    return 0.5 * jnp.tanh(0.5 * x) + 0.5


def _silu(x):
    t = 0.5 * x
    return t * (jnp.tanh(t) + 1.0)


def _ln_rows(x, g, b):
    mu = jnp.mean(x, axis=-1, keepdims=True)
    xc = x - mu
    var = jnp.mean(xc * xc, axis=-1, keepdims=True)
    return xc * lax.rsqrt(var + LN_EPS) * g + b


def _conv_rows(u, h8, cw):
    width = cw.shape[0]
    row = lax.broadcasted_iota(jnp.int32, (HDR, u.shape[1]), 0) if h8 is not None else None
    acc = None
    for j in range(width):
        sh = width - 1 - j
        term = u if sh == 0 else pltpu.roll(u, sh, 0)
        if h8 is not None and sh > 0:
            head = term[0:HDR]
            for r in range(sh):
                head = jnp.where(row == r, h8[HDR - sh + r:HDR - sh + r + 1], head)
            term = jnp.concatenate([head, term[HDR:]], axis=0)
        term = term * cw[j:j + 1]
        acc = term if acc is None else acc + term
    return acc


def _inproj_kernel(long_mode, with_ln, tm, nmt, blk, valid, *refs):
    if with_ln:
        g_ref, b_ref, xln_ref = refs[4], refs[5], refs[6]
        refs = refs[:4] + refs[7:]
    if long_mode:
        x_ref, w_ref, cw_ref, hist_ref, o_ref, tail_ref, carry_ref = refs

        @pl.when(pl.program_id(0) % nmt == 0)
        def _():
            carry_ref[...] = hist_ref[0]
    else:
        x_ref, w_ref, cw_ref, e_ref, o_ref, tail_ref = refs
        nsq = tm // blk
        r0 = blk - valid - (CONV_B - 1)
        row3 = lax.broadcasted_iota(jnp.int32, (nsq, blk, 2 * DH), 1)
    if with_ln:
        xln = _ln_rows(x_ref[...], g_ref[...], b_ref[...])
        xln_ref[...] = xln
        xb = xln.astype(bf16)
    else:
        xb = x_ref[...].astype(bf16)
    nstep = 3 * NH // 2
    cuts = [C_QA + WB] + [C_KA + (C_QKV - C_KA) * (i + 1) // (nstep - 1) // 128 * 128 for i in range(nstep - 1)]
    cuts[-1] = C_QKV
    for cb in range(nstep):
        cols = slice(cb * 2 * DH, (cb + 1) * 2 * DH)
        u = _dot(xb, w_ref[:, C_QKV + cb * 2 * DH:C_QKV + (cb + 1) * 2 * DH])
        lo = C_QA if cb == 0 else cuts[cb - 1]
        plain = _dot(xb, w_ref[:, lo:cuts[cb]])
        o_ref[:, lo:cuts[cb]] = plain * QSCALE if cb == 0 else plain
        if long_mode:
            h8 = carry_ref[:, cols]
            t8 = u[tm - HDR:tm]
            carry_ref[:, cols] = t8
            tail_ref[0, :, cols] = t8
        else:
            h8 = None
            u3 = u.reshape(nsq, blk, 2 * DH)
            e3 = e_ref[:, :, cols]
            for r in range(CONV_B - 1):
                u3 = jnp.where(row3 == r0 + r, e3[:, r:r + 1, :], u3)
            tail_ref[:, :, cols] = u3[:, blk - (CONV_B - 1):, :]
            u = u3.reshape(tm, 2 * DH)
        c = _silu(_conv_rows(u, h8, cw_ref[:, cols]))
        for half in range(2):
            ch = c[:, half * DH:(half + 1) * DH]
            if cb < NH:
                ch = ch * lax.rsqrt(jnp.sum(ch * ch, axis=1, keepdims=True) + NORM_EPS)
            if cb < NH // 2:
                ch = ch * QSCALE
            c0 = C_QKV + (2 * cb + half) * DH
            o_ref[:, c0:c0 + DH] = ch


def _inproj(x, w_mix, layer, cw, hist=None, e=None, nb=1, blk=CH, valid=CH, ln=None):
    m = x.shape[0]
    long_mode = hist is not None
    seq = m // nb
    tm = min(seq, PROJ_TM)
    nmt = seq // tm
    rows = lambda w: pl.BlockSpec((tm, w), lambda i: (i, 0))
    const = lambda shape: pl.BlockSpec(shape, lambda i: (0,) * len(shape))
    in_specs = [rows(D_MODEL), pl.BlockSpec((None, D_MODEL, N_MIX), lambda i: (layer, 0, 0)),
                const((CONV_B, 3 * WB))]
    if long_mode:
        in_specs += [const((1, HDR, 3 * WB))]
        tail_spec = pl.BlockSpec((1, HDR, 3 * WB), lambda i: (i, 0, 0))
        tail_shape = jax.ShapeDtypeStruct((m // tm, HDR, 3 * WB), f32)
        scratch = [pltpu.VMEM((HDR, 3 * WB), f32)]
        extra = hist
    else:
        nsq = tm // blk
        in_specs += [pl.BlockSpec((nsq, CONV_B - 1, 3 * WB), lambda i: (i, 0, 0))]
        tail_spec = pl.BlockSpec((nsq, CONV_B - 1, 3 * WB), lambda i: (i, 0, 0))
        tail_shape = jax.ShapeDtypeStruct((m // blk, CONV_B - 1, 3 * WB), f32)
        scratch = []
        extra = e
    args = [x, w_mix, cw, extra]
    out_specs = [rows(N_MIX), tail_spec]
    out_shape = [jax.ShapeDtypeStruct((m, N_MIX), f32), tail_shape]
    if ln is not None:
        in_specs += [const((1, D_MODEL))] * 2
        args += list(ln)
        out_specs = [rows(D_MODEL)] + out_specs
        out_shape = [jax.ShapeDtypeStruct((m, D_MODEL), f32)] + out_shape
    res = pl.pallas_call(
        functools.partial(_inproj_kernel, long_mode, ln is not None, tm, nmt, blk, valid),
        grid=(m // tm,),
        in_specs=in_specs,
        out_specs=out_specs,
        out_shape=out_shape,
        scratch_shapes=scratch,
        compiler_params=_cparams(("arbitrary",)),
        name="mixer_inproj",
    )(*args)
    return res if ln is None else (res[1], res[2], res[0])


def _chunk_consts(blk, valid):
    lg = blk.bit_length() - 1
    r = lax.broadcasted_iota(jnp.int32, (CH, CH), 0)
    c = lax.broadcasted_iota(jnp.int32, (CH, CH), 1)
    same = (r >> lg) == (c >> lg)
    tri = jnp.logical_and(same, c <= r)
    stri = jnp.logical_and(same, c < r)
    sel = lambda cond: jnp.where(cond, 1.0, 0.0).astype(bf16)
    tri_m = sel(tri)
    blk_m = sel(same)
    re = lax.broadcasted_iota(jnp.int32, (CH, 8), 0)
    ce = lax.broadcasted_iota(jnp.int32, (CH, 8), 1)
    expand_m = sel(ce == (re >> lg))
    rs = lax.broadcasted_iota(jnp.int32, (8, CH), 0)
    cs = lax.broadcasted_iota(jnp.int32, (8, CH), 1)
    rowsel_m = sel(cs == (rs << lg) + (blk - 1))
    rv = lax.broadcasted_iota(jnp.int32, (CH, 128), 0)
    valid_m = (rv & (blk - 1)) >= (blk - valid)
    lgs = min(SUB, blk).bit_length() - 1
    same_sub = (r >> lgs) == (c >> lgs)
    eye = jnp.where(r == c, 1.0, 0.0)
    couple = {}
    s = 1
    while s < min(SUB, blk):
        lg2 = s.bit_length()
        couple[s] = jnp.logical_and((r >> lg2) == (c >> lg2),
                                    jnp.logical_and((r & (2 * s - 1)) >= s, (c & (2 * s - 1)) < s))
        s *= 2
    return dict(tri=tri, stri=stri, tri_m=tri_m, blk_m=blk_m, same_sub=same_sub, eye=eye, couple=couple,
                expand_m=expand_m, rowsel_m=rowsel_m, valid_m=valid_m)


def _block_max(x, blk):
    if blk == CH:
        return jnp.broadcast_to(jnp.max(x, axis=0, keepdims=True), x.shape)
    x3 = x.reshape(CH // blk, blk, 128)
    return jnp.broadcast_to(jnp.max(x3, axis=1, keepdims=True), x3.shape).reshape(CH, 128)


def _mix_scratch(nc, nseq):
    return [pltpu.VMEM((nc, 6, CH, 128), f32),
            pltpu.VMEM((nc, NH, CH, DH), f32),
            pltpu.VMEM((nc, nseq * NH, DH, DH), f32),
            pltpu.VMEM((nc, nseq, 8, DH), f32),
            pltpu.VMEM((nc, nseq * NH, DH, DH), bf16),
            pltpu.VMEM((nc, nseq * NH, DH, DH), f32),
            pltpu.VMEM((nc, NH, CH, DH), bf16),
            pltpu.VMEM((nc, NH, CH, DH), f32)]


def _phase_a(blk, valid, k, chunks, par, sc):
    nseq = CH // blk
    gbias, alog = par[0], par[1]
    t_ref, numl_ref, kv_ref, nv_ref, m1_ref, m2_ref, qeff_ref, o2_ref = sc
    tri, stri = k["tri"], k["stri"]
    seqs = [slice(i * blk, (i + 1) * blk) for i in range(nseq)]
    nch = len(chunks)
    units = [(c, h) for c in range(nch) for h in range(NH)]
    lane = lax.broadcasted_iota(jnp.int32, (CH, 128), 1)
    cum_lane = jnp.logical_or(jnp.logical_and(lane >= 4, lane < 8), lane >= 12)

    gt = []
    for ci, pr in chunks:
        graw = pr(C_G, 128) + gbias
        g = jnp.where(lane < 4, graw,
                      jnp.where(lane < 8, -_softplus(-graw),
                                jnp.where(lane < 12, _sigmoid(graw),
                                          jnp.where(lane < 16, -jnp.exp(alog) * _softplus(graw), 0.0))))
        if valid < blk:
            g = jnp.where(k["valid_m"], g, jnp.where(lane < 4, NEG, 0.0))
        gt.append(g)
    cs = [_sel_dot(k["tri_m"], g) for g in gt]
    if nseq == 1:
        tot = [jnp.broadcast_to(x[CH - 1:CH, :], (CH, 128)) for x in cs]
    else:
        tot = [_sel_dot(k["blk_m"], g) for g in gt]
    mix_t = [jnp.transpose(jnp.where(cum_lane, cs[c], gt[c])) for c in range(nch)]
    b_all = [pltpu.roll(x, 124, 1) for x in cs]
    btot_all = [pltpu.roll(x, 124, 1) for x in tot]
    wlog = [btot_all[c] - b_all[c] + gt[c] for c in range(nch)]
    mloc_all = [_block_max(x, blk) for x in wlog]
    wsrc_all = [jnp.exp(wlog[c] - mloc_all[c]) for c in range(nch)]
    expg_all = [jnp.exp(x) for x in cs]
    expdiff_all = [jnp.exp(tot[c] - cs[c]) for c in range(nch)]

    qa, ka, va, qg, kgf, vg = {}, {}, {}, {}, {}, {}
    for c, h in units:
        _, pr = chunks[c]
        qa[c, h] = pr(C_QA + h * DH, DH).astype(bf16)
        ka[c, h] = pr(C_KA + h * DH, DH)
        va[c, h] = pr(C_VA + h * DH, DH).astype(bf16)
        qg[c, h] = pr(C_QKV + h * DH, DH)
        kgf[c, h] = pr(C_QKV + WB + h * DH, DH)
        vg[c, h] = pr(C_QKV + 2 * WB + h * DH, DH)
    qgb = {u: qg[u].astype(bf16) for u in units}
    kg = {u: kgf[u].astype(bf16) for u in units}

    sq = {u: _dot_nt(qa[u], ka[u].astype(bf16)) for u in units}
    kk = {u: _dot_nt(kg[u], kg[u]) for u in units}
    qk = {u: _dot_nt(qgb[u], kg[u]) for u in units}

    dm = {(c, h): jnp.where(tri, cs[c][:, 4 + h:5 + h] - mix_t[c][4 + h:5 + h, :] + mix_t[c][h:h + 1, :], NEG)
          for c, h in units}
    dmax = {u: jnp.max(dm[u], axis=1, keepdims=True) for u in units}
    s = {u: sq[u] * jnp.exp(dm[u] - dmax[u]) for u in units}
    denl = {u: jnp.sum(s[u], axis=1, keepdims=True) for u in units}
    sloc = {u: s[u].astype(bf16) for u in units}
    kwl = {(c, h): ka[c, h] * wsrc_all[c][:, h:h + 1] for c, h in units}
    for c in range(nch):
        ci = chunks[c][0]
        dmax_all = jnp.zeros((CH, 128), f32)
        denl_all = jnp.zeros((CH, 128), f32)
        for h in range(NH):
            dmax_all = jnp.where(lane == h, dmax[c, h], dmax_all)
            denl_all = jnp.where(lane == h, denl[c, h], denl_all)
        t_ref[ci, 0] = gt[c]
        t_ref[ci, 1] = b_all[c]
        t_ref[ci, 2] = btot_all[c]
        t_ref[ci, 3] = dmax_all
        t_ref[ci, 4] = denl_all
        t_ref[ci, 5] = mloc_all[c]

    xs, pw, qkd, kd = {}, {}, {}, {}
    for c, h in units:
        beta = gt[c][:, 8 + h:9 + h]
        decay = jnp.exp(jnp.where(tri, cs[c][:, 12 + h:13 + h] - mix_t[c][12 + h:13 + h, :], NEG))
        pw[c, h] = jnp.where(stri, beta * kk[c, h] * decay, 0.0)
        xs[c, h] = jnp.concatenate([beta * vg[c, h], (beta * expg_all[c][:, 12 + h:13 + h]) * kgf[c, h]], axis=1)
        qkd[c, h] = (qk[c, h] * decay).astype(bf16)
        kd[c, h] = (kgf[c, h] * expdiff_all[c][:, 12 + h:13 + h]).astype(bf16)

    for c, h in units:
        ci = chunks[c][0]
        numl_ref[ci, h] = _dot(sloc[c, h], va[c, h])
        kwb = kwl[c, h].astype(bf16)
        for i, sl in enumerate(seqs):
            kv_ref[ci, i * NH + h] = _dot_tn(kwb[sl], va[c, h][sl])
            nv_ref[ci, i, h:h + 1, :] = jnp.sum(kwl[c, h][sl], axis=0, keepdims=True)

    sub = min(SUB, blk)
    a_d = {u: jnp.where(k["same_sub"], pw[u], 0.0) for u in units}
    tinv = {u: k["eye"] - jnp.where(k["couple"][1], pw[u], 0.0) for u in units}
    s = 2
    while s < sub:
        tcur = {u: tinv[u].astype(bf16) for u in units}
        bd = {u: _dot(jnp.where(k["couple"][s], pw[u], 0.0).astype(bf16), tcur[u]) for u in units}
        tinv = {u: tinv[u] - _dot(tcur[u], bd[u].astype(bf16)) for u in units}
        s *= 2
    tb = {u: tinv[u].astype(bf16) for u in units}
    if sub == blk:
        xs = {u: _dot(tb[u], xs[u].astype(bf16)) for u in units}
    else:
        a_off = {u: (pw[u] - a_d[u]).astype(bf16) for u in units}
        nsub = CH // sub
        done = {u: [] for u in units}
        for j in range(nsub):
            rows = slice(j * sub, (j + 1) * sub)
            pad = lambda parts, n: parts + ([jnp.zeros((n, 2 * DH), f32)] if n else [])
            v = {u: xs[u][rows] for u in units}
            if j > 0:
                xcat = {u: jnp.concatenate(pad(done[u], CH - j * sub), axis=0).astype(bf16) for u in units}
                v = {u: v[u] - _dot(a_off[u][rows], xcat[u]) for u in units}
            vcat = {u: jnp.concatenate(pad(pad([], j * sub) + [v[u]], CH - (j + 1) * sub), axis=0).astype(bf16)
                    for u in units}
            for u in units:
                done[u].append(_dot(tb[u][rows], vcat[u]))
        xs = {u: jnp.concatenate(done[u], axis=0) for u in units}

    xb = {u: xs[u].astype(bf16) for u in units}
    qx = {u: _dot(qkd[u], xb[u]) for u in units}
    for c, h in units:
        ci = chunks[c][0]
        o2_ref[ci, h] = qx[c, h][:, :DH]
        qeff_ref[ci, h] = (expg_all[c][:, 12 + h:13 + h] * qg[c, h] - qx[c, h][:, DH:]).astype(bf16)
        for i, sl in enumerate(seqs):
            mm = _dot_tn(kd[c, h][sl], xb[c, h][sl])
            m2_ref[ci, i * NH + h] = mm[:, :DH]
            m1_ref[ci, i * NH + h] = mm[:, DH:].astype(bf16)


def _phase_b(blk, k, pr, par, st, sc, out, ci):
    nseq = CH // blk
    c_ref, n_ref, m_ref, s_ref = st
    t_ref, numl_ref, kv_ref, nv_ref, m1_ref, m2_ref, qeff_ref, o2_ref = sc
    ha_ref, hb_ref, rows = out
    seqs = [slice(i * blk, (i + 1) * blk) for i in range(nseq)]
    heads = range(NH)
    gt, b_all, btot_all = t_ref[ci, 0], t_ref[ci, 1], t_ref[ci, 2]
    dmax_all, denl_all, mloc_all = t_ref[ci, 3], t_ref[ci, 4], t_ref[ci, 5]

    qa = [pr(C_QA + h * DH, DH) for h in heads]
    qc, ms, oq = [], [], []
    for h in heads:
        qb = qa[h].astype(bf16)
        qc.append([_dot(qb[sl], c_ref[i, h].astype(bf16)) for i, sl in enumerate(seqs)])
    for h in heads:
        qe = qeff_ref[ci, h]
        ms_h, oq_h = [], []
        for i, sl in enumerate(seqs):
            sb = s_ref[i, h].astype(bf16)
            ms_h.append(_dot(m1_ref[ci, i * NH + h], sb))
            oq_h.append(_dot(qe[sl], sb))
        ms.append(ms_h)
        oq.append(oq_h)

    if nseq == 1:
        mprev_all = jnp.broadcast_to(m_ref[0:1, :], (CH, 128))
    else:
        mprev_all = _sel_dot(k["expand_m"], m_ref[...])
    inter = b_all + mprev_all
    mt = jnp.maximum(inter, dmax_all)
    scale_all = jnp.exp(dmax_all - mt)
    winter_all = jnp.exp(inter - mt)
    emt_all = jnp.exp(-mt)
    carry = btot_all + mprev_all
    mnew = jnp.maximum(carry, mloc_all)
    sc2_all = jnp.exp(mloc_all - mnew)
    wold_all = jnp.exp(carry - mnew)
    egt_all = jnp.exp(btot_all)
    if nseq == 1:
        m_ref[0:1, :] = mnew[CH - 1:CH, :]
    else:
        m_ref[...] = _sel_dot(k["rowsel_m"], mnew)

    cat = lambda parts: parts[0] if nseq == 1 else jnp.concatenate(parts, axis=0)
    qn = [cat([jnp.sum(qa[h][sl] * n_ref[i, h:h + 1, :], axis=1, keepdims=True) for i, sl in enumerate(seqs)])
          for h in heads]
    hraw = []
    for h in heads:
        scale, winter = scale_all[:, h:h + 1], winter_all[:, h:h + 1]
        den = scale * denl_all[:, h:h + 1] + winter * qn[h]
        inv = 1.0 / jnp.maximum(jnp.abs(den), emt_all[:, h:h + 1])
        hraw.append((scale * inv) * numl_ref[ci, h] + (winter * inv) * cat(qc[h]))
    o = [cat(oq[h]) + o2_ref[ci, h] for h in heads]

    for h in heads:
        for i in range(nseq):
            last = (i + 1) * blk - 1
            wold = wold_all[last:last + 1, h:h + 1]
            sc2 = sc2_all[last:last + 1, h:h + 1]
            c_ref[i, h] = wold * c_ref[i, h] + sc2 * kv_ref[ci, i * NH + h]
            n_ref[i, h:h + 1, :] = wold * n_ref[i, h:h + 1, :] + sc2 * nv_ref[ci, i, h:h + 1, :]
            s_ref[i, h] = (egt_all[last:last + 1, 8 + h:9 + h] * s_ref[i, h] - ms[h][i]) + m2_ref[ci, i * NH + h]

    for h in heads:
        ha_ref[rows, h * DH:(h + 1) * DH] = hraw[h]
        hb_ref[rows, h * DH:(h + 1) * DH] = o[h]


def _mixer_long_kernel(t, p_ref, c0_ref, n0_ref, m0_ref, s0_ref, gb_ref, al_ref,
                       ha_ref, hb_ref, c_ref, n_ref, m_ref, s_ref, *sc):
    g = pl.program_id(1)

    @pl.when(g == 0)
    def _():
        c_ref[...] = c0_ref[...]
        n_ref[...] = n0_ref[...]
        m_ref[...] = m0_ref[...]
        s_ref[...] = s0_ref[...]

    k = _chunk_consts(CH, CH)
    par = (gb_ref[...], al_ref[...])
    st = (c_ref, n_ref, _M0(m_ref), s_ref)

    def chunk_views(ci):
        rows = pl.ds(pl.multiple_of(ci * CH, CH), CH)
        return ci, (lambda c0, n: p_ref[rows, c0:c0 + n])

    def body_a(j, carry):
        _phase_a(CH, CH, k, [chunk_views(A_GROUP * j + c) for c in range(A_GROUP)], par, sc)
        return carry

    def body_b(j, carry):
        for c in range(B_UNROLL):
            ci = B_UNROLL * j + c
            rows = pl.ds(pl.multiple_of(ci * CH, CH), CH)
            _phase_b(CH, k, lambda c0, n, rows=rows: p_ref[rows, c0:c0 + n], par, st, sc, (ha_ref, hb_ref, rows), ci)
        return carry

    lax.fori_loop(0, t // (CH * A_GROUP), body_a, 0)
    lax.fori_loop(0, t // (CH * B_UNROLL), body_b, 0)


class _M0:
    def __init__(self, ref):
        self.ref = ref

    def __getitem__(self, idx):
        return self.ref[0] if idx is Ellipsis else self.ref[(0,) + idx]

    def __setitem__(self, idx, val):
        if idx is Ellipsis:
            self.ref[0] = val
        else:
            self.ref[(0,) + idx] = val


def _mixer_long(p, c0, n0, m0, s0, gb, al, nb, seq, t):
    nt = seq // t
    row_spec = lambda w: pl.BlockSpec((t, w), lambda b, g: (b * nt + g, 0))
    const = lambda shape: pl.BlockSpec(shape, lambda b, g: (0,) * len(shape))
    perb = lambda shape: pl.BlockSpec((1,) + shape, lambda b, g: (b,) + (0,) * len(shape))
    return pl.pallas_call(
        functools.partial(_mixer_long_kernel, t),
        grid=(nb, nt),
        in_specs=[row_spec(N_MIX),
                  const((1, NH, DH, DH)), const((1, NH, DH)), const((1, 8, 128)), const((1, NH, DH, DH)),
                  const((1, 128)), const((1, 128))],
        out_specs=[row_spec(WB), row_spec(WB),
                   perb((NH, DH, DH)), perb((NH, DH)), perb((8, 128)), perb((NH, DH, DH))],
        out_shape=[jax.ShapeDtypeStruct((nb * seq, WB), f32), jax.ShapeDtypeStruct((nb * seq, WB), f32),
                   jax.ShapeDtypeStruct((nb, NH, DH, DH), f32), jax.ShapeDtypeStruct((nb, NH, DH), f32),
                   jax.ShapeDtypeStruct((nb, 8, 128), f32), jax.ShapeDtypeStruct((nb, NH, DH, DH), f32)],
        scratch_shapes=_mix_scratch(t // CH, 1),
        compiler_params=_cparams(("arbitrary", "arbitrary")),
        name="mixer_long",
    )(p, c0, n0, m0, s0, gb, al)


def _mixer_block_kernel(blk, valid, n_alias, first_of, p_ref, c0_ref, n0_ref, m0_ref, s0_ref, gb_ref, al_ref,
                        *rest):
    ha_ref, hb_ref, c_ref, n_ref, m_ref, s_ref = rest[n_alias:n_alias + 6]
    sc = rest[n_alias + 6:]
    if first_of is not None:
        layer, depth = first_of
        for other in range(depth):
            if other != layer:
                c_ref[other] = jnp.zeros(c_ref.shape[1:], f32)
                s_ref[other] = jnp.zeros(s_ref.shape[1:], f32)
        c_ref, s_ref = c_ref.at[layer], s_ref.at[layer]
    c_ref[...] = c0_ref[...]
    n_ref[...] = n0_ref[...]
    m_ref[...] = m0_ref[...]
    s_ref[...] = s0_ref[...]
    k = _chunk_consts(blk, valid)
    par = (gb_ref[...], al_ref[...])
    pr = lambda c0, n: p_ref[:, c0:c0 + n]
    _phase_a(blk, valid, k, [(0, pr)], par, sc)
    _phase_b(blk, k, pr, par, (c_ref, n_ref, _M0(m_ref), s_ref), sc, (ha_ref, hb_ref, slice(0, CH)), 0)


def _mixer_block(p, c0, n0, m0, s0, gb, al, blk, valid, layer=None, prev=None):
    ng = p.shape[0] // CH
    nseq = CH // blk
    row_spec = lambda w: pl.BlockSpec((CH, w), lambda g: (g, 0))
    const = lambda shape: pl.BlockSpec(shape, lambda g: (0,) * len(shape))
    perg = lambda shape: pl.BlockSpec(shape, lambda g: (g,) + (0,) * (len(shape) - 1))
    big_shape = (ng * nseq, NH, DH, DH)
    first_of = None
    if layer is None:
        big = big_out = perg((nseq, NH, DH, DH))
    else:
        depth = c0.shape[0]
        big = big_out = pl.BlockSpec((None, nseq, NH, DH, DH), lambda g: (layer, g, 0, 0, 0))
        big_shape = (depth,) + big_shape
        if prev is None:
            first_of = (layer, depth)
            big_out = pl.BlockSpec((depth, nseq, NH, DH, DH), lambda g: (0, g, 0, 0, 0))
    aliased = () if prev is None else tuple(prev)
    n_in = 7
    return pl.pallas_call(
        functools.partial(_mixer_block_kernel, blk, valid, len(aliased), first_of),
        grid=(ng,),
        in_specs=[row_spec(N_MIX), big, perg((nseq, NH, DH)), perg((1, 8, 128)), big,
                  const((1, 128)), const((1, 128))]
                 + [pl.BlockSpec(memory_space=pl.ANY)] * len(aliased),
        out_specs=[row_spec(WB), row_spec(WB), big_out, perg((nseq, NH, DH)), perg((1, 8, 128)), big_out],
        out_shape=[jax.ShapeDtypeStruct((ng * CH, WB), f32), jax.ShapeDtypeStruct((ng * CH, WB), f32),
                   jax.ShapeDtypeStruct(big_shape, f32), jax.ShapeDtypeStruct((ng * nseq, NH, DH), f32),
                   jax.ShapeDtypeStruct((ng, 8, 128), f32), jax.ShapeDtypeStruct(big_shape, f32)],
        input_output_aliases={n_in: 2, n_in + 1: 5} if aliased else {},
        scratch_shapes=_mix_scratch(1, nseq),
        compiler_params=_cparams(("arbitrary",)),
        name="mixer_block",
    )(p, c0, n0, m0, s0, gb, al, *aliased)


def _outproj_kernel(x_ref, ha_ref, hb_ref, po_ref, pz_ref, anw_ref, bnw_ref,
                    wm_ref, wpa_ref, wpb_ref, wo_ref, g_ref, b_ref, o_ref):
    x = x_ref[...]
    mg = _sigmoid(_dot(x.astype(bf16), wm_ref[...]))
    cols = [slice(h * DH, (h + 1) * DH) for h in range(NH)]
    hr = [ha_ref[:, c] for c in cols]
    ob = [hb_ref[:, c] for c in cols]
    mu = [jnp.mean(v, axis=1, keepdims=True) for v in hr]
    osq = [jnp.mean(v * v, axis=1, keepdims=True) for v in ob]
    hc = [hr[h] - mu[h] for h in range(NH)]
    var = [jnp.mean(v * v, axis=1, keepdims=True) for v in hc]
    ha = [_sigmoid(po_ref[:, cols[h]]) * (hc[h] * lax.rsqrt(var[h] + NORM_EPS) * anw_ref[:, cols[h]])
          for h in range(NH)]
    hb = [(ob[h] * lax.rsqrt(osq[h] + NORM_EPS) * bnw_ref[...]) * _silu(pz_ref[:, cols[h]]) for h in range(NH)]
    ya = _dot(jnp.concatenate(ha, axis=1).astype(bf16), wpa_ref[...])
    yb = _dot(jnp.concatenate(hb, axis=1).astype(bf16), wpb_ref[...])
    y = mg[:, :D_MODEL] * ya + mg[:, D_MODEL:] * yb
    mix = _dot(y.astype(bf16), wo_ref[...])
    o_ref[...] = _ln_rows(ALPHA * x + mix, g_ref[...], b_ref[...])


def _outproj(x, ha, hb, p, anw, bnw, w_merge, layer, w_pa, w_pb, w_out, g, b):
    m = x.shape[0]
    tm = min(m, PROJ_TM)
    row_spec = lambda w: pl.BlockSpec((tm, w), lambda i: (i, 0))
    pcol = lambda c0: pl.BlockSpec((tm, WB), lambda i: (i, c0 // WB))
    const = lambda shape: pl.BlockSpec(shape, lambda i: (0, 0))
    return pl.pallas_call(
        _outproj_kernel,
        grid=(m // tm,),
        in_specs=[row_spec(D_MODEL), row_spec(WB), row_spec(WB), pcol(C_OA), pcol(C_Z),
                  const((1, WB)), const((1, DH)),
                  pl.BlockSpec((None, D_MODEL, 2 * D_MODEL), lambda i: (layer, 0, 0)),
                  const((WB, D_MODEL)), const((WB, D_MODEL)),
                  const((D_MODEL, D_MODEL)), const((1, D_MODEL)), const((1, D_MODEL))],
        out_specs=row_spec(D_MODEL),
        out_shape=jax.ShapeDtypeStruct((m, D_MODEL), f32),
        compiler_params=_cparams(("arbitrary",)),
        name="merge_outproj_ln",
    )(x, ha, hb, p, p, anw, bnw, w_merge, w_pa, w_pb, w_out, g, b)


def _ffn2_kernel(long_mode, tm, nmt, blk, valid, *refs):
    if long_mode:
        (x_ref, wa_ref, wb_ref, cwa_ref, cwb_ref, wd_ref, g_ref, b_ref, ha_ref, hb_ref,
         o_ref, sa_ref, sb_ref, hh_ref, ca_ref, cb_ref) = refs

        @pl.when(pl.program_id(0) % nmt == 0)
        def _():
            ca_ref[...] = ha_ref[0]
            cb_ref[...] = hb_ref[0]
    else:
        (x_ref, wa_ref, wb_ref, cwa_ref, cwb_ref, wd_ref, g_ref, b_ref, ea_ref, eb_ref,
         o_ref, sa_ref, sb_ref, hh_ref) = refs
        nsq = tm // blk
        r0 = blk - valid - (CONV_F - 1)
        row3 = lax.broadcasted_iota(jnp.int32, (nsq, blk, FT), 1)

        def merge_history(u, e_ref, s_ref, cols):
            u3 = u.reshape(nsq, blk, FT)
            e3 = e_ref[:, :, cols]
            for r in range(CONV_F - 1):
                u3 = jnp.where(row3 == r0 + r, e3[:, r:r + 1, :], u3)
            s_ref[:, :, cols] = u3[:, blk - (CONV_F - 1):, :]
            return u3.reshape(tm, FT)
    xb = x_ref[...].astype(bf16)
    for cb in range(NF):
        cols = slice(cb * FT, (cb + 1) * FT)
        ua = _dot(xb, wa_ref[:, cols])
        ub = _dot(xb, wb_ref[:, cols])
        if long_mode:
            h8a, h8b = ca_ref[:, cols], cb_ref[:, cols]
            ta, tb = ua[tm - HDR:tm], ub[tm - HDR:tm]
            ca_ref[:, cols] = ta
            cb_ref[:, cols] = tb
            sa_ref[0, :, cols] = ta
            sb_ref[0, :, cols] = tb
        else:
            h8a = h8b = None
            ua = merge_history(ua, ea_ref, sa_ref, cols)
            ub = merge_history(ub, eb_ref, sb_ref, cols)
        hh = _silu(_conv_rows(ua, h8a, cwa_ref[:, cols])) * _conv_rows(ub, h8b, cwb_ref[:, cols])
        hh_ref[:, cols] = hh.astype(bf16)
    out = _dot(hh_ref[...], wd_ref[...])
    o_ref[...] = _ln_rows(ALPHA * x_ref[...] + out, g_ref[...], b_ref[...])


def _ffn2(x, w_up, cw, w_down, layer, g, b, hist_a=None, hist_b=None,
          e_ab=None, nb=1, blk=CH, valid=CH):
    m = x.shape[0]
    long_mode = hist_a is not None
    seq = m // nb
    tm = min(seq, FFN_TM if long_mode else FFN_TM_BLOCK)
    nmt = seq // tm
    rows = lambda w: pl.BlockSpec((tm, w), lambda i: (i, 0))
    const = lambda shape: pl.BlockSpec(shape, lambda i: (0,) * len(shape))
    lsp = lambda shape, j: pl.BlockSpec((None,) + shape, lambda i: (layer, 0, j))
    in_specs = [rows(D_MODEL), lsp((D_MODEL, D_FF), 0), lsp((D_MODEL, D_FF), 1), lsp((CONV_F, D_FF), 0),
                lsp((CONV_F, D_FF), 1), lsp((D_FF, D_MODEL), 0), const((1, D_MODEL)), const((1, D_MODEL))]
    scratch = [pltpu.VMEM((tm, D_FF), bf16)]
    if long_mode:
        in_specs += [const((1, HDR, D_FF))] * 2
        st_spec = pl.BlockSpec((1, HDR, D_FF), lambda i: (i, 0, 0))
        st_shape = jax.ShapeDtypeStruct((m // tm, HDR, D_FF), f32)
        scratch += [pltpu.VMEM((HDR, D_FF), f32), pltpu.VMEM((HDR, D_FF), f32)]
        extra = (hist_a, hist_b)
    else:
        nsq = tm // blk
        if e_ab.ndim == 4:
            e_spec = lambda j: pl.BlockSpec((None, nsq, CONV_F - 1, D_FF), lambda i: (layer, i, 0, j))
        else:
            e_spec = lambda j: pl.BlockSpec((nsq, CONV_F - 1, D_FF), lambda i: (i, 0, j))
        in_specs += [e_spec(0), e_spec(1)]
        st_spec = pl.BlockSpec((nsq, CONV_F - 1, D_FF), lambda i: (i, 0, 0))
        st_shape = jax.ShapeDtypeStruct((m // blk, CONV_F - 1, D_FF), f32)
        extra = (e_ab, e_ab)
    return pl.pallas_call(
        functools.partial(_ffn2_kernel, long_mode, tm, nmt, blk, valid),
        grid=(m // tm,),
        in_specs=in_specs,
        out_specs=[rows(D_MODEL), st_spec, st_spec],
        out_shape=[jax.ShapeDtypeStruct((m, D_MODEL), f32), st_shape, st_shape],
        scratch_shapes=scratch,
        compiler_params=_cparams(("arbitrary",)),
        name="conv_ffn_ln",
    )(x, w_up, w_up, cw, cw, w_down, g, b, *extra)


def _mix_weights(w_in):
    w = w_in.astype(bf16)
    zc = jnp.zeros(w.shape[:2] + (128 - 4 * NH,), bf16)
    w_mix = jnp.concatenate([w[..., A_Q:A_I], w[..., B_Z:B_BETA], w[..., A_I:B_QKV], w[..., B_BETA:G_MERGE], zc,
                             w[..., B_QKV:B_Z]], axis=-1)
    return w_mix, w[..., G_MERGE:]


def _layer_weights(l, w_mix, w_merge, mlstm_gate_bias, mlstm_norm_w, gdn_conv_w, gdn_A_log, gdn_dt_bias, gdn_norm_w,
                   w_branch_a, w_branch_b, w_out, ln1_g, ln1_b, w_up, ffn_conv_w, w_down, ln2_g, ln2_b):
    z4 = jnp.zeros((NH,), f32)
    gb = jnp.concatenate([mlstm_gate_bias[l], z4, gdn_dt_bias[l], jnp.zeros((128 - 4 * NH,), f32)])[None]
    al = jnp.concatenate([z4, z4, z4, gdn_A_log[l], jnp.zeros((128 - 4 * NH,), f32)])[None]
    return dict(
        layer=l, w_mix=w_mix, w_merge=w_merge, gb=gb, al=al,
        anw=mlstm_norm_w[l][None], bnw=gdn_norm_w[l][None], cw=gdn_conv_w[l],
        w_pa=w_branch_a[l].astype(bf16), w_pb=w_branch_b[l].astype(bf16), w_out=w_out[l].astype(bf16),
        ln1_g=ln1_g[l][None], ln1_b=ln1_b[l][None],
        w_up=w_up, cw_f=ffn_conv_w, w_down=w_down, ln2_g=ln2_g[l][None], ln2_b=ln2_b[l][None])


def _block_layer(x, lw, st, blk, valid, layer=None, prev=None, ln=None):
    c0, n0, m0, s0, gbuf, fbuf = st
    nseq_tot = n0.shape[0]
    nseq = CH // blk
    ng = nseq_tot // nseq
    m0p = jnp.pad(m0.reshape(ng, nseq, NH), ((0, 0), (0, 8 - nseq), (0, 128 - NH)))
    if ln is None:
        p, ext = _inproj(x, lw["w_mix"], lw["layer"], lw["cw"], e=gbuf, blk=blk, valid=valid)
    else:
        p, ext, x = _inproj(x, lw["w_mix"], lw["layer"], lw["cw"], e=gbuf, blk=blk, valid=valid, ln=ln)
    ha, hb, c, n, m, s = _mixer_block(p, c0, n0, m0p, s0, lw["gb"], lw["al"], blk, valid, layer, prev)
    x1 = _outproj(x, ha, hb, p, lw["anw"], lw["bnw"], lw["w_merge"], lw["layer"], lw["w_pa"], lw["w_pb"],
                  lw["w_out"],
                  lw["ln1_g"], lw["ln1_b"])
    x2, ua, ub = _ffn2(x1, lw["w_up"], lw["cw_f"], lw["w_down"], lw["layer"],
                       lw["ln2_g"], lw["ln2_b"], e_ab=fbuf, blk=blk, valid=valid)
    m_new = m[:, :nseq, :NH].reshape(nseq_tot, NH)
    gconv = ext
    fconv = jnp.concatenate([ua, ub], axis=2)
    return x2, (c, n, m_new, s, gconv, fconv)


def _long_layer(x, lw, st, nb, seq, ln=None):
    c0, n0, m0, s0, gbuf, fbuf = st
    m0p = jnp.pad(m0.reshape(1, 1, NH), ((0, 0), (0, 7), (0, 128 - NH)))
    hist = jnp.pad(gbuf, ((0, 0), (HDR - (CONV_B - 1), 0), (0, 0)))
    hf = jnp.pad(fbuf, ((0, 0), (HDR - (CONV_F - 1), 0), (0, 0)))
    if ln is None:
        p, tails = _inproj(x, lw["w_mix"], lw["layer"], lw["cw"], hist=hist, nb=nb)
    else:
        p, tails, x = _inproj(x, lw["w_mix"], lw["layer"], lw["cw"], hist=hist, nb=nb, ln=ln)
    ha, hb, c, n, m, s = _mixer_long(p, c0, n0, m0p, s0, lw["gb"], lw["al"], nb, seq, MIX_T)
    x1 = _outproj(x, ha, hb, p, lw["anw"], lw["bnw"], lw["w_merge"], lw["layer"], lw["w_pa"], lw["w_pb"],
                  lw["w_out"],
                  lw["ln1_g"], lw["ln1_b"])
    x2, sa, sb = _ffn2(x1, lw["w_up"], lw["cw_f"], lw["w_down"], lw["layer"],
                       lw["ln2_g"], lw["ln2_b"], hist_a=hf[:, :, :D_FF], hist_b=hf[:, :, D_FF:], nb=nb)
    last_tile = lambda a: a.reshape((nb, -1) + a.shape[1:])[:, -1]
    gconv = last_tile(tails)[:, HDR - (CONV_B - 1):]
    fconv = jnp.concatenate([last_tile(sa), last_tile(sb)], axis=2)[:, HDR - (CONV_F - 1):]
    return x2, (c, n, m[:, 0, :NH], s, gconv, fconv)


def kernel(x_prompt, x_sample, state_mlstm_C, state_mlstm_n, state_mlstm_m, state_gdn_S, state_gdn_conv, state_ffn_conv, meta_tokens, ln_emb_g, ln_emb_b, w_in, mlstm_gate_bias, mlstm_norm_w, gdn_conv_w, gdn_A_log, gdn_dt_bias, gdn_norm_w, w_branch_a, w_branch_b, w_out, ln1_g, ln1_b, w_up, ffn_conv_w, w_down, ln2_g, ln2_b):
    nb, seq, _ = x_prompt.shape
    ns, ls, _ = x_sample.shape
    sblk = 8
    assert seq % max(MIX_T, PROJ_TM, FFN_TM) == 0 and ls + CONV_B - 1 <= sblk and ns % (CH // sblk) == 0 and N_META + CONV_B - 1 <= CH
    w_mix, w_merge = _mix_weights(w_in)
    w_up_b16, w_down_b16 = w_up.astype(bf16), w_down.astype(bf16)
    lws =[_layer_weights(l, w_mix, w_merge, mlstm_gate_bias, mlstm_norm_w, gdn_conv_w, gdn_A_log, gdn_dt_bias,
                          gdn_norm_w, w_branch_a, w_branch_b, w_out, ln1_g, ln1_b, w_up_b16, ffn_conv_w,
                          w_down_b16, ln2_g, ln2_b) for l in range(DEPTH)]
    emb_ln = (ln_emb_g[None], ln_emb_b[None])

    xm = jnp.pad(meta_tokens, ((CH - N_META, 0), (0, 0)))
    xs = jnp.pad(x_sample, ((0, 0), (sblk - ls, 0), (0, 0))).reshape(ns * sblk, D_MODEL)
    xp = x_prompt.reshape(nb * seq, D_MODEL)

    zero_st = (jnp.zeros((1, NH, DH, DH), f32), jnp.zeros((1, NH, DH), f32), jnp.zeros((1, NH), f32),
               jnp.zeros((1, NH, DH, DH), f32), jnp.zeros((1, CONV_B - 1, 3 * WB), f32),
               jnp.zeros((1, CONV_F - 1, 2 * D_FF), f32))
    p_states, s_states = [], []
    big = None
    for l in range(DEPTH):
        ln = emb_ln if l == 0 else None
        xm, st_m = _block_layer(xm, lws[l], zero_st, CH, N_META, ln=ln)
        xp, st_p = _long_layer(xp, lws[l], st_m, nb, seq, ln=ln)
        samp_st = (state_mlstm_C, state_mlstm_n[l], state_mlstm_m[l], state_gdn_S,
                   state_gdn_conv[l], state_ffn_conv)
        xs, st_s = _block_layer(xs, lws[l], samp_st, sblk, ls, layer=l, prev=big, ln=ln)
        big = (st_s[0], st_s[3])
        p_states.append(st_p)
        s_states.append(st_s)

    stack = lambda states, i: jnp.stack([s[i] for s in states], axis=0)
    y_prompt = xp.reshape(nb, seq, D_MODEL)
    y_sample = xs.reshape(ns, sblk, D_MODEL)[:, sblk - ls:]
    return (y_prompt, y_sample,
            stack(p_states, 0), stack(p_states, 1), stack(p_states, 2), stack(p_states, 3),
            stack(p_states, 4), stack(p_states, 5),
            big[0], stack(s_states, 1), stack(s_states, 2), big[1],
            stack(s_states, 4), stack(s_states, 5))
```

```python
import functools

import jax
import jax.numpy as jnp
from jax import lax
from jax.experimental import pallas as pl
from jax.experimental.pallas import tpu as pltpu

f32 = jnp.float32
bf16 = jnp.bfloat16

D_MODEL = 1024
N_META = 16
NH = 4
DH = 128
WB = NH * DH
CONV_B = 4
D_FF = 2816
CONV_F = 3
DEPTH = 2
ALPHA = (2 * DEPTH) ** 0.25
LN_EPS = 1e-5
NORM_EPS = 1e-6
QSCALE = DH ** -0.5

A_Q = 0
A_I = 4 * WB
B_QKV = A_I + 2 * NH
B_Z = B_QKV + 3 * WB
B_BETA = B_Z + WB
G_MERGE = B_BETA + 2 * NH

C_QA, C_KA, C_VA, C_OA = 0, WB, 2 * WB, 3 * WB
C_Z = 4 * WB
C_G = 5 * WB
C_QKV = C_G + 128
N_MIX = C_QKV + 3 * WB

CH = 64
SUB = 16
A_GROUP = 8
B_UNROLL = 2
HDR = 8
NEG = -1e30
FT = 256
NF = D_FF // FT
MIX_T = 512
PROJ_TM = 512
FFN_TM = 512
FFN_TM_BLOCK = 256
VMEM_LIMIT = 56 * 1024 * 1024


def _cparams(sem):
    return pltpu.CompilerParams(dimension_semantics=sem, vmem_limit_bytes=VMEM_LIMIT)


def _dot(a, b):
    return jnp.dot(a, b, preferred_element_type=f32)


def _dot_nt(a, b):
    return lax.dot_general(a, b, (((1,), (1,)), ((), ())), preferred_element_type=f32)


def _dot_tn(a, b):
    return lax.dot_general(a, b, (((0,), (0,)), ((), ())), preferred_element_type=f32)


def _sel_dot(sel, x):
    hi = x.astype(bf16)
    r1 = x - hi.astype(f32)
    mid = r1.astype(bf16)
    lo = (r1 - mid.astype(f32)).astype(bf16)
    return (_dot(sel, hi) + _dot(sel, mid)) + _dot(sel, lo)


def _softplus(x):
    return jnp.maximum(x, 0.0) + jnp.log1p(jnp.exp(-jnp.abs(x)))


def _sigmoid(x):
    return 0.5 * jnp.tanh(0.5 * x) + 0.5


def _silu(x):
    t = 0.5 * x
    return t * (jnp.tanh(t) + 1.0)


def _ln_rows(x, g, b):
    mu = jnp.mean(x, axis=-1, keepdims=True)
    xc = x - mu
    var = jnp.mean(xc * xc, axis=-1, keepdims=True)
    return xc * lax.rsqrt(var + LN_EPS) * g + b


def _conv_rows(u, h8, cw):
    width = cw.shape[0]
    row = lax.broadcasted_iota(jnp.int32, (HDR, u.shape[1]), 0) if h8 is not None else None
    acc = None
    for j in range(width):
        sh = width - 1 - j
        term = u if sh == 0 else pltpu.roll(u, sh, 0)
        if h8 is not None and sh > 0:
            head = term[0:HDR]
            for r in range(sh):
                head = jnp.where(row == r, h8[HDR - sh + r:HDR - sh + r + 1], head)
            term = jnp.concatenate([head, term[HDR:]], axis=0)
        term = term * cw[j:j + 1]
        acc = term if acc is None else acc + term
    return acc


def _inproj_kernel(long_mode, with_ln, tm, nmt, blk, valid, *refs):
    if with_ln:
        g_ref, b_ref, xln_ref = refs[4], refs[5], refs[6]
        refs = refs[:4] + refs[7:]
    if long_mode:
        x_ref, w_ref, cw_ref, hist_ref, o_ref, tail_ref, carry_ref = refs

        @pl.when(pl.program_id(0) % nmt == 0)
        def _():
            carry_ref[...] = hist_ref[0]
    else:
        x_ref, w_ref, cw_ref, e_ref, o_ref, tail_ref = refs
        nsq = tm // blk
        r0 = blk - valid - (CONV_B - 1)
        row3 = lax.broadcasted_iota(jnp.int32, (nsq, blk, 2 * DH), 1)
    if with_ln:
        xln = _ln_rows(x_ref[...], g_ref[...], b_ref[...])
        xln_ref[...] = xln
        xb = xln.astype(bf16)
    else:
        xb = x_ref[...].astype(bf16)
    nstep = 3 * NH // 2
    cuts = [C_QA + WB] + [C_KA + (C_QKV - C_KA) * (i + 1) // (nstep - 1) // 128 * 128 for i in range(nstep - 1)]
    cuts[-1] = C_QKV
    for cb in range(nstep):
        cols = slice(cb * 2 * DH, (cb + 1) * 2 * DH)
        u = _dot(xb, w_ref[:, C_QKV + cb * 2 * DH:C_QKV + (cb + 1) * 2 * DH])
        lo = C_QA if cb == 0 else cuts[cb - 1]
        plain = _dot(xb, w_ref[:, lo:cuts[cb]])
        o_ref[:, lo:cuts[cb]] = plain * QSCALE if cb == 0 else plain
        if long_mode:
            h8 = carry_ref[:, cols]
            t8 = u[tm - HDR:tm]
            carry_ref[:, cols] = t8
            tail_ref[0, :, cols] = t8
        else:
            h8 = None
            u3 = u.reshape(nsq, blk, 2 * DH)
            e3 = e_ref[:, :, cols]
            for r in range(CONV_B - 1):
                u3 = jnp.where(row3 == r0 + r, e3[:, r:r + 1, :], u3)
            tail_ref[:, :, cols] = u3[:, blk - (CONV_B - 1):, :]
            u = u3.reshape(tm, 2 * DH)
        c = _silu(_conv_rows(u, h8, cw_ref[:, cols]))
        for half in range(2):
            ch = c[:, half * DH:(half + 1) * DH]
            if cb < NH:
                ch = ch * lax.rsqrt(jnp.sum(ch * ch, axis=1, keepdims=True) + NORM_EPS)
            if cb < NH // 2:
                ch = ch * QSCALE
            c0 = C_QKV + (2 * cb + half) * DH
            o_ref[:, c0:c0 + DH] = ch


def _inproj(x, w_mix, layer, cw, hist=None, e=None, nb=1, blk=CH, valid=CH, ln=None):
    m = x.shape[0]
    long_mode = hist is not None
    seq = m // nb
    tm = min(seq, PROJ_TM)
    nmt = seq // tm
    rows = lambda w: pl.BlockSpec((tm, w), lambda i: (i, 0))
    const = lambda shape: pl.BlockSpec(shape, lambda i: (0,) * len(shape))
    in_specs = [rows(D_MODEL), pl.BlockSpec((None, D_MODEL, N_MIX), lambda i: (layer, 0, 0)),
                const((CONV_B, 3 * WB))]
    if long_mode:
        in_specs += [const((1, HDR, 3 * WB))]
        tail_spec = pl.BlockSpec((1, HDR, 3 * WB), lambda i: (i, 0, 0))
        tail_shape = jax.ShapeDtypeStruct((m // tm, HDR, 3 * WB), f32)
        scratch = [pltpu.VMEM((HDR, 3 * WB), f32)]
        extra = hist
    else:
        nsq = tm // blk
        in_specs += [pl.BlockSpec((nsq, CONV_B - 1, 3 * WB), lambda i: (i, 0, 0))]
        tail_spec = pl.BlockSpec((nsq, CONV_B - 1, 3 * WB), lambda i: (i, 0, 0))
        tail_shape = jax.ShapeDtypeStruct((m // blk, CONV_B - 1, 3 * WB), f32)
        scratch = []
        extra = e
    args = [x, w_mix, cw, extra]
    out_specs = [rows(N_MIX), tail_spec]
    out_shape = [jax.ShapeDtypeStruct((m, N_MIX), f32), tail_shape]
    if ln is not None:
        in_specs += [const((1, D_MODEL))] * 2
        args += list(ln)
        out_specs = [rows(D_MODEL)] + out_specs
        out_shape = [jax.ShapeDtypeStruct((m, D_MODEL), f32)] + out_shape
    res = pl.pallas_call(
        functools.partial(_inproj_kernel, long_mode, ln is not None, tm, nmt, blk, valid),
        grid=(m // tm,),
        in_specs=in_specs,
        out_specs=out_specs,
        out_shape=out_shape,
        scratch_shapes=scratch,
        compiler_params=_cparams(("arbitrary",)),
        name="mixer_inproj",
    )(*args)
    return res if ln is None else (res[1], res[2], res[0])


def _chunk_consts(blk, valid):
    lg = blk.bit_length() - 1
    r = lax.broadcasted_iota(jnp.int32, (CH, CH), 0)
    c = lax.broadcasted_iota(jnp.int32, (CH, CH), 1)
    same = (r >> lg) == (c >> lg)
    tri = jnp.logical_and(same, c <= r)
    stri = jnp.logical_and(same, c < r)
    sel = lambda cond: jnp.where(cond, 1.0, 0.0).astype(bf16)
    tri_m = sel(tri)
    blk_m = sel(same)
    re = lax.broadcasted_iota(jnp.int32, (CH, 8), 0)
    ce = lax.broadcasted_iota(jnp.int32, (CH, 8), 1)
    expand_m = sel(ce == (re >> lg))
    rs = lax.broadcasted_iota(jnp.int32, (8, CH), 0)
    cs = lax.broadcasted_iota(jnp.int32, (8, CH), 1)
    rowsel_m = sel(cs == (rs << lg) + (blk - 1))
    rv = lax.broadcasted_iota(jnp.int32, (CH, 128), 0)
    valid_m = (rv & (blk - 1)) >= (blk - valid)
    lgs = min(SUB, blk).bit_length() - 1
    same_sub = (r >> lgs) == (c >> lgs)
    eye = jnp.where(r == c, 1.0, 0.0)
    couple = {}
    s = 1
    while s < min(SUB, blk):
        lg2 = s.bit_length()
        couple[s] = jnp.logical_and((r >> lg2) == (c >> lg2),
                                    jnp.logical_and((r & (2 * s - 1)) >= s, (c & (2 * s - 1)) < s))
        s *= 2
    return dict(tri=tri, stri=stri, tri_m=tri_m, blk_m=blk_m, same_sub=same_sub, eye=eye, couple=couple,
                expand_m=expand_m, rowsel_m=rowsel_m, valid_m=valid_m)


def _block_max(x, blk):
    if blk == CH:
        return jnp.broadcast_to(jnp.max(x, axis=0, keepdims=True), x.shape)
    x3 = x.reshape(CH // blk, blk, 128)
    return jnp.broadcast_to(jnp.max(x3, axis=1, keepdims=True), x3.shape).reshape(CH, 128)


def _mix_scratch(nc, nseq):
    return [pltpu.VMEM((nc, 6, CH, 128), f32),
            pltpu.VMEM((nc, NH, CH, DH), f32),
            pltpu.VMEM((nc, nseq * NH, DH, DH), f32),
            pltpu.VMEM((nc, nseq, 8, DH), f32),
            pltpu.VMEM((nc, nseq * NH, DH, DH), bf16),
            pltpu.VMEM((nc, nseq * NH, DH, DH), f32),
            pltpu.VMEM((nc, NH, CH, DH), bf16),
            pltpu.VMEM((nc, NH, CH, DH), f32)]


def _phase_a(blk, valid, k, chunks, par, sc):
    nseq = CH // blk
    gbias, alog = par[0], par[1]
    t_ref, numl_ref, kv_ref, nv_ref, m1_ref, m2_ref, qeff_ref, o2_ref = sc
    tri, stri = k["tri"], k["stri"]
    seqs = [slice(i * blk, (i + 1) * blk) for i in range(nseq)]
    nch = len(chunks)
    units = [(c, h) for c in range(nch) for h in range(NH)]
    lane = lax.broadcasted_iota(jnp.int32, (CH, 128), 1)
    cum_lane = jnp.logical_or(jnp.logical_and(lane >= 4, lane < 8), lane >= 12)

    gt = []
    for ci, pr in chunks:
        graw = pr(C_G, 128) + gbias
        g = jnp.where(lane < 4, graw,
                      jnp.where(lane < 8, -_softplus(-graw),
                                jnp.where(lane < 12, _sigmoid(graw),
                                          jnp.where(lane < 16, -jnp.exp(alog) * _softplus(graw), 0.0))))
        if valid < blk:
            g = jnp.where(k["valid_m"], g, jnp.where(lane < 4, NEG, 0.0))
        gt.append(g)
    cs = [_sel_dot(k["tri_m"], g) for g in gt]
    if nseq == 1:
        tot = [jnp.broadcast_to(x[CH - 1:CH, :], (CH, 128)) for x in cs]
    else:
        tot = [_sel_dot(k["blk_m"], g) for g in gt]
    mix_t = [jnp.transpose(jnp.where(cum_lane, cs[c], gt[c])) for c in range(nch)]
    b_all = [pltpu.roll(x, 124, 1) for x in cs]
    btot_all = [pltpu.roll(x, 124, 1) for x in tot]
    wlog = [btot_all[c] - b_all[c] + gt[c] for c in range(nch)]
    mloc_all = [_block_max(x, blk) for x in wlog]
    wsrc_all = [jnp.exp(wlog[c] - mloc_all[c]) for c in range(nch)]
    expg_all = [jnp.exp(x) for x in cs]
    expdiff_all = [jnp.exp(tot[c] - cs[c]) for c in range(nch)]

    qa, ka, va, qg, kgf, vg = {}, {}, {}, {}, {}, {}
    for c, h in units:
        _, pr = chunks[c]
        qa[c, h] = pr(C_QA + h * DH, DH).astype(bf16)
        ka[c, h] = pr(C_KA + h * DH, DH)
        va[c, h] = pr(C_VA + h * DH, DH).astype(bf16)
        qg[c, h] = pr(C_QKV + h * DH, DH)
        kgf[c, h] = pr(C_QKV + WB + h * DH, DH)
        vg[c, h] = pr(C_QKV + 2 * WB + h * DH, DH)
    qgb = {u: qg[u].astype(bf16) for u in units}
    kg = {u: kgf[u].astype(bf16) for u in units}

    sq = {u: _dot_nt(qa[u], ka[u].astype(bf16)) for u in units}
    kk = {u: _dot_nt(kg[u], kg[u]) for u in units}
    qk = {u: _dot_nt(qgb[u], kg[u]) for u in units}

    dm = {(c, h): jnp.where(tri, cs[c][:, 4 + h:5 + h] - mix_t[c][4 + h:5 + h, :] + mix_t[c][h:h + 1, :], NEG)
          for c, h in units}
    dmax = {u: jnp.max(dm[u], axis=1, keepdims=True) for u in units}
    s = {u: sq[u] * jnp.exp(dm[u] - dmax[u]) for u in units}
    denl = {u: jnp.sum(s[u], axis=1, keepdims=True) for u in units}
    sloc = {u: s[u].astype(bf16) for u in units}
    kwl = {(c, h): ka[c, h] * wsrc_all[c][:, h:h + 1] for c, h in units}
    for c in range(nch):
        ci = chunks[c][0]
        dmax_all = jnp.zeros((CH, 128), f32)
        denl_all = jnp.zeros((CH, 128), f32)
        for h in range(NH):
            dmax_all = jnp.where(lane == h, dmax[c, h], dmax_all)
            denl_all = jnp.where(lane == h, denl[c, h], denl_all)
        t_ref[ci, 0] = gt[c]
        t_ref[ci, 1] = b_all[c]
        t_ref[ci, 2] = btot_all[c]
        t_ref[ci, 3] = dmax_all
        t_ref[ci, 4] = denl_all
        t_ref[ci, 5] = mloc_all[c]

    xs, pw, qkd, kd = {}, {}, {}, {}
    for c, h in units:
        beta = gt[c][:, 8 + h:9 + h]
        decay = jnp.exp(jnp.where(tri, cs[c][:, 12 + h:13 + h] - mix_t[c][12 + h:13 + h, :], NEG))
        pw[c, h] = jnp.where(stri, beta * kk[c, h] * decay, 0.0)
        xs[c, h] = jnp.concatenate([beta * vg[c, h], (beta * expg_all[c][:, 12 + h:13 + h]) * kgf[c, h]], axis=1)
        qkd[c, h] = (qk[c, h] * decay).astype(bf16)
        kd[c, h] = (kgf[c, h] * expdiff_all[c][:, 12 + h:13 + h]).astype(bf16)

    for c, h in units:
        ci = chunks[c][0]
        numl_ref[ci, h] = _dot(sloc[c, h], va[c, h])
        kwb = kwl[c, h].astype(bf16)
        for i, sl in enumerate(seqs):
            kv_ref[ci, i * NH + h] = _dot_tn(kwb[sl], va[c, h][sl])
            nv_ref[ci, i, h:h + 1, :] = jnp.sum(kwl[c, h][sl], axis=0, keepdims=True)

    sub = min(SUB, blk)
    a_d = {u: jnp.where(k["same_sub"], pw[u], 0.0) for u in units}
    tinv = {u: k["eye"] - jnp.where(k["couple"][1], pw[u], 0.0) for u in units}
    s = 2
    while s < sub:
        tcur = {u: tinv[u].astype(bf16) for u in units}
        bd = {u: _dot(jnp.where(k["couple"][s], pw[u], 0.0).astype(bf16), tcur[u]) for u in units}
        tinv = {u: tinv[u] - _dot(tcur[u], bd[u].astype(bf16)) for u in units}
        s *= 2
    tb = {u: tinv[u].astype(bf16) for u in units}
    if sub == blk:
        xs = {u: _dot(tb[u], xs[u].astype(bf16)) for u in units}
    else:
        a_off = {u: (pw[u] - a_d[u]).astype(bf16) for u in units}
        nsub = CH // sub
        done = {u: [] for u in units}
        for j in range(nsub):
            rows = slice(j * sub, (j + 1) * sub)
            pad = lambda parts, n: parts + ([jnp.zeros((n, 2 * DH), f32)] if n else [])
            v = {u: xs[u][rows] for u in units}
            if j > 0:
                xcat = {u: jnp.concatenate(pad(done[u], CH - j * sub), axis=0).astype(bf16) for u in units}
                v = {u: v[u] - _dot(a_off[u][rows], xcat[u]) for u in units}
            vcat = {u: jnp.concatenate(pad(pad([], j * sub) + [v[u]], CH - (j + 1) * sub), axis=0).astype(bf16)
                    for u in units}
            for u in units:
                done[u].append(_dot(tb[u][rows], vcat[u]))
        xs = {u: jnp.concatenate(done[u], axis=0) for u in units}

    xb = {u: xs[u].astype(bf16) for u in units}
    qx = {u: _dot(qkd[u], xb[u]) for u in units}
    for c, h in units:
        ci = chunks[c][0]
        o2_ref[ci, h] = qx[c, h][:, :DH]
        qeff_ref[ci, h] = (expg_all[c][:, 12 + h:13 + h] * qg[c, h] - qx[c, h][:, DH:]).astype(bf16)
        for i, sl in enumerate(seqs):
            mm = _dot_tn(kd[c, h][sl], xb[c, h][sl])
            m2_ref[ci, i * NH + h] = mm[:, :DH]
            m1_ref[ci, i * NH + h] = mm[:, DH:].astype(bf16)


def _phase_b(blk, k, pr, par, st, sc, out, ci):
    nseq = CH // blk
    c_ref, n_ref, m_ref, s_ref = st
    t_ref, numl_ref, kv_ref, nv_ref, m1_ref, m2_ref, qeff_ref, o2_ref = sc
    ha_ref, hb_ref, rows = out
    seqs = [slice(i * blk, (i + 1) * blk) for i in range(nseq)]
    heads = range(NH)
    gt, b_all, btot_all = t_ref[ci, 0], t_ref[ci, 1], t_ref[ci, 2]
    dmax_all, denl_all, mloc_all = t_ref[ci, 3], t_ref[ci, 4], t_ref[ci, 5]

    qa = [pr(C_QA + h * DH, DH) for h in heads]
    qc, ms, oq = [], [], []
    for h in heads:
        qb = qa[h].astype(bf16)
        qc.append([_dot(qb[sl], c_ref[i, h].astype(bf16)) for i, sl in enumerate(seqs)])
    for h in heads:
        qe = qeff_ref[ci, h]
        ms_h, oq_h = [], []
        for i, sl in enumerate(seqs):
            sb = s_ref[i, h].astype(bf16)
            ms_h.append(_dot(m1_ref[ci, i * NH + h], sb))
            oq_h.append(_dot(qe[sl], sb))
        ms.append(ms_h)
        oq.append(oq_h)

    if nseq == 1:
        mprev_all = jnp.broadcast_to(m_ref[0:1, :], (CH, 128))
    else:
        mprev_all = _sel_dot(k["expand_m"], m_ref[...])
    inter = b_all + mprev_all
    mt = jnp.maximum(inter, dmax_all)
    scale_all = jnp.exp(dmax_all - mt)
    winter_all = jnp.exp(inter - mt)
    emt_all = jnp.exp(-mt)
    carry = btot_all + mprev_all
    mnew = jnp.maximum(carry, mloc_all)
    sc2_all = jnp.exp(mloc_all - mnew)
    wold_all = jnp.exp(carry - mnew)
    egt_all = jnp.exp(btot_all)
    if nseq == 1:
        m_ref[0:1, :] = mnew[CH - 1:CH, :]
    else:
        m_ref[...] = _sel_dot(k["rowsel_m"], mnew)

    cat = lambda parts: parts[0] if nseq == 1 else jnp.concatenate(parts, axis=0)
    qn = [cat([jnp.sum(qa[h][sl] * n_ref[i, h:h + 1, :], axis=1, keepdims=True) for i, sl in enumerate(seqs)])
          for h in heads]
    hraw = []
    for h in heads:
        scale, winter = scale_all[:, h:h + 1], winter_all[:, h:h + 1]
        den = scale * denl_all[:, h:h + 1] + winter * qn[h]
        inv = 1.0 / jnp.maximum(jnp.abs(den), emt_all[:, h:h + 1])
        hraw.append((scale * inv) * numl_ref[ci, h] + (winter * inv) * cat(qc[h]))
    o = [cat(oq[h]) + o2_ref[ci, h] for h in heads]

    for h in heads:
        for i in range(nseq):
            last = (i + 1) * blk - 1
            wold = wold_all[last:last + 1, h:h + 1]
            sc2 = sc2_all[last:last + 1, h:h + 1]
            c_ref[i, h] = wold * c_ref[i, h] + sc2 * kv_ref[ci, i * NH + h]
            n_ref[i, h:h + 1, :] = wold * n_ref[i, h:h + 1, :] + sc2 * nv_ref[ci, i, h:h + 1, :]
            s_ref[i, h] = (egt_all[last:last + 1, 8 + h:9 + h] * s_ref[i, h] - ms[h][i]) + m2_ref[ci, i * NH + h]

    for h in heads:
        ha_ref[rows, h * DH:(h + 1) * DH] = hraw[h]
        hb_ref[rows, h * DH:(h + 1) * DH] = o[h]


def _mixer_long_kernel(t, p_ref, c0_ref, n0_ref, m0_ref, s0_ref, gb_ref, al_ref,
                       ha_ref, hb_ref, c_ref, n_ref, m_ref, s_ref, *sc):
    g = pl.program_id(1)

    @pl.when(g == 0)
    def _():
        c_ref[...] = c0_ref[...]
        n_ref[...] = n0_ref[...]
        m_ref[...] = m0_ref[...]
        s_ref[...] = s0_ref[...]

    k = _chunk_consts(CH, CH)
    par = (gb_ref[...], al_ref[...])
    st = (c_ref, n_ref, _M0(m_ref), s_ref)

    def chunk_views(ci):
        rows = pl.ds(pl.multiple_of(ci * CH, CH), CH)
        return ci, (lambda c0, n: p_ref[rows, c0:c0 + n])

    def body_a(j, carry):
        _phase_a(CH, CH, k, [chunk_views(A_GROUP * j + c) for c in range(A_GROUP)], par, sc)
        return carry

    def body_b(j, carry):
        for c in range(B_UNROLL):
            ci = B_UNROLL * j + c
            rows = pl.ds(pl.multiple_of(ci * CH, CH), CH)
            _phase_b(CH, k, lambda c0, n, rows=rows: p_ref[rows, c0:c0 + n], par, st, sc, (ha_ref, hb_ref, rows), ci)
        return carry

    lax.fori_loop(0, t // (CH * A_GROUP), body_a, 0)
    lax.fori_loop(0, t // (CH * B_UNROLL), body_b, 0)


class _M0:
    def __init__(self, ref):
        self.ref = ref

    def __getitem__(self, idx):
        return self.ref[0] if idx is Ellipsis else self.ref[(0,) + idx]

    def __setitem__(self, idx, val):
        if idx is Ellipsis:
            self.ref[0] = val
        else:
            self.ref[(0,) + idx] = val


def _mixer_long(p, c0, n0, m0, s0, gb, al, nb, seq, t):
    nt = seq // t
    row_spec = lambda w: pl.BlockSpec((t, w), lambda b, g: (b * nt + g, 0))
    const = lambda shape: pl.BlockSpec(shape, lambda b, g: (0,) * len(shape))
    perb = lambda shape: pl.BlockSpec((1,) + shape, lambda b, g: (b,) + (0,) * len(shape))
    return pl.pallas_call(
        functools.partial(_mixer_long_kernel, t),
        grid=(nb, nt),
        in_specs=[row_spec(N_MIX),
                  const((1, NH, DH, DH)), const((1, NH, DH)), const((1, 8, 128)), const((1, NH, DH, DH)),
                  const((1, 128)), const((1, 128))],
        out_specs=[row_spec(WB), row_spec(WB),
                   perb((NH, DH, DH)), perb((NH, DH)), perb((8, 128)), perb((NH, DH, DH))],
        out_shape=[jax.ShapeDtypeStruct((nb * seq, WB), f32), jax.ShapeDtypeStruct((nb * seq, WB), f32),
                   jax.ShapeDtypeStruct((nb, NH, DH, DH), f32), jax.ShapeDtypeStruct((nb, NH, DH), f32),
                   jax.ShapeDtypeStruct((nb, 8, 128), f32), jax.ShapeDtypeStruct((nb, NH, DH, DH), f32)],
        scratch_shapes=_mix_scratch(t // CH, 1),
        compiler_params=_cparams(("arbitrary", "arbitrary")),
        name="mixer_long",
    )(p, c0, n0, m0, s0, gb, al)


def _mixer_block_kernel(blk, valid, n_alias, first_of, p_ref, c0_ref, n0_ref, m0_ref, s0_ref, gb_ref, al_ref,
                        *rest):
    ha_ref, hb_ref, c_ref, n_ref, m_ref, s_ref = rest[n_alias:n_alias + 6]
    sc = rest[n_alias + 6:]
    if first_of is not None:
        layer, depth = first_of
        for other in range(depth):
            if other != layer:
                c_ref[other] = jnp.zeros(c_ref.shape[1:], f32)
                s_ref[other] = jnp.zeros(s_ref.shape[1:], f32)
        c_ref, s_ref = c_ref.at[layer], s_ref.at[layer]
    c_ref[...] = c0_ref[...]
    n_ref[...] = n0_ref[...]
    m_ref[...] = m0_ref[...]
    s_ref[...] = s0_ref[...]
    k = _chunk_consts(blk, valid)
    par = (gb_ref[...], al_ref[...])
    pr = lambda c0, n: p_ref[:, c0:c0 + n]
    _phase_a(blk, valid, k, [(0, pr)], par, sc)
    _phase_b(blk, k, pr, par, (c_ref, n_ref, _M0(m_ref), s_ref), sc, (ha_ref, hb_ref, slice(0, CH)), 0)


def _mixer_block(p, c0, n0, m0, s0, gb, al, blk, valid, layer=None, prev=None):
    ng = p.shape[0] // CH
    nseq = CH // blk
    row_spec = lambda w: pl.BlockSpec((CH, w), lambda g: (g, 0))
    const = lambda shape: pl.BlockSpec(shape, lambda g: (0,) * len(shape))
    perg = lambda shape: pl.BlockSpec(shape, lambda g: (g,) + (0,) * (len(shape) - 1))
    big_shape = (ng * nseq, NH, DH, DH)
    first_of = None
    if layer is None:
        big = big_out = perg((nseq, NH, DH, DH))
    else:
        depth = c0.shape[0]
        big = big_out = pl.BlockSpec((None, nseq, NH, DH, DH), lambda g: (layer, g, 0, 0, 0))
        big_shape = (depth,) + big_shape
        if prev is None:
            first_of = (layer, depth)
            big_out = pl.BlockSpec((depth, nseq, NH, DH, DH), lambda g: (0, g, 0, 0, 0))
    aliased = () if prev is None else tuple(prev)
    n_in = 7
    return pl.pallas_call(
        functools.partial(_mixer_block_kernel, blk, valid, len(aliased), first_of),
        grid=(ng,),
        in_specs=[row_spec(N_MIX), big, perg((nseq, NH, DH)), perg((1, 8, 128)), big,
                  const((1, 128)), const((1, 128))]
                 + [pl.BlockSpec(memory_space=pl.ANY)] * len(aliased),
        out_specs=[row_spec(WB), row_spec(WB), big_out, perg((nseq, NH, DH)), perg((1, 8, 128)), big_out],
        out_shape=[jax.ShapeDtypeStruct((ng * CH, WB), f32), jax.ShapeDtypeStruct((ng * CH, WB), f32),
                   jax.ShapeDtypeStruct(big_shape, f32), jax.ShapeDtypeStruct((ng * nseq, NH, DH), f32),
                   jax.ShapeDtypeStruct((ng, 8, 128), f32), jax.ShapeDtypeStruct(big_shape, f32)],
        input_output_aliases={n_in: 2, n_in + 1: 5} if aliased else {},
        scratch_shapes=_mix_scratch(1, nseq),
        compiler_params=_cparams(("arbitrary",)),
        name="mixer_block",
    )(p, c0, n0, m0, s0, gb, al, *aliased)


def _outproj_rows(x_ref, ha_ref, hb_ref, po_ref, pz_ref, anw_ref, bnw_ref,
                  wm_ref, wpa_ref, wpb_ref, wo_ref, g_ref, b_ref):
    x = x_ref[...]
    mg = _sigmoid(_dot(x.astype(bf16), wm_ref[...]))
    cols = [slice(h * DH, (h + 1) * DH) for h in range(NH)]
    hr = [ha_ref[:, c] for c in cols]
    ob = [hb_ref[:, c] for c in cols]
    mu = [jnp.mean(v, axis=1, keepdims=True) for v in hr]
    osq = [jnp.mean(v * v, axis=1, keepdims=True) for v in ob]
    hc = [hr[h] - mu[h] for h in range(NH)]
    var = [jnp.mean(v * v, axis=1, keepdims=True) for v in hc]
    ha = [_sigmoid(po_ref[:, cols[h]]) * (hc[h] * lax.rsqrt(var[h] + NORM_EPS) * anw_ref[:, cols[h]])
          for h in range(NH)]
    hb = [(ob[h] * lax.rsqrt(osq[h] + NORM_EPS) * bnw_ref[...]) * _silu(pz_ref[:, cols[h]]) for h in range(NH)]
    ya = _dot(jnp.concatenate(ha, axis=1).astype(bf16), wpa_ref[...])
    yb = _dot(jnp.concatenate(hb, axis=1).astype(bf16), wpb_ref[...])
    y = mg[:, :D_MODEL] * ya + mg[:, D_MODEL:] * yb
    mix = _dot(y.astype(bf16), wo_ref[...])
    return _ln_rows(ALPHA * x + mix, g_ref[...], b_ref[...])


def _outproj_kernel(*refs):
    refs[-1][...] = _outproj_rows(*refs[:-1])


def _outproj_operands(x, ha, hb, p, anw, bnw, w_merge, layer, w_pa, w_pb, w_out, g, b, tm):
    row_spec = lambda w: pl.BlockSpec((tm, w), lambda i: (i, 0))
    pcol = lambda c0: pl.BlockSpec((tm, WB), lambda i: (i, c0 // WB))
    const = lambda shape: pl.BlockSpec(shape, lambda i: (0, 0))
    specs = [row_spec(D_MODEL), row_spec(WB), row_spec(WB), pcol(C_OA), pcol(C_Z),
             const((1, WB)), const((1, DH)),
             pl.BlockSpec((None, D_MODEL, 2 * D_MODEL), lambda i: (layer, 0, 0)),
             const((WB, D_MODEL)), const((WB, D_MODEL)),
             const((D_MODEL, D_MODEL)), const((1, D_MODEL)), const((1, D_MODEL))]
    return specs, (x, ha, hb, p, p, anw, bnw, w_merge, w_pa, w_pb, w_out, g, b)


def _outproj(x, ha, hb, p, anw, bnw, w_merge, layer, w_pa, w_pb, w_out, g, b):
    m = x.shape[0]
    tm = min(m, PROJ_TM)
    specs, args = _outproj_operands(x, ha, hb, p, anw, bnw, w_merge, layer, w_pa, w_pb, w_out, g, b, tm)
    return pl.pallas_call(
        _outproj_kernel,
        grid=(m // tm,),
        in_specs=specs,
        out_specs=pl.BlockSpec((tm, D_MODEL), lambda i: (i, 0)),
        out_shape=jax.ShapeDtypeStruct((m, D_MODEL), f32),
        compiler_params=_cparams(("arbitrary",)),
        name="merge_outproj_ln",
    )(*args)


def _ffn2_kernel(long_mode, tm, nmt, blk, valid, *refs):
    if long_mode:
        (x_ref, wa_ref, wb_ref, cwa_ref, cwb_ref, wd_ref, g_ref, b_ref, ha_ref, hb_ref,
         o_ref, sa_ref, sb_ref, hh_ref, ca_ref, cb_ref) = refs

        @pl.when(pl.program_id(0) % nmt == 0)
        def _():
            ca_ref[...] = ha_ref[0]
            cb_ref[...] = hb_ref[0]
    else:
        (x_ref, wa_ref, wb_ref, cwa_ref, cwb_ref, wd_ref, g_ref, b_ref, ea_ref, eb_ref,
         o_ref, sa_ref, sb_ref, hh_ref) = refs
        nsq = tm // blk
        r0 = blk - valid - (CONV_F - 1)
        row3 = lax.broadcasted_iota(jnp.int32, (nsq, blk, FT), 1)

        def merge_history(u, e_ref, s_ref, cols):
            u3 = u.reshape(nsq, blk, FT)
            e3 = e_ref[:, :, cols]
            for r in range(CONV_F - 1):
                u3 = jnp.where(row3 == r0 + r, e3[:, r:r + 1, :], u3)
            s_ref[:, :, cols] = u3[:, blk - (CONV_F - 1):, :]
            return u3.reshape(tm, FT)
    xb = x_ref[...].astype(bf16)
    for cb in range(NF):
        cols = slice(cb * FT, (cb + 1) * FT)
        ua = _dot(xb, wa_ref[:, cols])
        ub = _dot(xb, wb_ref[:, cols])
        if long_mode:
            h8a, h8b = ca_ref[:, cols], cb_ref[:, cols]
            ta, tb = ua[tm - HDR:tm], ub[tm - HDR:tm]
            ca_ref[:, cols] = ta
            cb_ref[:, cols] = tb
            sa_ref[0, :, cols] = ta
            sb_ref[0, :, cols] = tb
        else:
            h8a = h8b = None
            ua = merge_history(ua, ea_ref, sa_ref, cols)
            ub = merge_history(ub, eb_ref, sb_ref, cols)
        hh = _silu(_conv_rows(ua, h8a, cwa_ref[:, cols])) * _conv_rows(ub, h8b, cwb_ref[:, cols])
        hh_ref[:, cols] = hh.astype(bf16)
    out = _dot(hh_ref[...], wd_ref[...])
    o_ref[...] = _ln_rows(ALPHA * x_ref[...] + out, g_ref[...], b_ref[...])


def _ffn2(x, w_up, cw, w_down, layer, g, b, hist_a=None, hist_b=None,
          e_ab=None, nb=1, blk=CH, valid=CH):
    m = x.shape[0]
    long_mode = hist_a is not None
    seq = m // nb
    tm = min(seq, FFN_TM if long_mode else FFN_TM_BLOCK)
    nmt = seq // tm
    rows = lambda w: pl.BlockSpec((tm, w), lambda i: (i, 0))
    const = lambda shape: pl.BlockSpec(shape, lambda i: (0,) * len(shape))
    lsp = lambda shape, j: pl.BlockSpec((None,) + shape, lambda i: (layer, 0, j))
    in_specs = [rows(D_MODEL), lsp((D_MODEL, D_FF), 0), lsp((D_MODEL, D_FF), 1), lsp((CONV_F, D_FF), 0),
                lsp((CONV_F, D_FF), 1), lsp((D_FF, D_MODEL), 0), const((1, D_MODEL)), const((1, D_MODEL))]
    scratch = [pltpu.VMEM((tm, D_FF), bf16)]
    if long_mode:
        in_specs += [const((1, HDR, D_FF))] * 2
        st_spec = pl.BlockSpec((1, HDR, D_FF), lambda i: (i, 0, 0))
        st_shape = jax.ShapeDtypeStruct((m // tm, HDR, D_FF), f32)
        scratch += [pltpu.VMEM((HDR, D_FF), f32), pltpu.VMEM((HDR, D_FF), f32)]
        extra = (hist_a, hist_b)
    else:
        nsq = tm // blk
        if e_ab.ndim == 4:
            e_spec = lambda j: pl.BlockSpec((None, nsq, CONV_F - 1, D_FF), lambda i: (layer, i, 0, j))
        else:
            e_spec = lambda j: pl.BlockSpec((nsq, CONV_F - 1, D_FF), lambda i: (i, 0, j))
        in_specs += [e_spec(0), e_spec(1)]
        st_spec = pl.BlockSpec((nsq, CONV_F - 1, D_FF), lambda i: (i, 0, 0))
        st_shape = jax.ShapeDtypeStruct((m // blk, CONV_F - 1, D_FF), f32)
        extra = (e_ab, e_ab)
    args = (x, w_up, w_up, cw, cw, w_down, g, b) + tuple(extra)
    return pl.pallas_call(
        functools.partial(_ffn2_kernel, long_mode, tm, nmt, blk, valid),
        grid=(m // tm,),
        in_specs=in_specs,
        out_specs=[rows(D_MODEL), st_spec, st_spec],
        out_shape=[jax.ShapeDtypeStruct((m, D_MODEL), f32), st_shape, st_shape],
        scratch_shapes=scratch,
        compiler_params=_cparams(("arbitrary",)),
        name="conv_ffn_ln",
    )(*args)


def _mix_weights(w_in):
    w = w_in.astype(bf16)
    zc = jnp.zeros(w.shape[:2] + (128 - 4 * NH,), bf16)
    w_mix = jnp.concatenate([w[..., A_Q:A_I], w[..., B_Z:B_BETA], w[..., A_I:B_QKV], w[..., B_BETA:G_MERGE], zc,
                             w[..., B_QKV:B_Z]], axis=-1)
    return w_mix, w[..., G_MERGE:]


def _layer_weights(l, w_mix, w_merge, mlstm_gate_bias, mlstm_norm_w, gdn_conv_w, gdn_A_log, gdn_dt_bias, gdn_norm_w,
                   w_branch_a, w_branch_b, w_out, ln1_g, ln1_b, w_up, ffn_conv_w, w_down, ln2_g, ln2_b):
    z4 = jnp.zeros((NH,), f32)
    gb = jnp.concatenate([mlstm_gate_bias[l], z4, gdn_dt_bias[l], jnp.zeros((128 - 4 * NH,), f32)])[None]
    al = jnp.concatenate([z4, z4, z4, gdn_A_log[l], jnp.zeros((128 - 4 * NH,), f32)])[None]
    return dict(
        layer=l, w_mix=w_mix, w_merge=w_merge, gb=gb, al=al,
        anw=mlstm_norm_w[l][None], bnw=gdn_norm_w[l][None], cw=gdn_conv_w[l],
        w_pa=w_branch_a[l].astype(bf16), w_pb=w_branch_b[l].astype(bf16), w_out=w_out[l].astype(bf16),
        ln1_g=ln1_g[l][None], ln1_b=ln1_b[l][None],
        w_up=w_up, cw_f=ffn_conv_w, w_down=w_down, ln2_g=ln2_g[l][None], ln2_b=ln2_b[l][None])


def _block_layer(x, lw, st, blk, valid, layer=None, prev=None, ln=None):
    c0, n0, m0, s0, gbuf, fbuf = st
    nseq_tot = n0.shape[0]
    nseq = CH // blk
    ng = nseq_tot // nseq
    m0p = jnp.pad(m0.reshape(ng, nseq, NH), ((0, 0), (0, 8 - nseq), (0, 128 - NH)))
    if ln is None:
        p, ext = _inproj(x, lw["w_mix"], lw["layer"], lw["cw"], e=gbuf, blk=blk, valid=valid)
    else:
        p, ext, x = _inproj(x, lw["w_mix"], lw["layer"], lw["cw"], e=gbuf, blk=blk, valid=valid, ln=ln)
    ha, hb, c, n, m, s = _mixer_block(p, c0, n0, m0p, s0, lw["gb"], lw["al"], blk, valid, layer, prev)
    x1 = _outproj(x, ha, hb, p, lw["anw"], lw["bnw"], lw["w_merge"], lw["layer"], lw["w_pa"], lw["w_pb"],
                  lw["w_out"],
                  lw["ln1_g"], lw["ln1_b"])
    x2, ua, ub = _ffn2(x1, lw["w_up"], lw["cw_f"], lw["w_down"], lw["layer"],
                       lw["ln2_g"], lw["ln2_b"], e_ab=fbuf, blk=blk, valid=valid)
    m_new = m[:, :nseq, :NH].reshape(nseq_tot, NH)
    gconv = ext
    fconv = jnp.concatenate([ua, ub], axis=2)
    return x2, (c, n, m_new, s, gconv, fconv)


def _long_layer(x, lw, st, nb, seq, ln=None):
    c0, n0, m0, s0, gbuf, fbuf = st
    m0p = jnp.pad(m0.reshape(1, 1, NH), ((0, 0), (0, 7), (0, 128 - NH)))
    hist = jnp.pad(gbuf, ((0, 0), (HDR - (CONV_B - 1), 0), (0, 0)))
    hf = jnp.pad(fbuf, ((0, 0), (HDR - (CONV_F - 1), 0), (0, 0)))
    if ln is None:
        p, tails = _inproj(x, lw["w_mix"], lw["layer"], lw["cw"], hist=hist, nb=nb)
    else:
        p, tails, x = _inproj(x, lw["w_mix"], lw["layer"], lw["cw"], hist=hist, nb=nb, ln=ln)
    ha, hb, c, n, m, s = _mixer_long(p, c0, n0, m0p, s0, lw["gb"], lw["al"], nb, seq, MIX_T)
    x1 = _outproj(x, ha, hb, p, lw["anw"], lw["bnw"], lw["w_merge"], lw["layer"], lw["w_pa"], lw["w_pb"],
                  lw["w_out"], lw["ln1_g"], lw["ln1_b"])
    x2, sa, sb = _ffn2(x1, lw["w_up"], lw["cw_f"], lw["w_down"], lw["layer"],
                       lw["ln2_g"], lw["ln2_b"], hist_a=hf[:, :, :D_FF], hist_b=hf[:, :, D_FF:], nb=nb)
    last_tile = lambda a: a.reshape((nb, -1) + a.shape[1:])[:, -1]
    gconv = last_tile(tails)[:, HDR - (CONV_B - 1):]
    fconv = jnp.concatenate([last_tile(sa), last_tile(sb)], axis=2)[:, HDR - (CONV_F - 1):]
    return x2, (c, n, m[:, 0, :NH], s, gconv, fconv)


def kernel(x_prompt, x_sample, state_mlstm_C, state_mlstm_n, state_mlstm_m, state_gdn_S, state_gdn_conv, state_ffn_conv, meta_tokens, ln_emb_g, ln_emb_b, w_in, mlstm_gate_bias, mlstm_norm_w, gdn_conv_w, gdn_A_log, gdn_dt_bias, gdn_norm_w, w_branch_a, w_branch_b, w_out, ln1_g, ln1_b, w_up, ffn_conv_w, w_down, ln2_g, ln2_b):
    nb, seq, _ = x_prompt.shape
    ns, ls, _ = x_sample.shape
    sblk = 8
    assert seq % max(MIX_T, PROJ_TM, FFN_TM) == 0 and ls + CONV_B - 1 <= sblk and ns % (CH // sblk) == 0 and N_META + CONV_B - 1 <= CH
    w_mix, w_merge = _mix_weights(w_in)
    w_up_b16, w_down_b16 = w_up.astype(bf16), w_down.astype(bf16)
    lws =[_layer_weights(l, w_mix, w_merge, mlstm_gate_bias, mlstm_norm_w, gdn_conv_w, gdn_A_log, gdn_dt_bias,
                          gdn_norm_w, w_branch_a, w_branch_b, w_out, ln1_g, ln1_b, w_up_b16, ffn_conv_w,
                          w_down_b16, ln2_g, ln2_b) for l in range(DEPTH)]
    emb_ln = (ln_emb_g[None], ln_emb_b[None])

    xm = jnp.pad(meta_tokens, ((CH - N_META, 0), (0, 0)))
    xs = jnp.pad(x_sample, ((0, 0), (sblk - ls, 0), (0, 0))).reshape(ns * sblk, D_MODEL)
    xp = x_prompt.reshape(nb * seq, D_MODEL)

    zero_st = (jnp.zeros((1, NH, DH, DH), f32), jnp.zeros((1, NH, DH), f32), jnp.zeros((1, NH), f32),
               jnp.zeros((1, NH, DH, DH), f32), jnp.zeros((1, CONV_B - 1, 3 * WB), f32),
               jnp.zeros((1, CONV_F - 1, 2 * D_FF), f32))
    p_states, s_states = [], []
    big = None
    for l in range(DEPTH):
        ln = emb_ln if l == 0 else None
        xm, st_m = _block_layer(xm, lws[l], zero_st, CH, N_META, ln=ln)
        xp, st_p = _long_layer(xp, lws[l], st_m, nb, seq, ln=ln)
        samp_st = (state_mlstm_C, state_mlstm_n[l], state_mlstm_m[l], state_gdn_S,
                   state_gdn_conv[l], state_ffn_conv)
        xs, st_s = _block_layer(xs, lws[l], samp_st, sblk, ls, layer=l, prev=big, ln=ln)
        big = (st_s[0], st_s[3])
        p_states.append(st_p)
        s_states.append(st_s)

    stack = lambda states, i: jnp.stack([s[i] for s in states], axis=0)
    y_prompt = xp.reshape(nb, seq, D_MODEL)
    y_sample = xs.reshape(ns, sblk, D_MODEL)[:, sblk - ls:]
    return (y_prompt, y_sample,
            stack(p_states, 0), stack(p_states, 1), stack(p_states, 2), stack(p_states, 3),
            stack(p_states, 4), stack(p_states, 5),
            big[0], stack(s_states, 1), stack(s_states, 2), big[1],
            stack(s_states, 4), stack(s_states, 5))
```
